```python
import jax
import jax.numpy as jnp
from jax import lax
import numpy as np

D_MODEL = 2048
BATCH = 8
SEQ = 4096
DEPTH = 2

CONV_CH = 512
CONV_WIDTH = 31
GMLP_HEADS = 4
GMLP_HEAD_CH = 128
GMLP_CH = GMLP_HEADS * GMLP_HEAD_CH
CHUNK = 128
ATT_HEADS = 8
HEAD_DIM = 128
ATT_CH = ATT_HEADS * HEAD_DIM
Q_BLOCK = 128
MIX_WIDTH = CONV_CH + GMLP_CH + ATT_CH

OFF_CONV = 0
OFF_GMLP = OFF_CONV + 2 * CONV_CH
OFF_Q = OFF_GMLP + 2 * GMLP_CH
OFF_K = OFF_Q + ATT_CH
OFF_V = OFF_K + ATT_CH
OFF_F = OFF_V + ATT_CH
N_IN = OFF_F + ATT_HEADS

N_EXPERTS = 64
TOP_K = 8
N_GROUPS = 8
TOPK_GROUPS = 4
D_EXPERT = 512
D_SHARED = 512
ROUTE_SCALE = 2.5
MOE_BLOCK = 128

D_PLE = 256
ALPHA = (2.0 * DEPTH) ** 0.25
BETA = (8.0 * DEPTH) ** -0.25
LN_EPS = 1e-5

kernel_name = "hybrid_conv_gmlp_fox_moe_deepnorm"


def layer_norm(x, g, b):
    xf = x.astype(jnp.float32)
    mu = jnp.mean(xf, axis=-1, keepdims=True)
    var = jnp.mean(jnp.square(xf - mu), axis=-1, keepdims=True)
    return ((xf - mu) * lax.rsqrt(var + LN_EPS) * g.astype(jnp.float32) + b.astype(jnp.float32)).astype(x.dtype)


def rms_normalise(x):
    xf = x.astype(jnp.float32)
    return (xf * lax.rsqrt(jnp.mean(jnp.square(xf), axis=-1, keepdims=True) + LN_EPS)).astype(x.dtype)


def conv_module(z, conv_w, conv_b, ln_g, ln_b):
    a, gate = jnp.split(z, 2, axis=-1)
    h = a * jax.nn.sigmoid(gate)
    h = lax.conv_general_dilated(
        h, conv_w[:, None, :], window_strides=(1,),
        padding=[(CONV_WIDTH - 1, 0)],
        dimension_numbers=("NWC", "WIO", "NWC"),
        feature_group_count=CONV_CH) + conv_b
    h = layer_norm(h, ln_g, ln_b)
    return jax.nn.silu(h)


def gmlp_module(z, ln_g, ln_b, w_sp, b_sp):
    bsz, seq, _ = z.shape
    z = jax.nn.gelu(z)
    u, v = jnp.split(z, 2, axis=-1)
    v = layer_norm(v, ln_g, ln_b)
    v = v.reshape(bsz, seq // CHUNK, CHUNK, GMLP_HEADS, GMLP_HEAD_CH)
    causal = jnp.tril(jnp.ones((CHUNK, CHUNK), dtype=bool))
    ws = jnp.where(causal, w_sp, 0)
    mixed = jnp.einsum("hts,bnshc->bnthc", ws, v) + b_sp.T[None, None, :, :, None]
    return u * mixed.reshape(bsz, seq, GMLP_CH)


def fox_attention(q, k, v, f_logit):
    bsz, seq, nh, dh = q.shape
    c = jnp.cumsum(jax.nn.log_sigmoid(f_logit.astype(jnp.float32)), axis=1)
    c_t = c.transpose(0, 2, 1)
    nq = seq // Q_BLOCK
    qb = q.reshape(bsz, nq, Q_BLOCK, nh, dh).transpose(1, 0, 2, 3, 4)
    cb = c_t.reshape(bsz, nh, nq, Q_BLOCK).transpose(2, 0, 1, 3)
    kpos = jnp.arange(seq)
    scale = dh ** -0.5

    def block(args):
        i, qi, ci = args
        s = jnp.einsum("bqhd,bkhd->bhqk", qi, k, preferred_element_type=jnp.float32) * scale
        s = s + (ci[..., :, None] - c_t[..., None, :])
        qpos = i * Q_BLOCK + jnp.arange(Q_BLOCK)
        s = jnp.where(kpos[None, :] <= qpos[:, None], s, -jnp.inf)
        pr = jax.nn.softmax(s, axis=-1)
        return jnp.einsum("bhqk,bkhd->bqhd", pr.astype(v.dtype), v)

    out = lax.map(block, (jnp.arange(nq), qb, cb))
    return out.transpose(1, 0, 2, 3, 4).reshape(bsz, seq, nh * dh)


def mixing_sublayer(x, w_in, b_f, conv_w, conv_b, conv_ln_g, conv_ln_b,
                    gmlp_ln_g, gmlp_ln_b, w_sp, b_sp, out_g, w_o):
    bsz, seq, _ = x.shape
    z = x @ w_in
    y_a = conv_module(z[..., OFF_CONV:OFF_GMLP], conv_w, conv_b, conv_ln_g, conv_ln_b)
    y_b = gmlp_module(z[..., OFF_GMLP:OFF_Q], gmlp_ln_g, gmlp_ln_b, w_sp, b_sp)
    q = z[..., OFF_Q:OFF_K].reshape(bsz, seq, ATT_HEADS, HEAD_DIM)
    k = z[..., OFF_K:OFF_V].reshape(bsz, seq, ATT_HEADS, HEAD_DIM)
    v = z[..., OFF_V:OFF_F].reshape(bsz, seq, ATT_HEADS, HEAD_DIM)
    y_c = fox_attention(q, k, v, z[..., OFF_F:N_IN] + b_f)
    y = jnp.concatenate([rms_normalise(y_a), rms_normalise(y_b), rms_normalise(y_c)], axis=-1) * out_g
    return y @ w_o


def route(x2, w_r, r_bias):
    n_tok = x2.shape[0]
    scores = jax.nn.sigmoid((x2 @ w_r).astype(jnp.float32))
    sel = scores + r_bias.astype(jnp.float32)
    grp_score = lax.top_k(sel.reshape(n_tok, N_GROUPS, N_EXPERTS // N_GROUPS), 2)[0].sum(-1)
    _, gidx = lax.top_k(grp_score, TOPK_GROUPS)
    gmask = jnp.any(gidx[..., None] == jnp.arange(N_GROUPS), axis=1)
    emask = jnp.repeat(gmask, N_EXPERTS // N_GROUPS, axis=1)
    _, eidx = lax.top_k(jnp.where(emask, sel, -jnp.inf), TOP_K)
    gates = jnp.take_along_axis(scores, eidx, axis=1)
    gates = gates / jnp.sum(gates, axis=-1, keepdims=True) * ROUTE_SCALE
    return eidx, gates


def routed_experts(x2, eidx, gates, w_e1, w_e3, w_e2):
    n_tok, d = x2.shape
    n_asg = n_tok * TOP_K
    flat_e = eidx.reshape(n_asg)
    flat_tok = jnp.repeat(jnp.arange(n_tok, dtype=jnp.int32), TOP_K)
    flat_g = gates.reshape(n_asg)
    order = jnp.argsort(flat_e)
    se, stok, sg = flat_e[order], flat_tok[order], flat_g[order]
    counts = jnp.bincount(flat_e, length=N_EXPERTS)
    start = jnp.cumsum(counts) - counts
    padded = (counts + MOE_BLOCK - 1) // MOE_BLOCK * MOE_BLOCK
    pend = jnp.cumsum(padded)
    pstart = pend - padded
    dest = pstart[se] + jnp.arange(n_asg) - start[se]
    n_blk = -(-n_asg // MOE_BLOCK) + N_EXPERTS
    n_rows = n_blk * MOE_BLOCK
    row_tok = jnp.zeros((n_rows,), jnp.int32).at[dest].set(stok)
    row_gate = jnp.zeros((n_rows,), x2.dtype).at[dest].set(sg.astype(x2.dtype))
    blk_exp = jnp.minimum(jnp.searchsorted(pend, jnp.arange(n_blk) * MOE_BLOCK, side="right"), N_EXPERTS - 1)

    def step(acc, args):
        toks, g, e = args
        xb = x2[toks]
        h = jax.nn.silu(xb @ w_e1[e]) * (xb @ w_e3[e])
        return acc.at[toks].add((h @ w_e2[e]) * g[:, None]), None

    out, _ = lax.scan(step, jnp.zeros_like(x2),
                      (row_tok.reshape(n_blk, MOE_BLOCK), row_gate.reshape(n_blk, MOE_BLOCK), blk_exp))
    return out


def setup_inputs(seed: int = 0) -> dict:
    key = jax.random.key(seed)
    ks = jax.random.split(key, 32)

    def nrm(k, shape, scale):
        return jax.random.normal(k, shape, jnp.float32) * scale

    L, D = DEPTH, D_MODEL
    return {
        "x": nrm(ks[0], (BATCH, SEQ, D), 1.0),
        "p": nrm(ks[1], (DEPTH, BATCH, SEQ, D_PLE), 1.0),
        "w_in": nrm(ks[2], (L, D, N_IN), D ** -0.5),
        "b_f": 2.0 + nrm(ks[3], (L, ATT_HEADS), 0.5),
        "conv_w": nrm(ks[4], (L, CONV_WIDTH, CONV_CH), CONV_WIDTH ** -0.5),
        "conv_b": nrm(ks[5], (L, CONV_CH), 0.02),
        "conv_ln_g": 1.0 + nrm(ks[6], (L, CONV_CH), 0.02),
        "conv_ln_b": nrm(ks[7], (L, CONV_CH), 0.02),
        "gmlp_ln_g": 1.0 + nrm(ks[8], (L, GMLP_CH), 0.02),
        "gmlp_ln_b": nrm(ks[9], (L, GMLP_CH), 0.02),
        "w_sp": nrm(ks[10], (L, GMLP_HEADS, CHUNK, CHUNK), CHUNK ** -0.5),
        "b_sp": 1.0 + nrm(ks[11], (L, GMLP_HEADS, CHUNK), 0.1),
        "out_g": 1.0 + nrm(ks[12], (L, MIX_WIDTH), 0.02),
        "w_o": nrm(ks[13], (L, MIX_WIDTH, D), BETA * MIX_WIDTH ** -0.5),
        "ln1_g": 1.0 + nrm(ks[14], (L, D), 0.02),
        "ln1_b": nrm(ks[15], (L, D), 0.02),
        "w_r": nrm(ks[16], (L, D, N_EXPERTS), D ** -0.5),
        "r_bias": nrm(ks[17], (L, N_EXPERTS), 0.01),
        "w_e1": nrm(ks[18], (L, N_EXPERTS, D, D_EXPERT), D ** -0.5),
        "w_e3": nrm(ks[19], (L, N_EXPERTS, D, D_EXPERT), D ** -0.5),
        "w_e2": nrm(ks[20], (L, N_EXPERTS, D_EXPERT, D), BETA * D_EXPERT ** -0.5),
        "w_sh1": nrm(ks[21], (L, D, D_SHARED), D ** -0.5),
        "w_sh3": nrm(ks[22], (L, D, D_SHARED), D ** -0.5),
        "w_sh2": nrm(ks[23], (L, D_SHARED, D), BETA * D_SHARED ** -0.5),
        "w_pe": nrm(ks[24], (L, D_PLE, D), BETA * D_PLE ** -0.5),
        "w_pg": nrm(ks[25], (L, D, D), D ** -0.5),
        "b_pg": nrm(ks[26], (L, D), 0.02),
        "ln2_g": 1.0 + nrm(ks[27], (L, D), 0.02),
        "ln2_b": nrm(ks[28], (L, D), 0.02),
    }


def reference(x, p, w_in, b_f, conv_w, conv_b, conv_ln_g, conv_ln_b, gmlp_ln_g, gmlp_ln_b,
              w_sp, b_sp, out_g, w_o, ln1_g, ln1_b, w_r, r_bias, w_e1, w_e3, w_e2,
              w_sh1, w_sh3, w_sh2, w_pe, w_pg, b_pg, ln2_g, ln2_b):
    bsz, seq, d = x.shape
    for i in range(DEPTH):
        h = mixing_sublayer(x, w_in[i], b_f[i], conv_w[i], conv_b[i], conv_ln_g[i], conv_ln_b[i],
                            gmlp_ln_g[i], gmlp_ln_b[i], w_sp[i], b_sp[i], out_g[i], w_o[i])
        x = layer_norm(ALPHA * x + h, ln1_g[i], ln1_b[i])
        x2 = x.reshape(bsz * seq, d)
        eidx, gates = route(x2, w_r[i], r_bias[i])
        moe = routed_experts(x2, eidx, gates, w_e1[i], w_e3[i], w_e2[i])
        moe = moe + (jax.nn.silu(x2 @ w_sh1[i]) * (x2 @ w_sh3[i])) @ w_sh2[i]
        ple = jax.nn.sigmoid(x @ w_pg[i] + b_pg[i]) * (p[i] @ w_pe[i])
        x = layer_norm(ALPHA * x + moe.reshape(bsz, seq, d) + ple, ln2_g[i], ln2_b[i])
    return x
```

```python
import functools

import jax
import jax.numpy as jnp
from jax import lax
from jax.experimental import pallas as pl
from jax.experimental.pallas import tpu as pltpu

CONV_CH = 512
CONV_WIDTH = 31
GMLP_HEADS = 4
GMLP_HEAD_CH = 128
GMLP_CH = GMLP_HEADS * GMLP_HEAD_CH
CHUNK = 128
ATT_HEADS = 8
HEAD_DIM = 128
ATT_CH = ATT_HEADS * HEAD_DIM
N_EXPERTS = 64
TOP_K = 8
N_GROUPS = 8
GROUP_SIZE = N_EXPERTS // N_GROUPS
TOPK_GROUPS = 4
ROUTE_SCALE = 2.5
LN_EPS = 1e-5

LANES = 128
SUBLANES = 8
VMEM_BYTES_V7X = 64 * 1024 * 1024

CONV_HALO = 32
CONV_ROWS = 32
MOE_BLOCK = 256


def _vmem_limit(nbytes):
    return int(min(nbytes, VMEM_BYTES_V7X - 8 * 1024 * 1024))


def _params(semantics, vmem_bytes):
    return pltpu.CompilerParams(dimension_semantics=semantics, vmem_limit_bytes=_vmem_limit(vmem_bytes))


def _const_spec(shape):
    nd = len(shape)
    return pl.BlockSpec(shape, lambda *_: (0,) * nd, pipeline_mode=pl.Buffered(1))


def _layer_norm_rows(x, g, b):
    mu = jnp.mean(x, axis=-1, keepdims=True)
    xc = x - mu
    var = jnp.mean(xc * xc, axis=-1, keepdims=True)
    return xc * lax.rsqrt(var + LN_EPS) * g + b


def _rms_rows(x):
    return x * lax.rsqrt(jnp.mean(x * x, axis=-1, keepdims=True) + LN_EPS)


def _sigmoid(x):
    return 1.0 / (1.0 + jnp.exp(-x))


def _silu(x):
    return x * _sigmoid(x)


def _inproj_kernel(x_ref, w_ref, wf_ref, zc_ref, zg_ref, q_ref, k_ref, v_ref, f_ref, xb_ref, *, q_scale):
    j = pl.program_id(1)

    @pl.when(j == 0)
    def _():
        xb_ref[...] = x_ref[...].astype(jnp.bfloat16)
        f_ref[...] = jnp.dot(xb_ref[...], wf_ref[...], preferred_element_type=jnp.float32)

    z = jnp.dot(xb_ref[...], w_ref[...], preferred_element_type=jnp.float32)

    @pl.when(j == 0)
    def _():
        zc_ref[...] = z

    @pl.when(j == 1)
    def _():
        zg_ref[...] = z

    @pl.when(j == 2)
    def _():
        q_ref[...] = (z * q_scale).astype(jnp.bfloat16)

    @pl.when(j == 3)
    def _():
        k_ref[...] = z.astype(jnp.bfloat16)

    @pl.when(j == 4)
    def _():
        v_ref[...] = z.astype(jnp.bfloat16)


def _in_projection(x2, w_main, w_f, tm):
    n_tok, d = x2.shape
    wide = 2 * CONV_CH
    n_tiles = w_main.shape[1] // wide
    row = lambda i, j: (i, 0)
    out_shape = (
        jax.ShapeDtypeStruct((n_tok, wide), jnp.float32),
        jax.ShapeDtypeStruct((n_tok, wide), jnp.float32),
        jax.ShapeDtypeStruct((n_tok, ATT_CH), jnp.bfloat16),
        jax.ShapeDtypeStruct((n_tok, ATT_CH), jnp.bfloat16),
        jax.ShapeDtypeStruct((n_tok, ATT_CH), jnp.bfloat16),
        jax.ShapeDtypeStruct((n_tok, LANES), jnp.float32),
    )
    vmem = (2 * tm * d * 4 + tm * d * 2 + 2 * d * wide * 2 + d * LANES * 2
            + 2 * 2 * tm * wide * 4 + 3 * 2 * tm * wide * 2 + 2 * tm * LANES * 4 + 2 * tm * wide * 4)
    return pl.pallas_call(
        functools.partial(_inproj_kernel, q_scale=HEAD_DIM ** -0.5),
        grid=(n_tok // tm, n_tiles),
        in_specs=[
            pl.BlockSpec((tm, d), row),
            pl.BlockSpec((d, wide), lambda i, j: (0, j)),
            _const_spec((d, LANES)),
        ],
        out_specs=[
            pl.BlockSpec((tm, wide), row),
            pl.BlockSpec((tm, wide), row),
            pl.BlockSpec((tm, ATT_CH), row),
            pl.BlockSpec((tm, ATT_CH), row),
            pl.BlockSpec((tm, ATT_CH), row),
            pl.BlockSpec((tm, LANES), row),
        ],
        out_shape=out_shape,
        scratch_shapes=[pltpu.VMEM((tm, d), jnp.bfloat16)],
        compiler_params=_params(("arbitrary", "arbitrary"), vmem + (8 << 20)),
        name="in_projection",
    )(x2, w_main, w_f)


def _conv_kernel(z_ref, w_ref, cb_ref, lng_ref, lnb_ref, og_ref, o_ref, hbuf_ref, *, ts):
    s = pl.program_id(1)

    @pl.when(s == 0)
    def _():
        hbuf_ref[0:CONV_HALO, :] = jnp.zeros((CONV_HALO, CONV_CH), jnp.float32)

    @pl.when(s > 0)
    def _():
        hbuf_ref[0:CONV_HALO, :] = hbuf_ref[ts:ts + CONV_HALO, :]

    hbuf_ref[CONV_HALO:CONV_HALO + ts, :] = z_ref[:, :CONV_CH] * _sigmoid(z_ref[:, CONV_CH:])

    cb = cb_ref[...]
    lng = lng_ref[...]
    lnb = lnb_ref[...]
    og = og_ref[...]
    first = CONV_HALO - (CONV_WIDTH - 1)
    for c in range(ts // CONV_ROWS):
        acc = jnp.broadcast_to(cb, (CONV_ROWS, CONV_CH))
        for j in range(CONV_WIDTH):
            off = first + j + c * CONV_ROWS
            acc = acc + w_ref[j:j + 1, :] * hbuf_ref[off:off + CONV_ROWS, :]
        y = _silu(_layer_norm_rows(acc, lng, lnb))
        o_ref[c * CONV_ROWS:(c + 1) * CONV_ROWS, :] = (_rms_rows(y) * og).astype(jnp.bfloat16)


def _conv_module(zc, conv_w, conv_b, ln_g, ln_b, og, bsz, seq, ts):
    n_tok = zc.shape[0]
    n_s = seq // ts
    vec = lambda v: v.reshape(1, CONV_CH)
    return pl.pallas_call(
        functools.partial(_conv_kernel, ts=ts),
        grid=(bsz, n_s),
        in_specs=[
            pl.BlockSpec((ts, 2 * CONV_CH), lambda b, s: (b * n_s + s, 0)),
            _const_spec((CONV_WIDTH, CONV_CH)),
            _const_spec((1, CONV_CH)),
            _const_spec((1, CONV_CH)),
            _const_spec((1, CONV_CH)),
            _const_spec((1, CONV_CH)),
        ],
        out_specs=pl.BlockSpec((ts, CONV_CH), lambda b, s: (b * n_s + s, 0)),
        out_shape=jax.ShapeDtypeStruct((n_tok, CONV_CH), jnp.bfloat16),
        scratch_shapes=[pltpu.VMEM((CONV_HALO + ts, CONV_CH), jnp.float32)],
        compiler_params=_params(("arbitrary", "arbitrary"), 32 << 20),
        name="conv_module",
    )(zc, conv_w, vec(conv_b), vec(ln_g), vec(ln_b), vec(og))


def _gelu_tanh(x):
    c = (2.0 / jnp.pi) ** 0.5
    return 0.5 * x * (1.0 + jnp.tanh(c * (x + 0.044715 * (x * x * x))))


def _gmlp_kernel(z_ref, lng_ref, lnb_ref, wsp_ref, bsp_ref, og_ref, o_ref, y_ref, *, tg):
    z = _gelu_tanh(z_ref[...])
    u = z[:, :GMLP_CH]
    v = _layer_norm_rows(z[:, GMLP_CH:], lng_ref[...], lnb_ref[...]).astype(jnp.bfloat16)
    t_idx = lax.broadcasted_iota(jnp.int32, (CHUNK, CHUNK), 0)
    s_idx = lax.broadcasted_iota(jnp.int32, (CHUNK, CHUNK), 1)
    causal = s_idx <= t_idx
    for h in range(GMLP_HEADS):
        ws = jnp.where(causal, wsp_ref[h], 0.0).astype(jnp.bfloat16)
        bias = bsp_ref[h]
        cols = slice(h * GMLP_HEAD_CH, (h + 1) * GMLP_HEAD_CH)
        for c in range(tg // CHUNK):
            rows = slice(c * CHUNK, (c + 1) * CHUNK)
            mixed = jnp.dot(ws, v[rows, cols], preferred_element_type=jnp.float32) + bias
            y_ref[rows, cols] = u[rows, cols] * mixed
    o_ref[...] = (_rms_rows(y_ref[...]) * og_ref[...]).astype(jnp.bfloat16)


def _gmlp_module(zg, ln_g, ln_b, w_sp, b_sp, og, tg):
    n_tok = zg.shape[0]
    vec = lambda v: v.reshape(1, GMLP_CH)
    return pl.pallas_call(
        functools.partial(_gmlp_kernel, tg=tg),
        grid=(n_tok // tg,),
        in_specs=[
            pl.BlockSpec((tg, 2 * GMLP_CH), lambda i: (i, 0)),
            _const_spec((1, GMLP_CH)),
            _const_spec((1, GMLP_CH)),
            _const_spec((GMLP_HEADS, CHUNK, CHUNK)),
            _const_spec((GMLP_HEADS, CHUNK, 1)),
            _const_spec((1, GMLP_CH)),
        ],
        out_specs=pl.BlockSpec((tg, GMLP_CH), lambda i: (i, 0)),
        out_shape=jax.ShapeDtypeStruct((n_tok, GMLP_CH), jnp.bfloat16),
        scratch_shapes=[pltpu.VMEM((tg, GMLP_CH), jnp.float32)],
        compiler_params=_params(("arbitrary",), 32 << 20),
        name="gmlp_module",
    )(zg, vec(ln_g), vec(ln_b), w_sp, b_sp.reshape(GMLP_HEADS, CHUNK, 1), vec(og))


def _split3(x):
    hi = x.astype(jnp.bfloat16)
    r1 = x - hi.astype(jnp.float32)
    mid = r1.astype(jnp.bfloat16)
    lo = (r1 - mid.astype(jnp.float32)).astype(jnp.bfloat16)
    return hi, mid, lo


def _fcum_kernel(f_ref, bf_ref, c_ref, ct_ref, carry_ref, *, ts):
    s = pl.program_id(1)

    @pl.when(s == 0)
    def _():
        carry_ref[...] = jnp.zeros_like(carry_ref)

    x = f_ref[...] + bf_ref[...]
    ls = -(jnp.maximum(-x, 0.0) + jnp.log(1.0 + jnp.exp(-jnp.abs(x))))
    t_idx = lax.broadcasted_iota(jnp.int32, (ts, ts), 0)
    s_idx = lax.broadcasted_iota(jnp.int32, (ts, ts), 1)
    tri = jnp.where(s_idx <= t_idx, 1.0, 0.0).astype(jnp.bfloat16)
    hi, mid, lo = _split3(ls)
    c = (jnp.dot(tri, lo, preferred_element_type=jnp.float32)
         + jnp.dot(tri, mid, preferred_element_type=jnp.float32)
         + jnp.dot(tri, hi, preferred_element_type=jnp.float32)) + carry_ref[...]
    c_ref[...] = c
    ct_ref[0] = c.T[:ATT_HEADS, :]
    carry_ref[...] = c[ts - 1:ts, :]


def _forget_cumsum(f, b_f_row, bsz, seq, ts):
    n_tok = f.shape[0]
    n_s = seq // ts
    return pl.pallas_call(
        functools.partial(_fcum_kernel, ts=ts),
        grid=(bsz, n_s),
        in_specs=[
            pl.BlockSpec((ts, LANES), lambda b, s: (b * n_s + s, 0)),
            _const_spec((1, LANES)),
        ],
        out_specs=[
            pl.BlockSpec((ts, LANES), lambda b, s: (b * n_s + s, 0)),
            pl.BlockSpec((1, ATT_HEADS, ts), lambda b, s: (b, 0, s)),
        ],
        out_shape=(
            jax.ShapeDtypeStruct((n_tok, LANES), jnp.float32),
            jax.ShapeDtypeStruct((bsz, ATT_HEADS, seq), jnp.float32),
        ),
        scratch_shapes=[pltpu.VMEM((1, LANES), jnp.float32)],
        compiler_params=_params(("arbitrary", "arbitrary"), 32 << 20),
        name="forget_cumsum",
    )(f, b_f_row)


def _fox_kernel(q_ref, k_ref, v_ref, c_ref, ct_ref, o_ref, m_ref, l_ref, acc_ref, *, tq):
    h = pl.program_id(1)
    i = pl.program_id(2)
    lane = lax.broadcasted_iota(jnp.int32, (tq, LANES), 1)
    cq = jnp.sum(jnp.where(lane == h, c_ref[...], 0.0), axis=1, keepdims=True)
    q = q_ref[...]
    m_ref[...] = jnp.full(m_ref.shape, -jnp.inf, jnp.float32)
    l_ref[...] = jnp.zeros_like(l_ref)
    acc_ref[...] = jnp.zeros_like(acc_ref)

    def step(j, on_diagonal):
        k0 = pl.multiple_of(j * tq, tq)
        k = k_ref[pl.ds(k0, tq), :]
        v = v_ref[pl.ds(k0, tq), :]
        ck = ct_ref[0, :, pl.ds(k0, tq)]
        s = lax.dot_general(q, k, (((1,), (1,)), ((), ())), preferred_element_type=jnp.float32)
        s = s + (cq - ck)
        if on_diagonal:
            row = lax.broadcasted_iota(jnp.int32, (tq, tq), 0)
            col = lax.broadcasted_iota(jnp.int32, (tq, tq), 1)
            s = jnp.where(col <= row, s, -jnp.inf)
        m_prev = m_ref[...]
        m_new = jnp.maximum(m_prev, jnp.max(s, axis=1, keepdims=True))
        alpha = jnp.exp(m_prev - m_new)
        p = jnp.exp(s - m_new)
        l_ref[...] = alpha * l_ref[...] + jnp.sum(p, axis=1, keepdims=True)
        acc_ref[...] = alpha * acc_ref[...] + jnp.dot(p.astype(jnp.bfloat16), v, preferred_element_type=jnp.float32)
        m_ref[...] = m_new

    def body(j, carry):
        step(j, False)
        return carry

    lax.fori_loop(0, i, body, 0)
    step(i, True)
    o_ref[...] = (acc_ref[...] / l_ref[...]).astype(jnp.bfloat16)


def _fox_attention(q, k, v, c, ct, bsz, seq, tq):
    n_tok = q.shape[0]
    nq = seq // tq
    return pl.pallas_call(
        functools.partial(_fox_kernel, tq=tq),
        grid=(bsz, ATT_HEADS, nq),
        in_specs=[
            pl.BlockSpec((tq, HEAD_DIM), lambda b, h, i: (b * nq + i, h)),
            pl.BlockSpec((seq, HEAD_DIM), lambda b, h, i: (b, h)),
            pl.BlockSpec((seq, HEAD_DIM), lambda b, h, i: (b, h)),
            pl.BlockSpec((tq, LANES), lambda b, h, i: (b * nq + i, 0)),
            pl.BlockSpec((1, 1, seq), lambda b, h, i: (b * ATT_HEADS + h, 0, 0)),
        ],
        out_specs=pl.BlockSpec((tq, HEAD_DIM), lambda b, h, i: (b * nq + i, h)),
        out_shape=jax.ShapeDtypeStruct((n_tok, ATT_CH), jnp.bfloat16),
        scratch_shapes=[
            pltpu.VMEM((tq, 1), jnp.float32),
            pltpu.VMEM((tq, 1), jnp.float32),
            pltpu.VMEM((tq, HEAD_DIM), jnp.float32),
        ],
        compiler_params=_params(("arbitrary", "arbitrary", "arbitrary"), 40 << 20),
        name="fox_attention",
    )(q, k, v, c, ct)


def _pack_bf16_pairs(x):
    n = x.shape[1] // 2
    r = x.astype(jnp.bfloat16).astype(jnp.float32)
    lo = lax.bitcast_convert_type(r[:, :n], jnp.uint32)
    hi = lax.bitcast_convert_type(r[:, n:], jnp.uint32)
    return (lo >> 16) | (hi & jnp.uint32(0xFFFF0000))


def _unpack_bf16_pairs(w):
    lo = lax.bitcast_convert_type(w << 16, jnp.float32).astype(jnp.bfloat16)
    hi = lax.bitcast_convert_type(w & jnp.uint32(0xFFFF0000), jnp.float32).astype(jnp.bfloat16)
    return lo, hi


def _outproj_kernel(ya_ref, yb_ref, yc_ref, x_ref, wo_ref, ogc_ref, g_ref, b_ref, x1_ref, x1p_ref, y_ref, *, alpha):
    y_ref[:, 0:CONV_CH] = ya_ref[...]
    y_ref[:, CONV_CH:CONV_CH + GMLP_CH] = yb_ref[...]
    yc = yc_ref[...].astype(jnp.float32)
    y_ref[:, CONV_CH + GMLP_CH:] = (_rms_rows(yc) * ogc_ref[...]).astype(jnp.bfloat16)
    h = jnp.dot(y_ref[...], wo_ref[...], preferred_element_type=jnp.float32)
    x1 = _layer_norm_rows(alpha * x_ref[...] + h, g_ref[...], b_ref[...])
    x1_ref[...] = x1
    x1p_ref[...] = _pack_bf16_pairs(x1)


def _out_projection(ya, yb, yc, x2, w_o, og_c, ln_g, ln_b, alpha, tm):
    n_tok, d = x2.shape
    mix = w_o.shape[0]
    row = lambda i: (i, 0)
    vmem = (mix * d * 2 + 2 * tm * d * 4 * 2 + 2 * tm * (d // 2) * 4 + 2 * tm * mix * 2 + tm * mix * 2 + 3 * tm * d * 4)
    return pl.pallas_call(
        functools.partial(_outproj_kernel, alpha=alpha),
        grid=(n_tok // tm,),
        in_specs=[
            pl.BlockSpec((tm, CONV_CH), row),
            pl.BlockSpec((tm, GMLP_CH), row),
            pl.BlockSpec((tm, ATT_CH), row),
            pl.BlockSpec((tm, d), row),
            _const_spec((mix, d)),
            _const_spec((1, ATT_CH)),
            _const_spec((1, d)),
            _const_spec((1, d)),
        ],
        out_specs=[pl.BlockSpec((tm, d), row), pl.BlockSpec((tm, d // 2), row)],
        out_shape=(
            jax.ShapeDtypeStruct((n_tok, d), jnp.float32),
            jax.ShapeDtypeStruct((n_tok, d // 2), jnp.uint32),
        ),
        scratch_shapes=[pltpu.VMEM((tm, mix), jnp.bfloat16)],
        compiler_params=_params(("arbitrary",), vmem + (8 << 20)),
        name="out_projection",
    )(ya, yb, yc, x2, w_o, og_c.reshape(1, ATT_CH), ln_g.reshape(1, d), ln_b.reshape(1, d))


def _first_argmax_rows(x, idx):
    m = jnp.max(x, axis=0, keepdims=True)
    first = jnp.min(jnp.where(x == m, idx, x.shape[0]), axis=0, keepdims=True)
    return m, first


def _router_kernel(x_ref, wh_ref, wl_ref, rb_ref, eidx_ref, gate_t_ref, rank_ref, cnt_ref, carry_ref, *, tr):
    i = pl.program_id(0)

    @pl.when(i == 0)
    def _():
        carry_ref[...] = jnp.zeros_like(carry_ref)

    x = x_ref[...]
    xh = x.astype(jnp.bfloat16)
    xl = (x - xh.astype(jnp.float32)).astype(jnp.bfloat16)
    nt = (((1,), (1,)), ((), ()))
    wh = wh_ref[...]
    logits = (lax.dot_general(wl_ref[...], xh, nt, preferred_element_type=jnp.float32)
              + lax.dot_general(wh, xl, nt, preferred_element_type=jnp.float32)
              + lax.dot_general(wh, xh, nt, preferred_element_type=jnp.float32))
    scores = _sigmoid(logits)
    sel = scores + rb_ref[...]

    neg = -jnp.inf
    e_idx = lax.broadcasted_iota(jnp.int32, (N_EXPERTS, tr), 0)
    g_idx = lax.broadcasted_iota(jnp.int32, (N_GROUPS, tr), 0)
    in_idx = lax.broadcasted_iota(jnp.int32, (GROUP_SIZE, tr), 0)

    gs_rows = []
    for g in range(N_GROUPS):
        blk = sel[g * GROUP_SIZE:(g + 1) * GROUP_SIZE, :]
        m1, a1 = _first_argmax_rows(blk, in_idx)
        m2 = jnp.max(jnp.where(in_idx == a1, neg, blk), axis=0, keepdims=True)
        gs_rows.append(m1 + m2)
    gs = jnp.concatenate(gs_rows, axis=0)

    gsel = jnp.zeros((N_GROUPS, tr), jnp.float32)
    for _ in range(TOPK_GROUPS):
        _, a = _first_argmax_rows(gs, g_idx)
        pick = g_idx == a
        gsel = jnp.where(pick, 1.0, gsel)
        gs = jnp.where(pick, neg, gs)
    esel = jnp.concatenate(
        [jnp.broadcast_to(gsel[g:g + 1, :], (GROUP_SIZE, tr)) for g in range(N_GROUPS)], axis=0)
    cand = jnp.where(esel > 0.5, sel, neg)

    picks, gates = [], []
    chosen = jnp.zeros((N_EXPERTS, tr), jnp.float32)
    for _ in range(TOP_K):
        _, a = _first_argmax_rows(cand, e_idx)
        pick = e_idx == a
        picks.append(a)
        gates.append(jnp.sum(jnp.where(pick, scores, 0.0), axis=0, keepdims=True))
        chosen = jnp.where(pick, 1.0, chosen)
        cand = jnp.where(pick, neg, cand)
    gate = jnp.concatenate(gates, axis=0)
    gate = gate / jnp.sum(gate, axis=0, keepdims=True) * ROUTE_SCALE

    r_idx = lax.broadcasted_iota(jnp.int32, (tr, tr), 0)
    c_idx = lax.broadcasted_iota(jnp.int32, (tr, tr), 1)
    upper = jnp.where(r_idx <= c_idx, 1.0, 0.0).astype(jnp.bfloat16)
    incl = jnp.dot(chosen.astype(jnp.bfloat16), upper, preferred_element_type=jnp.float32)
    rank_all = carry_ref[...] + incl - chosen
    carry_ref[...] = carry_ref[...] + incl[:, tr - 1:tr]
    ranks = [jnp.sum(jnp.where(e_idx == a, rank_all, 0.0), axis=0, keepdims=True) for a in picks]

    eidx_ref[...] = jnp.concatenate(picks, axis=0)
    rank_ref[...] = jnp.concatenate(ranks, axis=0).astype(jnp.int32)
    pad = jnp.zeros((LANES - TOP_K, tr), jnp.float32)
    gate_t_ref[...] = jnp.concatenate([gate, pad], axis=0).T
    cnt_ref[...] = jnp.broadcast_to(carry_ref[...], (N_EXPERTS, LANES)).astype(jnp.int32)


def _router(x1, w_r, r_bias, tr):
    n_tok, d = x1.shape
    wt = w_r.T
    wh = wt.astype(jnp.bfloat16)
    wl = (wt - wh.astype(jnp.float32)).astype(jnp.bfloat16)
    col = lambda i: (0, i)
    return pl.pallas_call(
        functools.partial(_router_kernel, tr=tr),
        grid=(n_tok // tr,),
        in_specs=[
            pl.BlockSpec((tr, d), lambda i: (i, 0)),
            _const_spec((N_EXPERTS, d)),
            _const_spec((N_EXPERTS, d)),
            _const_spec((N_EXPERTS, 1)),
        ],
        out_specs=[
            pl.BlockSpec((TOP_K, tr), col),
            pl.BlockSpec((tr, LANES), lambda i: (i, 0)),
            pl.BlockSpec((TOP_K, tr), col),
            pl.BlockSpec((N_EXPERTS, LANES), lambda i: (0, 0)),
        ],
        out_shape=(
            jax.ShapeDtypeStruct((TOP_K, n_tok), jnp.int32),
            jax.ShapeDtypeStruct((n_tok, LANES), jnp.float32),
            jax.ShapeDtypeStruct((TOP_K, n_tok), jnp.int32),
            jax.ShapeDtypeStruct((N_EXPERTS, LANES), jnp.int32),
        ),
        scratch_shapes=[pltpu.VMEM((N_EXPERTS, 1), jnp.float32)],
        compiler_params=_params(("arbitrary",), 32 << 20),
        name="router",
    )(x1, wh, wl, r_bias.reshape(N_EXPERTS, 1))


def _dest_kernel(pstart_ref, eidx_ref, rank_ref, dest_ref):
    e = eidx_ref[...]
    base = jnp.zeros(e.shape, jnp.int32)
    for k in range(N_EXPERTS):
        base = jnp.where(e == k, pstart_ref[k], base)
    dest_ref[...] = base + rank_ref[...]


def _dest_rows(pstart, eidx, rank, tc):
    n_tok = eidx.shape[1]
    col = lambda i, ps: (0, i)
    return pl.pallas_call(
        _dest_kernel,
        grid_spec=pltpu.PrefetchScalarGridSpec(
            num_scalar_prefetch=1,
            grid=(n_tok // tc,),
            in_specs=[pl.BlockSpec((TOP_K, tc), col), pl.BlockSpec((TOP_K, tc), col)],
            out_specs=pl.BlockSpec((TOP_K, tc), col),
        ),
        out_shape=jax.ShapeDtypeStruct((TOP_K, n_tok), jnp.int32),
        compiler_params=_params(("arbitrary",), 32 << 20),
        name="dest_rows",
    )(pstart, eidx, rank)


def _dispatch_kernel(pend_ref, dest_ref, x_ref, xs_ref, zero_ref, sem, *, td, bm):
    i = pl.program_id(0)

    @pl.when(i == 0)
    def _():
        zero_ref[...] = jnp.zeros_like(zero_ref)
        for e in range(N_EXPERTS):
            prev = pend_ref[e - 1] if e > 0 else 0

            @pl.when(pend_ref[e] > prev)
            def _():
                start = pl.multiple_of(pend_ref[e] - bm, bm)
                pltpu.make_async_copy(zero_ref, xs_ref.at[pl.ds(start, bm), :], sem).start()
        first_unused = pend_ref[N_EXPERTS - 1] // bm
        n_blk = xs_ref.shape[0] // bm

        def fill(b, carry):
            pltpu.make_async_copy(zero_ref, xs_ref.at[pl.ds(pl.multiple_of(b * bm, bm), bm), :], sem).start()
            return carry

        def fill_done(b, carry):
            pltpu.make_async_copy(zero_ref, xs_ref.at[pl.ds(0, bm), :], sem).wait()
            return carry

        lax.fori_loop(first_unused, n_blk, fill, 0)
        lax.fori_loop(first_unused, n_blk, fill_done, 0)
        for e in range(N_EXPERTS):
            prev = pend_ref[e - 1] if e > 0 else 0

            @pl.when(pend_ref[e] > prev)
            def _():
                pltpu.make_async_copy(zero_ref, xs_ref.at[pl.ds(0, bm), :], sem).wait()

    def issue(r, carry):
        for j in range(TOP_K):
            d = dest_ref[j, r]
            pltpu.make_async_copy(x_ref.at[pl.ds(r, 1), :], xs_ref.at[pl.ds(d, 1), :], sem).start()
        return carry

    lax.fori_loop(0, td, issue, 0)
    for j in range(TOP_K):
        pltpu.make_async_copy(x_ref, xs_ref.at[pl.ds(0, td), :], sem).wait()


def _dispatch(pend, dest, x1p, n_rows, td, bm):
    n_tok, width = x1p.shape
    return pl.pallas_call(
        functools.partial(_dispatch_kernel, td=td, bm=bm),
        grid_spec=pltpu.PrefetchScalarGridSpec(
            num_scalar_prefetch=1,
            grid=(n_tok // td,),
            in_specs=[
                pl.BlockSpec((TOP_K, td), lambda i, pe: (0, i), memory_space=pltpu.SMEM),
                pl.BlockSpec((td, width), lambda i, pe: (i, 0)),
            ],
            out_specs=pl.BlockSpec(memory_space=pl.ANY),
            scratch_shapes=[pltpu.VMEM((bm, width), jnp.uint32), pltpu.SemaphoreType.DMA],
        ),
        out_shape=jax.ShapeDtypeStruct((n_rows, width), jnp.uint32),
        compiler_params=_params(("arbitrary",), 32 << 20),
        name="dispatch",
    )(pend, dest, x1p)


def _experts_kernel(be_ref, nv_ref, xs_ref, w1_ref, w3_ref, w2_ref, ys_ref, xb_ref):
    b = pl.program_id(0)

    @pl.when(b < nv_ref[0])
    def _():
        half = xs_ref.shape[1]
        lo, hi = _unpack_bf16_pairs(xs_ref[...])
        xb_ref[:, :half] = lo
        xb_ref[:, half:] = hi
        xb = xb_ref[...]
        h1 = jnp.dot(xb, w1_ref[0], preferred_element_type=jnp.float32)
        h3 = jnp.dot(xb, w3_ref[0], preferred_element_type=jnp.float32)
        h = (_silu(h1) * h3).astype(jnp.bfloat16)
        ys_ref[...] = jnp.dot(h, w2_ref[0], preferred_element_type=jnp.float32)

    @pl.when(b >= nv_ref[0])
    def _():
        ys_ref[...] = jnp.zeros_like(ys_ref)


def _experts(blk_exp, n_valid, xs, w1, w3, w2, bm):
    n_rows, half = xs.shape
    d = 2 * half
    de = w1.shape[2]
    n_blk = n_rows // bm
    blk = lambda b, be, nv: (jnp.minimum(b, nv[0] - 1), 0)
    wsel = lambda b, be, nv: (be[jnp.minimum(b, nv[0] - 1)], 0, 0)
    vmem = 2 * bm * half * 4 + bm * d * 2 + 2 * 3 * d * de * 2 + 2 * bm * d * 4 + 4 * bm * de * 4 + bm * d * 4
    return pl.pallas_call(
        _experts_kernel,
        grid_spec=pltpu.PrefetchScalarGridSpec(
            num_scalar_prefetch=2,
            grid=(n_blk,),
            in_specs=[
                pl.BlockSpec((bm, half), blk),
                pl.BlockSpec((1, d, de), wsel),
                pl.BlockSpec((1, d, de), wsel),
                pl.BlockSpec((1, de, d), wsel),
            ],
            out_specs=pl.BlockSpec((bm, d), lambda b, be, nv: (b, 0)),
            scratch_shapes=[pltpu.VMEM((bm, d), jnp.bfloat16)],
        ),
        out_shape=jax.ShapeDtypeStruct((n_rows, d), jnp.float32),
        compiler_params=_params(("arbitrary",), vmem + (8 << 20)),
        name="experts",
    )(blk_exp, n_valid, xs, w1, w3, w2)


def _combine_kernel(dest_ref, x1_ref, p_ref, gt_ref, w1_ref, w3_ref, w2_ref, wpg_ref, bpg_ref, wpe_ref, g_ref, b_ref,
                    ys_ref, o_ref, ybuf_ref, sem, *, tm, alpha):
    def issue(r, carry):
        for j in range(TOP_K):
            d = dest_ref[j, r]
            pltpu.make_async_copy(ys_ref.at[pl.ds(d, 1), :], ybuf_ref.at[j, pl.ds(r, 1), :], sem).start()
        return carry

    lax.fori_loop(0, tm, issue, 0)

    x1 = x1_ref[...]
    xb = x1.astype(jnp.bfloat16)
    h1 = jnp.dot(xb, w1_ref[...], preferred_element_type=jnp.float32)
    h3 = jnp.dot(xb, w3_ref[...], preferred_element_type=jnp.float32)
    h = (_silu(h1) * h3).astype(jnp.bfloat16)
    total = alpha * x1 + jnp.dot(h, w2_ref[...], preferred_element_type=jnp.float32)
    gate = _sigmoid(jnp.dot(xb, wpg_ref[...], preferred_element_type=jnp.float32) + bpg_ref[...])
    pe = jnp.dot(p_ref[...].astype(jnp.bfloat16), wpe_ref[...], preferred_element_type=jnp.float32)
    total = total + gate * pe

    for j in range(TOP_K):
        pltpu.make_async_copy(ys_ref.at[pl.ds(0, tm), :], ybuf_ref.at[j], sem).wait()
    gt = gt_ref[...]
    for j in range(TOP_K):
        total = total + gt[:, j:j + 1] * ybuf_ref[j]
    o_ref[...] = _layer_norm_rows(total, g_ref[...], b_ref[...])


def _combine(dest, x1, p2, gate_t, w_sh1, w_sh3, w_sh2, w_pg, b_pg, w_pe, ln_g, ln_b, ys, alpha, tm):
    n_tok, d = x1.shape
    ds = w_sh1.shape[1]
    dp = p2.shape[1]
    row = lambda i: (i, 0)
    vmem = (TOP_K * tm * d * 4 + (3 * d * ds + d * d + dp * d) * 2 + 2 * 2 * tm * d * 4 + 2 * tm * dp * 4
            + 2 * tm * LANES * 4 + 6 * tm * d * 4)
    return pl.pallas_call(
        functools.partial(_combine_kernel, tm=tm, alpha=alpha),
        grid=(n_tok // tm,),
        in_specs=[
            pl.BlockSpec((TOP_K, tm), lambda i: (0, i), memory_space=pltpu.SMEM),
            pl.BlockSpec((tm, d), row),
            pl.BlockSpec((tm, dp), row),
            pl.BlockSpec((tm, LANES), row),
            _const_spec((d, ds)),
            _const_spec((d, ds)),
            _const_spec((ds, d)),
            _const_spec((d, d)),
            _const_spec((1, d)),
            _const_spec((dp, d)),
            _const_spec((1, d)),
            _const_spec((1, d)),
            pl.BlockSpec(memory_space=pl.ANY),
        ],
        out_specs=pl.BlockSpec((tm, d), row),
        out_shape=jax.ShapeDtypeStruct((n_tok, d), jnp.float32),
        scratch_shapes=[pltpu.VMEM((TOP_K, tm, d), jnp.float32), pltpu.SemaphoreType.DMA],
        compiler_params=_params(("arbitrary",), vmem + (8 << 20)),
        name="combine",
    )(dest, x1, p2, gate_t, w_sh1, w_sh3, w_sh2, w_pg, b_pg.reshape(1, d), w_pe, ln_g.reshape(1, d),
      ln_b.reshape(1, d), ys)


def _tile(n, target):
    t = min(n, target)
    assert n % t == 0, (n, t)
    return t


def kernel(x, p, w_in, b_f, conv_w, conv_b, conv_ln_g, conv_ln_b, gmlp_ln_g, gmlp_ln_b, w_sp, b_sp, out_g, w_o,
           ln1_g, ln1_b, w_r, r_bias, w_e1, w_e3, w_e2, w_sh1, w_sh3, w_sh2, w_pe, w_pg, b_pg, ln2_g, ln2_b):
    bsz, seq, d = x.shape
    depth = w_in.shape[0]
    n_tok = bsz * seq
    alpha = (2.0 * depth) ** 0.25
    n_main = 2 * CONV_CH + 2 * GMLP_CH + 3 * ATT_CH
    assert w_in.shape[2] == n_main + ATT_HEADS and seq % CHUNK == 0

    bm = MOE_BLOCK
    n_blk = -(-n_tok * TOP_K // bm) + N_EXPERTS
    n_rows = n_blk * bm
    bf16 = jnp.bfloat16

    x2 = x.reshape(n_tok, d)
    for i in range(depth):
        w_main = w_in[i, :, :n_main].astype(bf16)
        w_f = jnp.pad(w_in[i, :, n_main:], ((0, 0), (0, LANES - ATT_HEADS))).astype(bf16)
        b_f_row = jnp.pad(b_f[i], (0, LANES - ATT_HEADS)).reshape(1, LANES)

        zc, zg, q, k, v, f = _in_projection(x2, w_main, w_f, _tile(n_tok, 512))
        ya = _conv_module(zc, conv_w[i], conv_b[i], conv_ln_g[i], conv_ln_b[i], out_g[i, :CONV_CH],
                          bsz, seq, _tile(seq, 256))
        yb = _gmlp_module(zg, gmlp_ln_g[i], gmlp_ln_b[i], w_sp[i], b_sp[i],
                          out_g[i, CONV_CH:CONV_CH + GMLP_CH], _tile(n_tok, 512))
        c, ct = _forget_cumsum(f, b_f_row, bsz, seq, _tile(seq, 512))
        yc = _fox_attention(q, k, v, c, ct.reshape(bsz * ATT_HEADS, 1, seq), bsz, seq, _tile(seq, 512))
        x1, x1p = _out_projection(ya, yb, yc, x2, w_o[i].astype(bf16), out_g[i, CONV_CH + GMLP_CH:],
                                  ln1_g[i], ln1_b[i], alpha, _tile(n_tok, 256))

        eidx, gate_t, rank, cnt = _router(x1, w_r[i], r_bias[i], _tile(n_tok, 512))
        counts = cnt[:, 0]
        padded = (counts + bm - 1) // bm * bm
        pend = jnp.cumsum(padded).astype(jnp.int32)
        pstart = pend - padded
        blk_exp = jnp.minimum(
            jnp.searchsorted(pend, jnp.arange(n_blk, dtype=jnp.int32) * bm, side="right"), N_EXPERTS - 1
        ).astype(jnp.int32)
        n_valid = (pend[N_EXPERTS - 1:] // bm).astype(jnp.int32)

        dest = _dest_rows(pstart, eidx, rank, _tile(n_tok, 4096))
        xs = _dispatch(pend, dest, x1p, n_rows, _tile(n_tok, 256), bm)
        ys = _experts(blk_exp, n_valid, xs, w_e1[i].astype(bf16), w_e3[i].astype(bf16), w_e2[i].astype(bf16), bm)
        x2 = _combine(dest, x1, p[i].reshape(n_tok, -1), gate_t, w_sh1[i].astype(bf16), w_sh3[i].astype(bf16),
                      w_sh2[i].astype(bf16), w_pg[i].astype(bf16), b_pg[i], w_pe[i].astype(bf16),
                      ln2_g[i], ln2_b[i], ys, alpha, _tile(n_tok, 128))
    return x2.reshape(bsz, seq, d)
```

```python
import functools

import jax
import jax.numpy as jnp
from jax import lax
from jax.experimental import pallas as pl
from jax.experimental.pallas import tpu as pltpu

CONV_CH = 512
CONV_WIDTH = 31
GMLP_HEADS = 4
GMLP_HEAD_CH = 128
GMLP_CH = GMLP_HEADS * GMLP_HEAD_CH
CHUNK = 128
ATT_HEADS = 8
HEAD_DIM = 128
ATT_CH = ATT_HEADS * HEAD_DIM
N_EXPERTS = 64
TOP_K = 8
N_GROUPS = 8
GROUP_SIZE = N_EXPERTS // N_GROUPS
TOPK_GROUPS = 4
ROUTE_SCALE = 2.5
LN_EPS = 1e-5
LOG2E = 1.4426950408889634

LANES = 128
SUBLANES = 8
VMEM_BYTES_V7X = 64 * 1024 * 1024

CONV_HALO = 32
CONV_ROWS = 32
MOE_BLOCK = 512


def _vmem_limit(nbytes):
    return int(min(nbytes, VMEM_BYTES_V7X - 8 * 1024 * 1024))


def _params(semantics, vmem_bytes):
    return pltpu.CompilerParams(dimension_semantics=semantics, vmem_limit_bytes=_vmem_limit(vmem_bytes))


def _const_spec(shape):
    nd = len(shape)
    return pl.BlockSpec(shape, lambda *_: (0,) * nd, pipeline_mode=pl.Buffered(1))


def _layer_norm_rows(x, g, b):
    mu = jnp.mean(x, axis=-1, keepdims=True)
    xc = x - mu
    var = jnp.mean(xc * xc, axis=-1, keepdims=True)
    return xc * lax.rsqrt(var + LN_EPS) * g + b


def _rms_rows(x):
    return x * lax.rsqrt(jnp.mean(x * x, axis=-1, keepdims=True) + LN_EPS)


def _sigmoid(x):
    return 1.0 / (1.0 + jnp.exp(-x))


def _silu(x):
    return x * _sigmoid(x)


def _inproj_kernel(x_ref, w_ref, wf_ref, zc_ref, zg_ref, q_ref, k_ref, v_ref, f_ref, xb_ref, *, q_scale):
    j = pl.program_id(1)

    @pl.when(j == 0)
    def _():
        xb_ref[...] = x_ref[...].astype(jnp.bfloat16)
        f_ref[...] = jnp.dot(xb_ref[...], wf_ref[...], preferred_element_type=jnp.float32)

    z = jnp.dot(xb_ref[...], w_ref[...], preferred_element_type=jnp.float32)

    @pl.when(j == 0)
    def _():
        zc_ref[...] = z

    @pl.when(j == 1)
    def _():
        zg_ref[...] = z

    @pl.when(j == 2)
    def _():
        q_ref[...] = (z * q_scale).astype(jnp.bfloat16)

    @pl.when(j == 3)
    def _():
        k_ref[...] = z.astype(jnp.bfloat16)

    @pl.when(j == 4)
    def _():
        v_ref[...] = z.astype(jnp.bfloat16)


def _in_projection(x2, w_main, w_f, tm):
    n_tok, d = x2.shape
    wide = 2 * CONV_CH
    n_tiles = w_main.shape[1] // wide
    row = lambda i, j: (i, 0)
    out_shape = (
        jax.ShapeDtypeStruct((n_tok, wide), jnp.float32),
        jax.ShapeDtypeStruct((n_tok, wide), jnp.float32),
        jax.ShapeDtypeStruct((n_tok, ATT_CH), jnp.bfloat16),
        jax.ShapeDtypeStruct((n_tok, ATT_CH), jnp.bfloat16),
        jax.ShapeDtypeStruct((n_tok, ATT_CH), jnp.bfloat16),
        jax.ShapeDtypeStruct((n_tok, LANES), jnp.float32),
    )
    vmem = (2 * tm * d * 4 + tm * d * 2 + 2 * d * wide * 2 + d * LANES * 2
            + 2 * 2 * tm * wide * 4 + 3 * 2 * tm * wide * 2 + 2 * tm * LANES * 4 + 2 * tm * wide * 4)
    return pl.pallas_call(
        functools.partial(_inproj_kernel, q_scale=HEAD_DIM ** -0.5 * LOG2E),
        grid=(n_tok // tm, n_tiles),
        in_specs=[
            pl.BlockSpec((tm, d), row),
            pl.BlockSpec((d, wide), lambda i, j: (0, j)),
            _const_spec((d, LANES)),
        ],
        out_specs=[
            pl.BlockSpec((tm, wide), row),
            pl.BlockSpec((tm, wide), row),
            pl.BlockSpec((tm, ATT_CH), row),
            pl.BlockSpec((tm, ATT_CH), row),
            pl.BlockSpec((tm, ATT_CH), row),
            pl.BlockSpec((tm, LANES), row),
        ],
        out_shape=out_shape,
        scratch_shapes=[pltpu.VMEM((tm, d), jnp.bfloat16)],
        compiler_params=_params(("arbitrary", "arbitrary"), vmem + (8 << 20)),
        name="in_projection",
    )(x2, w_main, w_f)


def _conv_kernel(z_ref, w_ref, cb_ref, lng_ref, lnb_ref, og_ref, o_ref, hbuf_ref, *, ts):
    s = pl.program_id(1)

    @pl.when(s == 0)
    def _():
        hbuf_ref[0:CONV_HALO, :] = jnp.zeros((CONV_HALO, CONV_CH), jnp.float32)

    @pl.when(s > 0)
    def _():
        hbuf_ref[0:CONV_HALO, :] = hbuf_ref[ts:ts + CONV_HALO, :]

    hbuf_ref[CONV_HALO:CONV_HALO + ts, :] = z_ref[:, :CONV_CH] * _sigmoid(z_ref[:, CONV_CH:])

    cb = cb_ref[...]
    lng = lng_ref[...]
    lnb = lnb_ref[...]
    og = og_ref[...]
    first = CONV_HALO - (CONV_WIDTH - 1)
    for c in range(ts // CONV_ROWS):
        acc = jnp.broadcast_to(cb, (CONV_ROWS, CONV_CH))
        for j in range(CONV_WIDTH):
            off = first + j + c * CONV_ROWS
            acc = acc + w_ref[j:j + 1, :] * hbuf_ref[off:off + CONV_ROWS, :]
        y = _silu(_layer_norm_rows(acc, lng, lnb))
        o_ref[c * CONV_ROWS:(c + 1) * CONV_ROWS, :] = (_rms_rows(y) * og).astype(jnp.bfloat16)


def _conv_module(zc, conv_w, conv_b, ln_g, ln_b, og, bsz, seq, ts):
    n_tok = zc.shape[0]
    n_s = seq // ts
    vec = lambda v: v.reshape(1, CONV_CH)
    return pl.pallas_call(
        functools.partial(_conv_kernel, ts=ts),
        grid=(bsz, n_s),
        in_specs=[
            pl.BlockSpec((ts, 2 * CONV_CH), lambda b, s: (b * n_s + s, 0)),
            _const_spec((CONV_WIDTH, CONV_CH)),
            _const_spec((1, CONV_CH)),
            _const_spec((1, CONV_CH)),
            _const_spec((1, CONV_CH)),
            _const_spec((1, CONV_CH)),
        ],
        out_specs=pl.BlockSpec((ts, CONV_CH), lambda b, s: (b * n_s + s, 0)),
        out_shape=jax.ShapeDtypeStruct((n_tok, CONV_CH), jnp.bfloat16),
        scratch_shapes=[pltpu.VMEM((CONV_HALO + ts, CONV_CH), jnp.float32)],
        compiler_params=_params(("arbitrary", "arbitrary"), 32 << 20),
        name="conv_module",
    )(zc, conv_w, vec(conv_b), vec(ln_g), vec(ln_b), vec(og))


def _gelu_tanh(x):
    c = (2.0 / jnp.pi) ** 0.5
    return 0.5 * x * (1.0 + jnp.tanh(c * (x + 0.044715 * (x * x * x))))


def _gmlp_kernel(z_ref, lng_ref, lnb_ref, wsp_ref, bsp_ref, og_ref, o_ref, y_ref, *, tg):
    z = _gelu_tanh(z_ref[...])
    u = z[:, :GMLP_CH]
    v = _layer_norm_rows(z[:, GMLP_CH:], lng_ref[...], lnb_ref[...]).astype(jnp.bfloat16)
    t_idx = lax.broadcasted_iota(jnp.int32, (CHUNK, CHUNK), 0)
    s_idx = lax.broadcasted_iota(jnp.int32, (CHUNK, CHUNK), 1)
    causal = s_idx <= t_idx
    for h in range(GMLP_HEADS):
        ws = jnp.where(causal, wsp_ref[h], 0.0).astype(jnp.bfloat16)
        bias = bsp_ref[h]
        cols = slice(h * GMLP_HEAD_CH, (h + 1) * GMLP_HEAD_CH)
        for c in range(tg // CHUNK):
            rows = slice(c * CHUNK, (c + 1) * CHUNK)
            mixed = jnp.dot(ws, v[rows, cols], preferred_element_type=jnp.float32) + bias
            y_ref[rows, cols] = u[rows, cols] * mixed
    o_ref[...] = (_rms_rows(y_ref[...]) * og_ref[...]).astype(jnp.bfloat16)


def _gmlp_module(zg, ln_g, ln_b, w_sp, b_sp, og, tg):
    n_tok = zg.shape[0]
    vec = lambda v: v.reshape(1, GMLP_CH)
    return pl.pallas_call(
        functools.partial(_gmlp_kernel, tg=tg),
        grid=(n_tok // tg,),
        in_specs=[
            pl.BlockSpec((tg, 2 * GMLP_CH), lambda i: (i, 0)),
            _const_spec((1, GMLP_CH)),
            _const_spec((1, GMLP_CH)),
            _const_spec((GMLP_HEADS, CHUNK, CHUNK)),
            _const_spec((GMLP_HEADS, CHUNK, 1)),
            _const_spec((1, GMLP_CH)),
        ],
        out_specs=pl.BlockSpec((tg, GMLP_CH), lambda i: (i, 0)),
        out_shape=jax.ShapeDtypeStruct((n_tok, GMLP_CH), jnp.bfloat16),
        scratch_shapes=[pltpu.VMEM((tg, GMLP_CH), jnp.float32)],
        compiler_params=_params(("arbitrary",), 32 << 20),
        name="gmlp_module",
    )(zg, vec(ln_g), vec(ln_b), w_sp, b_sp.reshape(GMLP_HEADS, CHUNK, 1), vec(og))


def _split3(x):
    hi = x.astype(jnp.bfloat16)
    r1 = x - hi.astype(jnp.float32)
    mid = r1.astype(jnp.bfloat16)
    lo = (r1 - mid.astype(jnp.float32)).astype(jnp.bfloat16)
    return hi, mid, lo


def _fcum_kernel(f_ref, bf_ref, c_ref, carry_ref, *, ts):
    s = pl.program_id(1)

    @pl.when(s == 0)
    def _():
        carry_ref[...] = jnp.zeros_like(carry_ref)

    x = f_ref[...] + bf_ref[...]
    ls = -(jnp.maximum(-x, 0.0) + jnp.log(1.0 + jnp.exp(-jnp.abs(x))))
    t_idx = lax.broadcasted_iota(jnp.int32, (ts, ts), 0)
    s_idx = lax.broadcasted_iota(jnp.int32, (ts, ts), 1)
    tri = jnp.where(s_idx <= t_idx, 1.0, 0.0).astype(jnp.bfloat16)
    hi, mid, lo = _split3(ls)
    c = (jnp.dot(tri, lo, preferred_element_type=jnp.float32)
         + jnp.dot(tri, mid, preferred_element_type=jnp.float32)
         + jnp.dot(tri, hi, preferred_element_type=jnp.float32)) + carry_ref[...]
    c_ref[...] = c * LOG2E
    carry_ref[...] = c[ts - 1:ts, :]


def _forget_cumsum(f, b_f_row, bsz, seq, ts):
    n_tok = f.shape[0]
    n_s = seq // ts
    return pl.pallas_call(
        functools.partial(_fcum_kernel, ts=ts),
        grid=(bsz, n_s),
        in_specs=[
            pl.BlockSpec((ts, LANES), lambda b, s: (b * n_s + s, 0)),
            _const_spec((1, LANES)),
        ],
        out_specs=pl.BlockSpec((ts, LANES), lambda b, s: (b * n_s + s, 0)),
        out_shape=jax.ShapeDtypeStruct((n_tok, LANES), jnp.float32),
        scratch_shapes=[pltpu.VMEM((1, LANES), jnp.float32)],
        compiler_params=_params(("arbitrary", "arbitrary"), 32 << 20),
        name="forget_cumsum",
    )(f, b_f_row)


def _bias_columns(c_tile, h, as_query):
    rows = c_tile.shape[0]
    lane = lax.broadcasted_iota(jnp.int32, (rows, LANES), 1)
    col = jnp.sum(jnp.where(lane == h, c_tile, 0.0), axis=1, keepdims=True)
    val = jnp.broadcast_to(col if as_query else -col, (rows, LANES))
    hi, mid, lo = (t.astype(jnp.float32) for t in _split3(val))
    base = 0 if as_query else 3
    out = jnp.where(lane < 6, 1.0, 0.0)
    for offset, term in enumerate((hi, mid, lo)):
        out = jnp.where(lane == base + offset, term, out)
    return out.astype(jnp.bfloat16)


def _fox_kernel(q_ref, k_ref, v_ref, cq_ref, ck_ref, o_ref, kaug_ref, vt_ref, qaug_ref, m_ref, l_ref, acc_ref, s_ref,
                *, tq, tk, cw, seq):
    h = pl.program_id(1)
    i = pl.program_id(2)
    n_chain = tq // cw

    @pl.when(i == 0)
    def _():
        for r in range(seq // tk):
            rows = slice(r * tk, (r + 1) * tk)
            kaug_ref[rows, :HEAD_DIM] = k_ref[rows, :]
            kaug_ref[rows, HEAD_DIM:] = _bias_columns(ck_ref[rows, :], h, as_query=False)
            vt_ref[:, rows] = v_ref[rows, :].astype(jnp.float32).T.astype(jnp.bfloat16)

    qaug_ref[:, :HEAD_DIM] = q_ref[...]
    qaug_ref[:, HEAD_DIM:] = _bias_columns(cq_ref[...], h, as_query=True)
    m_ref[...] = jnp.full(m_ref.shape, -jnp.inf, jnp.float32)
    l_ref[...] = jnp.zeros_like(l_ref)
    acc_ref[...] = jnp.zeros_like(acc_ref)

    def scores(slot, c, k0):
        s_ref[slot] = lax.dot_general(kaug_ref[pl.ds(k0, tk), :], qaug_ref[c * cw:(c + 1) * cw, :],
                                      (((1,), (1,)), ((), ())), preferred_element_type=jnp.float32)

    def absorb(slot, c, k0, key_minus_query):
        s = s_ref[slot]
        if key_minus_query is not None:
            key = lax.broadcasted_iota(jnp.int32, (tk, cw), 0) + key_minus_query
            qry = lax.broadcasted_iota(jnp.int32, (tk, cw), 1)
            s = jnp.where(key <= qry, s, -jnp.inf)
        m_prev = m_ref[c]
        m_new = jnp.maximum(m_prev, jnp.max(s, axis=0, keepdims=True))
        alpha = jnp.exp2(m_prev - m_new)
        p = jnp.exp2(s - m_new)
        l_ref[c] = alpha * l_ref[c] + jnp.sum(p, axis=0, keepdims=True)
        pv = jnp.dot(vt_ref[:, pl.ds(k0, tk)], p.astype(jnp.bfloat16), preferred_element_type=jnp.float32)
        acc_ref[c] = alpha * acc_ref[c] + pv
        m_ref[c] = m_new

    n_kb = tq // tk

    def query_tile_keys(base, masks):
        work = [(kb, c) for kb in range(n_kb) for c in range(n_chain) if masks[kb][c] is not False]
        for slot, (kb, c) in enumerate(work):
            scores(slot, c, pl.multiple_of(base + kb * tk, tk))
        for slot, (kb, c) in enumerate(work):
            absorb(slot, c, pl.multiple_of(base + kb * tk, tk), masks[kb][c])

    def body(j, carry):
        query_tile_keys(j * tq, [[None] * n_chain] * n_kb)
        return carry

    lax.fori_loop(0, i, body, 0)
    masks = []
    for kb in range(n_kb):
        row = []
        for c in range(n_chain):
            first_key, first_query = kb * tk, c * cw
            if first_key + tk - 1 <= first_query:
                row.append(None)
            elif first_key <= first_query + cw - 1:
                row.append(first_key - first_query)
            else:
                row.append(False)
        masks.append(row)
    query_tile_keys(i * tq, masks)
    for c in range(n_chain):
        o_ref[c * cw:(c + 1) * cw, :] = (acc_ref[c] / l_ref[c]).T.astype(jnp.bfloat16)


def _fox_attention(q, k, v, c, bsz, seq, tq, tk, cw):
    n_tok = q.shape[0]
    nq = seq // tq
    n_chain = tq // cw
    return pl.pallas_call(
        functools.partial(_fox_kernel, tq=tq, tk=tk, cw=cw, seq=seq),
        grid=(bsz, ATT_HEADS, nq),
        in_specs=[
            pl.BlockSpec((tq, HEAD_DIM), lambda b, h, i: (b * nq + i, h)),
            pl.BlockSpec((seq, HEAD_DIM), lambda b, h, i: (b, h)),
            pl.BlockSpec((seq, HEAD_DIM), lambda b, h, i: (b, h)),
            pl.BlockSpec((tq, LANES), lambda b, h, i: (b * nq + i, 0)),
            pl.BlockSpec((seq, LANES), lambda b, h, i: (b, 0)),
        ],
        out_specs=pl.BlockSpec((tq, HEAD_DIM), lambda b, h, i: (b * nq + i, h)),
        out_shape=jax.ShapeDtypeStruct((n_tok, ATT_CH), jnp.bfloat16),
        scratch_shapes=[
            pltpu.VMEM((seq, 2 * HEAD_DIM), jnp.bfloat16),
            pltpu.VMEM((HEAD_DIM, seq), jnp.bfloat16),
            pltpu.VMEM((tq, 2 * HEAD_DIM), jnp.bfloat16),
            pltpu.VMEM((n_chain, 1, cw), jnp.float32),
            pltpu.VMEM((n_chain, 1, cw), jnp.float32),
            pltpu.VMEM((n_chain, HEAD_DIM, cw), jnp.float32),
            pltpu.VMEM((n_chain * (tq // tk), tk, cw), jnp.float32),
        ],
        compiler_params=_params(("arbitrary", "arbitrary", "arbitrary"), 40 << 20),
        name="fox_attention",
    )(q, k, v, c, c)


def _pack_bf16_pairs(x):
    n = x.shape[1] // 2
    r = x.astype(jnp.bfloat16).astype(jnp.float32)
    lo = lax.bitcast_convert_type(r[:, :n], jnp.uint32)
    hi = lax.bitcast_convert_type(r[:, n:], jnp.uint32)
    return (lo >> 16) | (hi & jnp.uint32(0xFFFF0000))


def _unpack_bf16_pairs(w):
    lo = lax.bitcast_convert_type(w << 16, jnp.float32).astype(jnp.bfloat16)
    hi = lax.bitcast_convert_type(w & jnp.uint32(0xFFFF0000), jnp.float32).astype(jnp.bfloat16)
    return lo, hi


def _outproj_kernel(ya_ref, yb_ref, yc_ref, x_ref, wo_ref, ogc_ref, g_ref, b_ref, x1_ref, x1p_ref, y_ref, *, alpha):
    y_ref[:, 0:CONV_CH] = ya_ref[...]
    y_ref[:, CONV_CH:CONV_CH + GMLP_CH] = yb_ref[...]
    yc = yc_ref[...].astype(jnp.float32)
    y_ref[:, CONV_CH + GMLP_CH:] = (_rms_rows(yc) * ogc_ref[...]).astype(jnp.bfloat16)
    h = jnp.dot(y_ref[...], wo_ref[...], preferred_element_type=jnp.float32)
    x1 = _layer_norm_rows(alpha * x_ref[...] + h, g_ref[...], b_ref[...])
    x1_ref[...] = x1
    x1p_ref[...] = _pack_bf16_pairs(x1)


def _out_projection(ya, yb, yc, x2, w_o, og_c, ln_g, ln_b, alpha, tm):
    n_tok, d = x2.shape
    mix = w_o.shape[0]
    row = lambda i: (i, 0)
    vmem = (mix * d * 2 + 2 * tm * d * 4 * 2 + 2 * tm * (d // 2) * 4 + 2 * tm * mix * 2 + tm * mix * 2 + 3 * tm * d * 4)
    return pl.pallas_call(
        functools.partial(_outproj_kernel, alpha=alpha),
        grid=(n_tok // tm,),
        in_specs=[
            pl.BlockSpec((tm, CONV_CH), row),
            pl.BlockSpec((tm, GMLP_CH), row),
            pl.BlockSpec((tm, ATT_CH), row),
            pl.BlockSpec((tm, d), row),
            _const_spec((mix, d)),
            _const_spec((1, ATT_CH)),
            _const_spec((1, d)),
            _const_spec((1, d)),
        ],
        out_specs=[pl.BlockSpec((tm, d), row), pl.BlockSpec((tm, d // 2), row)],
        out_shape=(
            jax.ShapeDtypeStruct((n_tok, d), jnp.float32),
            jax.ShapeDtypeStruct((n_tok, d // 2), jnp.uint32),
        ),
        scratch_shapes=[pltpu.VMEM((tm, mix), jnp.bfloat16)],
        compiler_params=_params(("arbitrary",), vmem + (8 << 20)),
        name="out_projection",
    )(ya, yb, yc, x2, w_o, og_c.reshape(1, ATT_CH), ln_g.reshape(1, d), ln_b.reshape(1, d))


def _first_argmax_rows(x, idx):
    m = jnp.max(x, axis=0, keepdims=True)
    first = jnp.min(jnp.where(x == m, idx, x.shape[0]), axis=0, keepdims=True)
    return m, first


def _router_kernel(x_ref, wh_ref, wl_ref, rb_ref, eidx_ref, gate_t_ref, rank_ref, cnt_ref, carry_ref, *, tr):
    i = pl.program_id(0)

    @pl.when(i == 0)
    def _():
        carry_ref[...] = jnp.zeros_like(carry_ref)

    x = x_ref[...]
    xh = x.astype(jnp.bfloat16)
    xl = (x - xh.astype(jnp.float32)).astype(jnp.bfloat16)
    nt = (((1,), (1,)), ((), ()))
    wh = wh_ref[...]
    logits = (lax.dot_general(wl_ref[...], xh, nt, preferred_element_type=jnp.float32)
              + lax.dot_general(wh, xl, nt, preferred_element_type=jnp.float32)
              + lax.dot_general(wh, xh, nt, preferred_element_type=jnp.float32))
    scores = _sigmoid(logits)
    sel = scores + rb_ref[...]

    neg = -jnp.inf
    e_idx = lax.broadcasted_iota(jnp.int32, (N_EXPERTS, tr), 0)
    g_idx = lax.broadcasted_iota(jnp.int32, (N_GROUPS, tr), 0)
    in_idx = lax.broadcasted_iota(jnp.int32, (GROUP_SIZE, tr), 0)

    gs_rows = []
    for g in range(N_GROUPS):
        blk = sel[g * GROUP_SIZE:(g + 1) * GROUP_SIZE, :]
        m1, a1 = _first_argmax_rows(blk, in_idx)
        m2 = jnp.max(jnp.where(in_idx == a1, neg, blk), axis=0, keepdims=True)
        gs_rows.append(m1 + m2)
    gs = jnp.concatenate(gs_rows, axis=0)

    gsel = jnp.zeros((N_GROUPS, tr), jnp.float32)
    for _ in range(TOPK_GROUPS):
        _, a = _first_argmax_rows(gs, g_idx)
        pick = g_idx == a
        gsel = jnp.where(pick, 1.0, gsel)
        gs = jnp.where(pick, neg, gs)
    esel = jnp.concatenate(
        [jnp.broadcast_to(gsel[g:g + 1, :], (GROUP_SIZE, tr)) for g in range(N_GROUPS)], axis=0)
    cand = jnp.where(esel > 0.5, sel, neg)

    picks, gates = [], []
    chosen = jnp.zeros((N_EXPERTS, tr), jnp.float32)
    for _ in range(TOP_K):
        _, a = _first_argmax_rows(cand, e_idx)
        pick = e_idx == a
        picks.append(a)
        gates.append(jnp.sum(jnp.where(pick, scores, 0.0), axis=0, keepdims=True))
        chosen = jnp.where(pick, 1.0, chosen)
        cand = jnp.where(pick, neg, cand)
    gate = jnp.concatenate(gates, axis=0)
    gate = gate / jnp.sum(gate, axis=0, keepdims=True) * ROUTE_SCALE

    r_idx = lax.broadcasted_iota(jnp.int32, (tr, tr), 0)
    c_idx = lax.broadcasted_iota(jnp.int32, (tr, tr), 1)
    upper = jnp.where(r_idx <= c_idx, 1.0, 0.0).astype(jnp.bfloat16)
    incl = jnp.dot(chosen.astype(jnp.bfloat16), upper, preferred_element_type=jnp.float32)
    rank_all = carry_ref[...] + incl - chosen
    carry_ref[...] = carry_ref[...] + incl[:, tr - 1:tr]
    ranks = [jnp.sum(jnp.where(e_idx == a, rank_all, 0.0), axis=0, keepdims=True) for a in picks]

    eidx_ref[...] = jnp.concatenate(picks, axis=0)
    rank_ref[...] = jnp.concatenate(ranks, axis=0).astype(jnp.int32)
    pad = jnp.zeros((LANES - TOP_K, tr), jnp.float32)
    gate_t_ref[...] = jnp.concatenate([gate, pad], axis=0).T
    cnt_ref[...] = jnp.broadcast_to(carry_ref[...], (N_EXPERTS, LANES)).astype(jnp.int32)


def _router(x1, w_r, r_bias, tr):
    n_tok, d = x1.shape
    wt = w_r.T
    wh = wt.astype(jnp.bfloat16)
    wl = (wt - wh.astype(jnp.float32)).astype(jnp.bfloat16)
    col = lambda i: (0, i)
    return pl.pallas_call(
        functools.partial(_router_kernel, tr=tr),
        grid=(n_tok // tr,),
        in_specs=[
            pl.BlockSpec((tr, d), lambda i: (i, 0)),
            _const_spec((N_EXPERTS, d)),
            _const_spec((N_EXPERTS, d)),
            _const_spec((N_EXPERTS, 1)),
        ],
        out_specs=[
            pl.BlockSpec((TOP_K, tr), col),
            pl.BlockSpec((tr, LANES), lambda i: (i, 0)),
            pl.BlockSpec((TOP_K, tr), col),
            pl.BlockSpec((N_EXPERTS, LANES), lambda i: (0, 0)),
        ],
        out_shape=(
            jax.ShapeDtypeStruct((TOP_K, n_tok), jnp.int32),
            jax.ShapeDtypeStruct((n_tok, LANES), jnp.float32),
            jax.ShapeDtypeStruct((TOP_K, n_tok), jnp.int32),
            jax.ShapeDtypeStruct((N_EXPERTS, LANES), jnp.int32),
        ),
        scratch_shapes=[pltpu.VMEM((N_EXPERTS, 1), jnp.float32)],
        compiler_params=_params(("arbitrary",), 32 << 20),
        name="router",
    )(x1, wh, wl, r_bias.reshape(N_EXPERTS, 1))


def _dest_kernel(pstart_ref, eidx_ref, rank_ref, dest_ref):
    e = eidx_ref[...]
    base = jnp.zeros(e.shape, jnp.int32)
    for k in range(N_EXPERTS):
        base = jnp.where(e == k, pstart_ref[k], base)
    dest_ref[...] = base + rank_ref[...]


def _dest_rows(pstart, eidx, rank, tc):
    n_tok = eidx.shape[1]
    col = lambda i, ps: (0, i)
    return pl.pallas_call(
        _dest_kernel,
        grid_spec=pltpu.PrefetchScalarGridSpec(
            num_scalar_prefetch=1,
            grid=(n_tok // tc,),
            in_specs=[pl.BlockSpec((TOP_K, tc), col), pl.BlockSpec((TOP_K, tc), col)],
            out_specs=pl.BlockSpec((TOP_K, tc), col),
        ),
        out_shape=jax.ShapeDtypeStruct((TOP_K, n_tok), jnp.int32),
        compiler_params=_params(("arbitrary",), 32 << 20),
        name="dest_rows",
    )(pstart, eidx, rank)


def _dispatch_kernel(pend_ref, dest_ref, x_ref, xs_ref, zero_ref, sem, *, td, bm):
    i = pl.program_id(0)

    @pl.when(i == 0)
    def _():
        zero_ref[...] = jnp.zeros_like(zero_ref)
        for e in range(N_EXPERTS):
            prev = pend_ref[e - 1] if e > 0 else 0

            @pl.when(pend_ref[e] > prev)
            def _():
                start = pl.multiple_of(pend_ref[e] - bm, bm)
                pltpu.make_async_copy(zero_ref, xs_ref.at[pl.ds(start, bm), :], sem).start()
        first_unused = pend_ref[N_EXPERTS - 1] // bm
        n_blk = xs_ref.shape[0] // bm

        def fill(b, carry):
            pltpu.make_async_copy(zero_ref, xs_ref.at[pl.ds(pl.multiple_of(b * bm, bm), bm), :], sem).start()
            return carry

        def fill_done(b, carry):
            pltpu.make_async_copy(zero_ref, xs_ref.at[pl.ds(0, bm), :], sem).wait()
            return carry

        lax.fori_loop(first_unused, n_blk, fill, 0)
        lax.fori_loop(first_unused, n_blk, fill_done, 0)
        for e in range(N_EXPERTS):
            prev = pend_ref[e - 1] if e > 0 else 0

            @pl.when(pend_ref[e] > prev)
            def _():
                pltpu.make_async_copy(zero_ref, xs_ref.at[pl.ds(0, bm), :], sem).wait()

    def issue(r, carry):
        for j in range(TOP_K):
            d = dest_ref[j, r]
            pltpu.make_async_copy(x_ref.at[pl.ds(r, 1), :], xs_ref.at[pl.ds(d, 1), :], sem).start()
        return carry

    lax.fori_loop(0, td, issue, 0)
    for j in range(TOP_K):
        pltpu.make_async_copy(x_ref, xs_ref.at[pl.ds(0, td), :], sem).wait()


def _dispatch(pend, dest, x1p, n_rows, td, bm):
    n_tok, width = x1p.shape
    return pl.pallas_call(
        functools.partial(_dispatch_kernel, td=td, bm=bm),
        grid_spec=pltpu.PrefetchScalarGridSpec(
            num_scalar_prefetch=1,
            grid=(n_tok // td,),
            in_specs=[
                pl.BlockSpec((TOP_K, td), lambda i, pe: (0, i), memory_space=pltpu.SMEM),
                pl.BlockSpec((td, width), lambda i, pe: (i, 0)),
            ],
            out_specs=pl.BlockSpec(memory_space=pl.ANY),
            scratch_shapes=[pltpu.VMEM((bm, width), jnp.uint32), pltpu.SemaphoreType.DMA],
        ),
        out_shape=jax.ShapeDtypeStruct((n_rows, width), jnp.uint32),
        compiler_params=_params(("arbitrary",), 32 << 20),
        name="dispatch",
    )(pend, dest, x1p)


def _cast_kernel(w_ref, o_ref):
    o_ref[...] = w_ref[...].astype(o_ref.dtype)


def _to_bf16(w, layer):
    _, n_e, a, b = w.shape
    return pl.pallas_call(
        _cast_kernel,
        grid=(n_e,),
        in_specs=[pl.BlockSpec((None, 1, a, b), lambda e: (layer, e, 0, 0))],
        out_specs=pl.BlockSpec((1, a, b), lambda e: (e, 0, 0)),
        out_shape=jax.ShapeDtypeStruct((n_e, a, b), jnp.bfloat16),
        compiler_params=_params(("arbitrary",), 32 << 20),
        name="expert_weights_bf16",
    )(w)


def _experts_kernel(be_ref, nv_ref, xs_ref, w1_ref, w3_ref, w2_ref, ys_ref, xb_ref):
    b = pl.program_id(0)

    @pl.when(b < nv_ref[0])
    def _():
        half = xs_ref.shape[1]
        lo, hi = _unpack_bf16_pairs(xs_ref[...])
        xb_ref[:, :half] = lo
        xb_ref[:, half:] = hi
        xb = xb_ref[...]
        h1 = jnp.dot(xb, w1_ref[0], preferred_element_type=jnp.float32)
        h3 = jnp.dot(xb, w3_ref[0], preferred_element_type=jnp.float32)
        h = (_silu(h1) * h3).astype(jnp.bfloat16)
        ys_ref[...] = _pack_bf16_pairs(jnp.dot(h, w2_ref[0], preferred_element_type=jnp.float32))

    @pl.when(b >= nv_ref[0])
    def _():
        ys_ref[...] = jnp.zeros_like(ys_ref)


def _experts(blk_exp, n_valid, xs, w1, w3, w2, bm):
    n_rows, half = xs.shape
    d = 2 * half
    de = w1.shape[2]
    n_blk = n_rows // bm
    blk = lambda b, be, nv: (jnp.minimum(b, nv[0] - 1), 0)
    wsel = lambda b, be, nv: (be[jnp.minimum(b, nv[0] - 1)], 0, 0)
    vmem = 2 * bm * half * 4 + bm * d * 2 + 2 * 3 * d * de * 2 + 2 * bm * half * 4 + 4 * bm * de * 4 + 2 * bm * d * 4
    return pl.pallas_call(
        _experts_kernel,
        grid_spec=pltpu.PrefetchScalarGridSpec(
            num_scalar_prefetch=2,
            grid=(n_blk,),
            in_specs=[
                pl.BlockSpec((bm, half), blk),
                pl.BlockSpec((1, d, de), wsel),
                pl.BlockSpec((1, d, de), wsel),
                pl.BlockSpec((1, de, d), wsel),
            ],
            out_specs=pl.BlockSpec((bm, half), lambda b, be, nv: (b, 0)),
            scratch_shapes=[pltpu.VMEM((bm, d), jnp.bfloat16)],
        ),
        out_shape=jax.ShapeDtypeStruct((n_rows, half), jnp.uint32),
        compiler_params=_params(("arbitrary",), vmem + (8 << 20)),
        name="experts",
    )(blk_exp, n_valid, xs, w1, w3, w2)


def _combine_kernel(dest_ref, x1_ref, p_ref, gt_ref, w1_ref, w3_ref, w2_ref, wpg_ref, bpg_ref, wpe_ref, g_ref, b_ref,
                    ys_ref, o_ref, ybuf_ref, sem, *, tm, alpha):
    def issue(r, carry):
        for j in range(TOP_K):
            d = dest_ref[j, r]
            pltpu.make_async_copy(ys_ref.at[pl.ds(d, 1), :], ybuf_ref.at[j, pl.ds(r, 1), :], sem).start()
        return carry

    lax.fori_loop(0, tm, issue, 0)

    x1 = x1_ref[...]
    xb = x1.astype(jnp.bfloat16)
    h1 = jnp.dot(xb, w1_ref[...], preferred_element_type=jnp.float32)
    h3 = jnp.dot(xb, w3_ref[...], preferred_element_type=jnp.float32)
    h = (_silu(h1) * h3).astype(jnp.bfloat16)
    total = alpha * x1 + jnp.dot(h, w2_ref[...], preferred_element_type=jnp.float32)
    gate = _sigmoid(jnp.dot(xb, wpg_ref[...], preferred_element_type=jnp.float32) + bpg_ref[...])
    pe = jnp.dot(p_ref[...].astype(jnp.bfloat16), wpe_ref[...], preferred_element_type=jnp.float32)
    total = total + gate * pe

    for j in range(TOP_K):
        pltpu.make_async_copy(ys_ref.at[pl.ds(0, tm), :], ybuf_ref.at[j], sem).wait()
    gt = gt_ref[...]
    half = ybuf_ref.shape[2]
    moe_lo = jnp.zeros((tm, half), jnp.float32)
    moe_hi = jnp.zeros((tm, half), jnp.float32)
    for j in range(TOP_K):
        w = ybuf_ref[j]
        g = gt[:, j:j + 1]
        moe_lo = moe_lo + g * lax.bitcast_convert_type(w << 16, jnp.float32)
        moe_hi = moe_hi + g * lax.bitcast_convert_type(w & jnp.uint32(0xFFFF0000), jnp.float32)
    total = total + jnp.concatenate([moe_lo, moe_hi], axis=1)
    o_ref[...] = _layer_norm_rows(total, g_ref[...], b_ref[...])


def _combine(dest, x1, p2, gate_t, w_sh1, w_sh3, w_sh2, w_pg, b_pg, w_pe, ln_g, ln_b, ys, alpha, tm):
    n_tok, d = x1.shape
    ds = w_sh1.shape[1]
    dp = p2.shape[1]
    row = lambda i: (i, 0)
    vmem = (TOP_K * tm * (d // 2) * 4 + (3 * d * ds + d * d + dp * d) * 2 + 2 * 2 * tm * d * 4 + 2 * tm * dp * 4
            + 2 * tm * LANES * 4 + 6 * tm * d * 4)
    return pl.pallas_call(
        functools.partial(_combine_kernel, tm=tm, alpha=alpha),
        grid=(n_tok // tm,),
        in_specs=[
            pl.BlockSpec((TOP_K, tm), lambda i: (0, i), memory_space=pltpu.SMEM),
            pl.BlockSpec((tm, d), row),
            pl.BlockSpec((tm, dp), row),
            pl.BlockSpec((tm, LANES), row),
            _const_spec((d, ds)),
            _const_spec((d, ds)),
            _const_spec((ds, d)),
            _const_spec((d, d)),
            _const_spec((1, d)),
            _const_spec((dp, d)),
            _const_spec((1, d)),
            _const_spec((1, d)),
            pl.BlockSpec(memory_space=pl.ANY),
        ],
        out_specs=pl.BlockSpec((tm, d), row),
        out_shape=jax.ShapeDtypeStruct((n_tok, d), jnp.float32),
        scratch_shapes=[pltpu.VMEM((TOP_K, tm, d // 2), jnp.uint32), pltpu.SemaphoreType.DMA],
        compiler_params=_params(("arbitrary",), vmem + (8 << 20)),
        name="combine",
    )(dest, x1, p2, gate_t, w_sh1, w_sh3, w_sh2, w_pg, b_pg.reshape(1, d), w_pe, ln_g.reshape(1, d),
      ln_b.reshape(1, d), ys)


def _tile(n, target):
    t = min(n, target)
    assert n % t == 0, (n, t)
    return t


def kernel(x, p, w_in, b_f, conv_w, conv_b, conv_ln_g, conv_ln_b, gmlp_ln_g, gmlp_ln_b, w_sp, b_sp, out_g, w_o,
           ln1_g, ln1_b, w_r, r_bias, w_e1, w_e3, w_e2, w_sh1, w_sh3, w_sh2, w_pe, w_pg, b_pg, ln2_g, ln2_b):
    bsz, seq, d = x.shape
    depth = w_in.shape[0]
    n_tok = bsz * seq
    alpha = (2.0 * depth) ** 0.25
    n_main = 2 * CONV_CH + 2 * GMLP_CH + 3 * ATT_CH
    assert w_in.shape[2] == n_main + ATT_HEADS and seq % CHUNK == 0

    bm = MOE_BLOCK
    n_blk = -(-n_tok * TOP_K // bm) + N_EXPERTS
    n_rows = n_blk * bm
    bf16 = jnp.bfloat16

    x2 = x.reshape(n_tok, d)
    for i in range(depth):
        w_main = w_in[i, :, :n_main].astype(bf16)
        w_f = jnp.pad(w_in[i, :, n_main:], ((0, 0), (0, LANES - ATT_HEADS))).astype(bf16)
        b_f_row = jnp.pad(b_f[i], (0, LANES - ATT_HEADS)).reshape(1, LANES)

        zc, zg, q, k, v, f = _in_projection(x2, w_main, w_f, _tile(n_tok, 512))
        ya = _conv_module(zc, conv_w[i], conv_b[i], conv_ln_g[i], conv_ln_b[i], out_g[i, :CONV_CH],
                          bsz, seq, _tile(seq, 256))
        yb = _gmlp_module(zg, gmlp_ln_g[i], gmlp_ln_b[i], w_sp[i], b_sp[i],
                          out_g[i, CONV_CH:CONV_CH + GMLP_CH], _tile(n_tok, 512))
        c = _forget_cumsum(f, b_f_row, bsz, seq, _tile(seq, 512))
        tq = _tile(seq, 1024)
        yc = _fox_attention(q, k, v, c, bsz, seq, tq, _tile(tq, 512), _tile(tq, 512))
        x1, x1p = _out_projection(ya, yb, yc, x2, w_o[i].astype(bf16), out_g[i, CONV_CH + GMLP_CH:],
                                  ln1_g[i], ln1_b[i], alpha, _tile(n_tok, 256))

        eidx, gate_t, rank, cnt = _router(x1, w_r[i], r_bias[i], _tile(n_tok, 512))
        counts = cnt[:, 0]
        padded = (counts + bm - 1) // bm * bm
        pend = jnp.cumsum(padded).astype(jnp.int32)
        pstart = pend - padded
        blk_start = jnp.arange(n_blk, dtype=jnp.int32) * bm
        blk_exp = jnp.minimum(jnp.sum(pend[None, :] <= blk_start[:, None], axis=1), N_EXPERTS - 1).astype(jnp.int32)
        n_valid = (pend[N_EXPERTS - 1:] // bm).astype(jnp.int32)

        dest = _dest_rows(pstart, eidx, rank, _tile(n_tok, 4096))
        xs = _dispatch(pend, dest, x1p, n_rows, _tile(n_tok, 256), bm)
        ys = _experts(blk_exp, n_valid, xs, _to_bf16(w_e1, i), _to_bf16(w_e3, i), _to_bf16(w_e2, i), bm)
        x2 = _combine(dest, x1, p[i].reshape(n_tok, -1), gate_t, w_sh1[i].astype(bf16), w_sh3[i].astype(bf16),
                      w_sh2[i].astype(bf16), w_pg[i].astype(bf16), b_pg[i], w_pe[i].astype(bf16),
                      ln2_g[i], ln2_b[i], ys, alpha, _tile(n_tok, 256))
    return x2.reshape(bsz, seq, d)
```

```python
import functools

import jax
import jax.numpy as jnp
from jax import lax
from jax.experimental import pallas as pl
from jax.experimental.pallas import tpu as pltpu

CONV_CH = 512
CONV_WIDTH = 31
GMLP_HEADS = 4
GMLP_HEAD_CH = 128
GMLP_CH = GMLP_HEADS * GMLP_HEAD_CH
CHUNK = 128
ATT_HEADS = 8
HEAD_DIM = 128
ATT_CH = ATT_HEADS * HEAD_DIM
N_EXPERTS = 64
TOP_K = 8
N_GROUPS = 8
GROUP_SIZE = N_EXPERTS // N_GROUPS
TOPK_GROUPS = 4
ROUTE_SCALE = 2.5
LN_EPS = 1e-5
LOG2E = 1.4426950408889634

LANES = 128
SUBLANES = 8
VMEM_BYTES_V7X = 64 * 1024 * 1024

CONV_HALO = 32
CONV_ROWS = 32
MOE_BLOCK = 512


def _vmem_limit(nbytes):
    return int(min(nbytes, VMEM_BYTES_V7X - 8 * 1024 * 1024))


def _params(semantics, vmem_bytes):
    return pltpu.CompilerParams(dimension_semantics=semantics, vmem_limit_bytes=_vmem_limit(vmem_bytes))


def _const_spec(shape):
    nd = len(shape)
    return pl.BlockSpec(shape, lambda *_: (0,) * nd, pipeline_mode=pl.Buffered(1))


def _layer_norm_rows(x, g, b):
    mu = jnp.mean(x, axis=-1, keepdims=True)
    xc = x - mu
    var = jnp.mean(xc * xc, axis=-1, keepdims=True)
    return xc * lax.rsqrt(var + LN_EPS) * g + b


def _rms_rows(x):
    return x * lax.rsqrt(jnp.mean(x * x, axis=-1, keepdims=True) + LN_EPS)


def _sigmoid(x):
    return 1.0 / (1.0 + jnp.exp(-x))


def _silu(x):
    return x * _sigmoid(x)


def _inproj_kernel(x_ref, w_ref, wf_ref, zc_ref, zg_ref, q_ref, k_ref, v_ref, f_ref, xb_ref, *, q_scale):
    j = pl.program_id(1)

    @pl.when(j == 0)
    def _():
        xb_ref[...] = x_ref[...].astype(jnp.bfloat16)
        f_ref[...] = jnp.dot(xb_ref[...], wf_ref[...], preferred_element_type=jnp.float32)

    z = jnp.dot(xb_ref[...], w_ref[...], preferred_element_type=jnp.float32)

    @pl.when(j == 0)
    def _():
        zc_ref[...] = z

    @pl.when(j == 1)
    def _():
        zg_ref[...] = z

    @pl.when(j == 2)
    def _():
        q_ref[...] = (z * q_scale).astype(jnp.bfloat16)

    @pl.when(j == 3)
    def _():
        k_ref[...] = z.astype(jnp.bfloat16)

    @pl.when(j == 4)
    def _():
        v_ref[...] = z.astype(jnp.bfloat16)


def _in_projection(x2, w_main, w_f, tm):
    n_tok, d = x2.shape
    wide = 2 * CONV_CH
    n_tiles = w_main.shape[1] // wide
    row = lambda i, j: (i, 0)
    out_shape = (
        jax.ShapeDtypeStruct((n_tok, wide), jnp.float32),
        jax.ShapeDtypeStruct((n_tok, wide), jnp.float32),
        jax.ShapeDtypeStruct((n_tok, ATT_CH), jnp.bfloat16),
        jax.ShapeDtypeStruct((n_tok, ATT_CH), jnp.bfloat16),
        jax.ShapeDtypeStruct((n_tok, ATT_CH), jnp.bfloat16),
        jax.ShapeDtypeStruct((n_tok, LANES), jnp.float32),
    )
    vmem = (2 * tm * d * 4 + tm * d * 2 + 2 * d * wide * 2 + d * LANES * 2
            + 2 * 2 * tm * wide * 4 + 3 * 2 * tm * wide * 2 + 2 * tm * LANES * 4 + 2 * tm * wide * 4)
    return pl.pallas_call(
        functools.partial(_inproj_kernel, q_scale=HEAD_DIM ** -0.5 * LOG2E),
        grid=(n_tok // tm, n_tiles),
        in_specs=[
            pl.BlockSpec((tm, d), row),
            pl.BlockSpec((d, wide), lambda i, j: (0, j)),
            _const_spec((d, LANES)),
        ],
        out_specs=[
            pl.BlockSpec((tm, wide), row),
            pl.BlockSpec((tm, wide), row),
            pl.BlockSpec((tm, ATT_CH), row),
            pl.BlockSpec((tm, ATT_CH), row),
            pl.BlockSpec((tm, ATT_CH), row),
            pl.BlockSpec((tm, LANES), row),
        ],
        out_shape=out_shape,
        scratch_shapes=[pltpu.VMEM((tm, d), jnp.bfloat16)],
        compiler_params=_params(("arbitrary", "arbitrary"), vmem + (8 << 20)),
        name="in_projection",
    )(x2, w_main, w_f)


def _conv_kernel(z_ref, w_ref, cb_ref, lng_ref, lnb_ref, og_ref, o_ref, hbuf_ref, *, ts):
    s = pl.program_id(1)

    @pl.when(s == 0)
    def _():
        hbuf_ref[0:CONV_HALO, :] = jnp.zeros((CONV_HALO, CONV_CH), jnp.float32)

    @pl.when(s > 0)
    def _():
        hbuf_ref[0:CONV_HALO, :] = hbuf_ref[ts:ts + CONV_HALO, :]

    hbuf_ref[CONV_HALO:CONV_HALO + ts, :] = z_ref[:, :CONV_CH] * _sigmoid(z_ref[:, CONV_CH:])

    cb = cb_ref[...]
    lng = lng_ref[...]
    lnb = lnb_ref[...]
    og = og_ref[...]
    first = CONV_HALO - (CONV_WIDTH - 1)
    for c in range(ts // CONV_ROWS):
        acc = jnp.broadcast_to(cb, (CONV_ROWS, CONV_CH))
        for j in range(CONV_WIDTH):
            off = first + j + c * CONV_ROWS
            acc = acc + w_ref[j:j + 1, :] * hbuf_ref[off:off + CONV_ROWS, :]
        y = _silu(_layer_norm_rows(acc, lng, lnb))
        o_ref[c * CONV_ROWS:(c + 1) * CONV_ROWS, :] = (_rms_rows(y) * og).astype(jnp.bfloat16)


def _conv_module(zc, conv_w, conv_b, ln_g, ln_b, og, bsz, seq, ts):
    n_tok = zc.shape[0]
    n_s = seq // ts
    vec = lambda v: v.reshape(1, CONV_CH)
    return pl.pallas_call(
        functools.partial(_conv_kernel, ts=ts),
        grid=(bsz, n_s),
        in_specs=[
            pl.BlockSpec((ts, 2 * CONV_CH), lambda b, s: (b * n_s + s, 0)),
            _const_spec((CONV_WIDTH, CONV_CH)),
            _const_spec((1, CONV_CH)),
            _const_spec((1, CONV_CH)),
            _const_spec((1, CONV_CH)),
            _const_spec((1, CONV_CH)),
        ],
        out_specs=pl.BlockSpec((ts, CONV_CH), lambda b, s: (b * n_s + s, 0)),
        out_shape=jax.ShapeDtypeStruct((n_tok, CONV_CH), jnp.bfloat16),
        scratch_shapes=[pltpu.VMEM((CONV_HALO + ts, CONV_CH), jnp.float32)],
        compiler_params=_params(("arbitrary", "arbitrary"), 32 << 20),
        name="conv_module",
    )(zc, conv_w, vec(conv_b), vec(ln_g), vec(ln_b), vec(og))


def _gelu_tanh(x):
    c = (2.0 / jnp.pi) ** 0.5
    return 0.5 * x * (1.0 + jnp.tanh(c * (x + 0.044715 * (x * x * x))))


def _gmlp_kernel(z_ref, lng_ref, lnb_ref, wsp_ref, bsp_ref, og_ref, o_ref, y_ref, *, tg):
    z = _gelu_tanh(z_ref[...])
    u = z[:, :GMLP_CH]
    v = _layer_norm_rows(z[:, GMLP_CH:], lng_ref[...], lnb_ref[...]).astype(jnp.bfloat16)
    t_idx = lax.broadcasted_iota(jnp.int32, (CHUNK, CHUNK), 0)
    s_idx = lax.broadcasted_iota(jnp.int32, (CHUNK, CHUNK), 1)
    causal = s_idx <= t_idx
    for h in range(GMLP_HEADS):
        ws = jnp.where(causal, wsp_ref[h], 0.0).astype(jnp.bfloat16)
        bias = bsp_ref[h]
        cols = slice(h * GMLP_HEAD_CH, (h + 1) * GMLP_HEAD_CH)
        for c in range(tg // CHUNK):
            rows = slice(c * CHUNK, (c + 1) * CHUNK)
            mixed = jnp.dot(ws, v[rows, cols], preferred_element_type=jnp.float32) + bias
            y_ref[rows, cols] = u[rows, cols] * mixed
    o_ref[...] = (_rms_rows(y_ref[...]) * og_ref[...]).astype(jnp.bfloat16)


def _gmlp_module(zg, ln_g, ln_b, w_sp, b_sp, og, tg):
    n_tok = zg.shape[0]
    vec = lambda v: v.reshape(1, GMLP_CH)
    return pl.pallas_call(
        functools.partial(_gmlp_kernel, tg=tg),
        grid=(n_tok // tg,),
        in_specs=[
            pl.BlockSpec((tg, 2 * GMLP_CH), lambda i: (i, 0)),
            _const_spec((1, GMLP_CH)),
            _const_spec((1, GMLP_CH)),
            _const_spec((GMLP_HEADS, CHUNK, CHUNK)),
            _const_spec((GMLP_HEADS, CHUNK, 1)),
            _const_spec((1, GMLP_CH)),
        ],
        out_specs=pl.BlockSpec((tg, GMLP_CH), lambda i: (i, 0)),
        out_shape=jax.ShapeDtypeStruct((n_tok, GMLP_CH), jnp.bfloat16),
        scratch_shapes=[pltpu.VMEM((tg, GMLP_CH), jnp.float32)],
        compiler_params=_params(("arbitrary",), 32 << 20),
        name="gmlp_module",
    )(zg, vec(ln_g), vec(ln_b), w_sp, b_sp.reshape(GMLP_HEADS, CHUNK, 1), vec(og))


def _split3(x):
    hi = x.astype(jnp.bfloat16)
    r1 = x - hi.astype(jnp.float32)
    mid = r1.astype(jnp.bfloat16)
    lo = (r1 - mid.astype(jnp.float32)).astype(jnp.bfloat16)
    return hi, mid, lo


def _fcum_kernel(f_ref, bf_ref, c_ref, carry_ref, *, ts):
    s = pl.program_id(1)

    @pl.when(s == 0)
    def _():
        carry_ref[...] = jnp.zeros_like(carry_ref)

    x = f_ref[...] + bf_ref[...]
    ls = -(jnp.maximum(-x, 0.0) + jnp.log(1.0 + jnp.exp(-jnp.abs(x))))
    t_idx = lax.broadcasted_iota(jnp.int32, (ts, ts), 0)
    s_idx = lax.broadcasted_iota(jnp.int32, (ts, ts), 1)
    tri = jnp.where(s_idx <= t_idx, 1.0, 0.0).astype(jnp.bfloat16)
    hi, mid, lo = _split3(ls)
    c = (jnp.dot(tri, lo, preferred_element_type=jnp.float32)
         + jnp.dot(tri, mid, preferred_element_type=jnp.float32)
         + jnp.dot(tri, hi, preferred_element_type=jnp.float32)) + carry_ref[...]
    c_ref[...] = c * LOG2E
    carry_ref[...] = c[ts - 1:ts, :]


def _forget_cumsum(f, b_f_row, bsz, seq, ts):
    n_tok = f.shape[0]
    n_s = seq // ts
    return pl.pallas_call(
        functools.partial(_fcum_kernel, ts=ts),
        grid=(bsz, n_s),
        in_specs=[
            pl.BlockSpec((ts, LANES), lambda b, s: (b * n_s + s, 0)),
            _const_spec((1, LANES)),
        ],
        out_specs=pl.BlockSpec((ts, LANES), lambda b, s: (b * n_s + s, 0)),
        out_shape=jax.ShapeDtypeStruct((n_tok, LANES), jnp.float32),
        scratch_shapes=[pltpu.VMEM((1, LANES), jnp.float32)],
        compiler_params=_params(("arbitrary", "arbitrary"), 32 << 20),
        name="forget_cumsum",
    )(f, b_f_row)


def _bias_columns(c_tile, h, as_query):
    rows = c_tile.shape[0]
    lane = lax.broadcasted_iota(jnp.int32, (rows, LANES), 1)
    col = jnp.sum(jnp.where(lane == h, c_tile, 0.0), axis=1, keepdims=True)
    val = jnp.broadcast_to(col if as_query else -col, (rows, LANES))
    hi, mid, lo = (t.astype(jnp.float32) for t in _split3(val))
    base = 0 if as_query else 3
    out = jnp.where(lane < 6, 1.0, 0.0)
    for offset, term in enumerate((hi, mid, lo)):
        out = jnp.where(lane == base + offset, term, out)
    return out.astype(jnp.bfloat16)


def _fox_kernel(q_ref, k_ref, v_ref, cq_ref, ck_ref, o_ref, kaug_ref, vt_ref, qaug_ref, m_ref, l_ref, acc_ref, s_ref,
                *, tq, tk, cw, seq):
    h = pl.program_id(1)
    i = pl.program_id(2)
    n_chain = tq // cw

    @pl.when(i == 0)
    def _():
        for r in range(seq // tk):
            rows = slice(r * tk, (r + 1) * tk)
            kaug_ref[rows, :HEAD_DIM] = k_ref[rows, :]
            kaug_ref[rows, HEAD_DIM:] = _bias_columns(ck_ref[rows, :], h, as_query=False)
            vt_ref[:, rows] = v_ref[rows, :].astype(jnp.float32).T.astype(jnp.bfloat16)

    qaug_ref[:, :HEAD_DIM] = q_ref[...]
    qaug_ref[:, HEAD_DIM:] = _bias_columns(cq_ref[...], h, as_query=True)
    m_ref[...] = jnp.full(m_ref.shape, -jnp.inf, jnp.float32)
    l_ref[...] = jnp.zeros_like(l_ref)
    acc_ref[...] = jnp.zeros_like(acc_ref)

    def scores(slot, c, k0):
        s_ref[slot] = lax.dot_general(kaug_ref[pl.ds(k0, tk), :], qaug_ref[c * cw:(c + 1) * cw, :],
                                      (((1,), (1,)), ((), ())), preferred_element_type=jnp.float32)

    def absorb(slot, c, k0, key_minus_query):
        s = s_ref[slot]
        if key_minus_query is not None:
            key = lax.broadcasted_iota(jnp.int32, (tk, cw), 0) + key_minus_query
            qry = lax.broadcasted_iota(jnp.int32, (tk, cw), 1)
            s = jnp.where(key <= qry, s, -jnp.inf)
        m_prev = m_ref[c]
        m_new = jnp.maximum(m_prev, jnp.max(s, axis=0, keepdims=True))
        alpha = jnp.exp2(m_prev - m_new)
        p = jnp.exp2(s - m_new)
        l_ref[c] = alpha * l_ref[c] + jnp.sum(p, axis=0, keepdims=True)
        pv = jnp.dot(vt_ref[:, pl.ds(k0, tk)], p.astype(jnp.bfloat16), preferred_element_type=jnp.float32)
        acc_ref[c] = alpha * acc_ref[c] + pv
        m_ref[c] = m_new

    n_kb = tq // tk

    def query_tile_keys(base, masks):
        work = [(kb, c) for kb in range(n_kb) for c in range(n_chain) if masks[kb][c] is not False]
        for slot, (kb, c) in enumerate(work):
            scores(slot, c, pl.multiple_of(base + kb * tk, tk))
        for slot, (kb, c) in enumerate(work):
            absorb(slot, c, pl.multiple_of(base + kb * tk, tk), masks[kb][c])

    def body(j, carry):
        query_tile_keys(j * tq, [[None] * n_chain] * n_kb)
        return carry

    lax.fori_loop(0, i, body, 0)
    masks = []
    for kb in range(n_kb):
        row = []
        for c in range(n_chain):
            first_key, first_query = kb * tk, c * cw
            if first_key + tk - 1 <= first_query:
                row.append(None)
            elif first_key <= first_query + cw - 1:
                row.append(first_key - first_query)
            else:
                row.append(False)
        masks.append(row)
    query_tile_keys(i * tq, masks)
    for c in range(n_chain):
        o_ref[c * cw:(c + 1) * cw, :] = (acc_ref[c] / l_ref[c]).T.astype(jnp.bfloat16)


def _fox_attention(q, k, v, c, bsz, seq, tq, tk, cw):
    n_tok = q.shape[0]
    nq = seq // tq
    n_chain = tq // cw
    return pl.pallas_call(
        functools.partial(_fox_kernel, tq=tq, tk=tk, cw=cw, seq=seq),
        grid=(bsz, ATT_HEADS, nq),
        in_specs=[
            pl.BlockSpec((tq, HEAD_DIM), lambda b, h, i: (b * nq + i, h)),
            pl.BlockSpec((seq, HEAD_DIM), lambda b, h, i: (b, h)),
            pl.BlockSpec((seq, HEAD_DIM), lambda b, h, i: (b, h)),
            pl.BlockSpec((tq, LANES), lambda b, h, i: (b * nq + i, 0)),
            pl.BlockSpec((seq, LANES), lambda b, h, i: (b, 0)),
        ],
        out_specs=pl.BlockSpec((tq, HEAD_DIM), lambda b, h, i: (b * nq + i, h)),
        out_shape=jax.ShapeDtypeStruct((n_tok, ATT_CH), jnp.bfloat16),
        scratch_shapes=[
            pltpu.VMEM((seq, 2 * HEAD_DIM), jnp.bfloat16),
            pltpu.VMEM((HEAD_DIM, seq), jnp.bfloat16),
            pltpu.VMEM((tq, 2 * HEAD_DIM), jnp.bfloat16),
            pltpu.VMEM((n_chain, 1, cw), jnp.float32),
            pltpu.VMEM((n_chain, 1, cw), jnp.float32),
            pltpu.VMEM((n_chain, HEAD_DIM, cw), jnp.float32),
            pltpu.VMEM((n_chain * (tq // tk), tk, cw), jnp.float32),
        ],
        compiler_params=_params(("arbitrary", "arbitrary", "arbitrary"), 40 << 20),
        name="fox_attention",
    )(q, k, v, c, c)


def _pack_bf16_pairs(x):
    n = x.shape[1] // 2
    r = x.astype(jnp.bfloat16).astype(jnp.float32)
    lo = lax.bitcast_convert_type(r[:, :n], jnp.uint32)
    hi = lax.bitcast_convert_type(r[:, n:], jnp.uint32)
    return (lo >> 16) | (hi & jnp.uint32(0xFFFF0000))


def _unpack_bf16_pairs(w):
    lo = lax.bitcast_convert_type(w << 16, jnp.float32).astype(jnp.bfloat16)
    hi = lax.bitcast_convert_type(w & jnp.uint32(0xFFFF0000), jnp.float32).astype(jnp.bfloat16)
    return lo, hi


def _outproj_kernel(ya_ref, yb_ref, yc_ref, x_ref, wo_ref, ogc_ref, g_ref, b_ref, x1_ref, x1p_ref, y_ref, *, alpha):
    y_ref[:, 0:CONV_CH] = ya_ref[...]
    y_ref[:, CONV_CH:CONV_CH + GMLP_CH] = yb_ref[...]
    yc = yc_ref[...].astype(jnp.float32)
    y_ref[:, CONV_CH + GMLP_CH:] = (_rms_rows(yc) * ogc_ref[...]).astype(jnp.bfloat16)
    h = jnp.dot(y_ref[...], wo_ref[...], preferred_element_type=jnp.float32)
    x1 = _layer_norm_rows(alpha * x_ref[...] + h, g_ref[...], b_ref[...])
    x1_ref[...] = x1
    x1p_ref[...] = _pack_bf16_pairs(x1)


def _out_projection(ya, yb, yc, x2, w_o, og_c, ln_g, ln_b, alpha, tm):
    n_tok, d = x2.shape
    mix = w_o.shape[0]
    row = lambda i: (i, 0)
    vmem = (mix * d * 2 + 2 * tm * d * 4 * 2 + 2 * tm * (d // 2) * 4 + 2 * tm * mix * 2 + tm * mix * 2 + 3 * tm * d * 4)
    return pl.pallas_call(
        functools.partial(_outproj_kernel, alpha=alpha),
        grid=(n_tok // tm,),
        in_specs=[
            pl.BlockSpec((tm, CONV_CH), row),
            pl.BlockSpec((tm, GMLP_CH), row),
            pl.BlockSpec((tm, ATT_CH), row),
            pl.BlockSpec((tm, d), row),
            _const_spec((mix, d)),
            _const_spec((1, ATT_CH)),
            _const_spec((1, d)),
            _const_spec((1, d)),
        ],
        out_specs=[pl.BlockSpec((tm, d), row), pl.BlockSpec((tm, d // 2), row)],
        out_shape=(
            jax.ShapeDtypeStruct((n_tok, d), jnp.float32),
            jax.ShapeDtypeStruct((n_tok, d // 2), jnp.uint32),
        ),
        scratch_shapes=[pltpu.VMEM((tm, mix), jnp.bfloat16)],
        compiler_params=_params(("arbitrary",), vmem + (8 << 20)),
        name="out_projection",
    )(ya, yb, yc, x2, w_o, og_c.reshape(1, ATT_CH), ln_g.reshape(1, d), ln_b.reshape(1, d))


def _first_argmax_rows(x, idx):
    m = jnp.max(x, axis=0, keepdims=True)
    first = jnp.min(jnp.where(x == m, idx, x.shape[0]), axis=0, keepdims=True)
    return m, first


def _router_kernel(x_ref, wh_ref, wl_ref, rb_ref, eidx_ref, gate_t_ref, rank_ref, cnt_ref, carry_ref, *, tr):
    i = pl.program_id(0)

    @pl.when(i == 0)
    def _():
        carry_ref[...] = jnp.zeros_like(carry_ref)

    x = x_ref[...]
    xh = x.astype(jnp.bfloat16)
    xl = (x - xh.astype(jnp.float32)).astype(jnp.bfloat16)
    nt = (((1,), (1,)), ((), ()))
    wh = wh_ref[...]
    logits = (lax.dot_general(wl_ref[...], xh, nt, preferred_element_type=jnp.float32)
              + lax.dot_general(wh, xl, nt, preferred_element_type=jnp.float32)
              + lax.dot_general(wh, xh, nt, preferred_element_type=jnp.float32))
    scores = _sigmoid(logits)
    sel = scores + rb_ref[...]

    neg = -jnp.inf
    e_idx = lax.broadcasted_iota(jnp.int32, (N_EXPERTS, tr), 0)
    g_idx = lax.broadcasted_iota(jnp.int32, (N_GROUPS, tr), 0)
    in_idx = lax.broadcasted_iota(jnp.int32, (GROUP_SIZE, tr), 0)

    gs_rows = []
    for g in range(N_GROUPS):
        blk = sel[g * GROUP_SIZE:(g + 1) * GROUP_SIZE, :]
        m1, a1 = _first_argmax_rows(blk, in_idx)
        m2 = jnp.max(jnp.where(in_idx == a1, neg, blk), axis=0, keepdims=True)
        gs_rows.append(m1 + m2)
    gs = jnp.concatenate(gs_rows, axis=0)

    gsel = jnp.zeros((N_GROUPS, tr), jnp.float32)
    for _ in range(TOPK_GROUPS):
        _, a = _first_argmax_rows(gs, g_idx)
        pick = g_idx == a
        gsel = jnp.where(pick, 1.0, gsel)
        gs = jnp.where(pick, neg, gs)
    esel = jnp.concatenate(
        [jnp.broadcast_to(gsel[g:g + 1, :], (GROUP_SIZE, tr)) for g in range(N_GROUPS)], axis=0)
    cand = jnp.where(esel > 0.5, sel, neg)

    picks, gates = [], []
    chosen = jnp.zeros((N_EXPERTS, tr), jnp.float32)
    for _ in range(TOP_K):
        _, a = _first_argmax_rows(cand, e_idx)
        pick = e_idx == a
        picks.append(a)
        gates.append(jnp.sum(jnp.where(pick, scores, 0.0), axis=0, keepdims=True))
        chosen = jnp.where(pick, 1.0, chosen)
        cand = jnp.where(pick, neg, cand)
    gate = jnp.concatenate(gates, axis=0)
    gate = gate / jnp.sum(gate, axis=0, keepdims=True) * ROUTE_SCALE

    r_idx = lax.broadcasted_iota(jnp.int32, (tr, tr), 0)
    c_idx = lax.broadcasted_iota(jnp.int32, (tr, tr), 1)
    upper = jnp.where(r_idx <= c_idx, 1.0, 0.0).astype(jnp.bfloat16)
    incl = jnp.dot(chosen.astype(jnp.bfloat16), upper, preferred_element_type=jnp.float32)
    rank_all = carry_ref[...] + incl - chosen
    carry_ref[...] = carry_ref[...] + incl[:, tr - 1:tr]
    ranks = [jnp.sum(jnp.where(e_idx == a, rank_all, 0.0), axis=0, keepdims=True) for a in picks]

    eidx_ref[...] = jnp.concatenate(picks, axis=0)
    rank_ref[...] = jnp.concatenate(ranks, axis=0).astype(jnp.int32)
    pad = jnp.zeros((LANES - TOP_K, tr), jnp.float32)
    gate_t_ref[...] = jnp.concatenate([gate, pad], axis=0).T
    cnt_ref[...] = jnp.broadcast_to(carry_ref[...], (N_EXPERTS, LANES)).astype(jnp.int32)


def _router(x1, w_r, r_bias, tr):
    n_tok, d = x1.shape
    wt = w_r.T
    wh = wt.astype(jnp.bfloat16)
    wl = (wt - wh.astype(jnp.float32)).astype(jnp.bfloat16)
    col = lambda i: (0, i)
    return pl.pallas_call(
        functools.partial(_router_kernel, tr=tr),
        grid=(n_tok // tr,),
        in_specs=[
            pl.BlockSpec((tr, d), lambda i: (i, 0)),
            _const_spec((N_EXPERTS, d)),
            _const_spec((N_EXPERTS, d)),
            _const_spec((N_EXPERTS, 1)),
        ],
        out_specs=[
            pl.BlockSpec((TOP_K, tr), col),
            pl.BlockSpec((tr, LANES), lambda i: (i, 0)),
            pl.BlockSpec((TOP_K, tr), col),
            pl.BlockSpec((N_EXPERTS, LANES), lambda i: (0, 0)),
        ],
        out_shape=(
            jax.ShapeDtypeStruct((TOP_K, n_tok), jnp.int32),
            jax.ShapeDtypeStruct((n_tok, LANES), jnp.float32),
            jax.ShapeDtypeStruct((TOP_K, n_tok), jnp.int32),
            jax.ShapeDtypeStruct((N_EXPERTS, LANES), jnp.int32),
        ),
        scratch_shapes=[pltpu.VMEM((N_EXPERTS, 1), jnp.float32)],
        compiler_params=_params(("arbitrary",), 32 << 20),
        name="router",
    )(x1, wh, wl, r_bias.reshape(N_EXPERTS, 1))


def _dest_kernel(pstart_ref, eidx_ref, rank_ref, dest_ref):
    e = eidx_ref[...]
    base = jnp.zeros(e.shape, jnp.int32)
    for k in range(N_EXPERTS):
        base = jnp.where(e == k, pstart_ref[k], base)
    dest_ref[...] = base + rank_ref[...]


def _dest_rows(pstart, eidx, rank, tc):
    n_tok = eidx.shape[1]
    col = lambda i, ps: (0, i)
    return pl.pallas_call(
        _dest_kernel,
        grid_spec=pltpu.PrefetchScalarGridSpec(
            num_scalar_prefetch=1,
            grid=(n_tok // tc,),
            in_specs=[pl.BlockSpec((TOP_K, tc), col), pl.BlockSpec((TOP_K, tc), col)],
            out_specs=pl.BlockSpec((TOP_K, tc), col),
        ),
        out_shape=jax.ShapeDtypeStruct((TOP_K, n_tok), jnp.int32),
        compiler_params=_params(("arbitrary",), 32 << 20),
        name="dest_rows",
    )(pstart, eidx, rank)


def _dispatch_kernel(pend_ref, dest_ref, x_ref, xs_ref, zero_ref, sem, *, td, bm):
    i = pl.program_id(0)

    @pl.when(i == 0)
    def _():
        zero_ref[...] = jnp.zeros_like(zero_ref)
        for e in range(N_EXPERTS):
            prev = pend_ref[e - 1] if e > 0 else 0

            @pl.when(pend_ref[e] > prev)
            def _():
                start = pl.multiple_of(pend_ref[e] - bm, bm)
                pltpu.make_async_copy(zero_ref, xs_ref.at[pl.ds(start, bm), :], sem).start()
        first_unused = pend_ref[N_EXPERTS - 1] // bm
        n_blk = xs_ref.shape[0] // bm

        def fill(b, carry):
            pltpu.make_async_copy(zero_ref, xs_ref.at[pl.ds(pl.multiple_of(b * bm, bm), bm), :], sem).start()
            return carry

        def fill_done(b, carry):
            pltpu.make_async_copy(zero_ref, xs_ref.at[pl.ds(0, bm), :], sem).wait()
            return carry

        lax.fori_loop(first_unused, n_blk, fill, 0)
        lax.fori_loop(first_unused, n_blk, fill_done, 0)
        for e in range(N_EXPERTS):
            prev = pend_ref[e - 1] if e > 0 else 0

            @pl.when(pend_ref[e] > prev)
            def _():
                pltpu.make_async_copy(zero_ref, xs_ref.at[pl.ds(0, bm), :], sem).wait()

    def issue(r, carry):
        for j in range(TOP_K):
            d = dest_ref[j, r]
            pltpu.make_async_copy(x_ref.at[pl.ds(r, 1), :], xs_ref.at[pl.ds(d, 1), :], sem).start()
        return carry

    lax.fori_loop(0, td, issue, 0)
    for j in range(TOP_K):
        pltpu.make_async_copy(x_ref, xs_ref.at[pl.ds(0, td), :], sem).wait()


def _dispatch(pend, dest, x1p, n_rows, td, bm):
    n_tok, width = x1p.shape
    return pl.pallas_call(
        functools.partial(_dispatch_kernel, td=td, bm=bm),
        grid_spec=pltpu.PrefetchScalarGridSpec(
            num_scalar_prefetch=1,
            grid=(n_tok // td,),
            in_specs=[
                pl.BlockSpec((TOP_K, td), lambda i, pe: (0, i), memory_space=pltpu.SMEM),
                pl.BlockSpec((td, width), lambda i, pe: (i, 0)),
            ],
            out_specs=pl.BlockSpec(memory_space=pl.ANY),
            scratch_shapes=[pltpu.VMEM((bm, width), jnp.uint32), pltpu.SemaphoreType.DMA],
        ),
        out_shape=jax.ShapeDtypeStruct((n_rows, width), jnp.uint32),
        compiler_params=_params(("arbitrary",), 32 << 20),
        name="dispatch",
    )(pend, dest, x1p)


def _cast_kernel(w_ref, o_ref):
    o_ref[...] = w_ref[...].astype(o_ref.dtype)


def _to_bf16(w, layer):
    _, n_e, a, b = w.shape
    return pl.pallas_call(
        _cast_kernel,
        grid=(n_e,),
        in_specs=[pl.BlockSpec((None, 1, a, b), lambda e: (layer, e, 0, 0))],
        out_specs=pl.BlockSpec((1, a, b), lambda e: (e, 0, 0)),
        out_shape=jax.ShapeDtypeStruct((n_e, a, b), jnp.bfloat16),
        compiler_params=_params(("arbitrary",), 32 << 20),
        name="expert_weights_bf16",
    )(w)


def _experts_kernel(be_ref, nv_ref, xs_ref, w1_ref, w3_ref, w2_ref, ys_ref, xb_ref):
    b = pl.program_id(0)

    @pl.when(b < nv_ref[0])
    def _():
        half = xs_ref.shape[1]
        lo, hi = _unpack_bf16_pairs(xs_ref[...])
        xb_ref[:, :half] = lo
        xb_ref[:, half:] = hi
        xb = xb_ref[...]
        h1 = jnp.dot(xb, w1_ref[0], preferred_element_type=jnp.float32)
        h3 = jnp.dot(xb, w3_ref[0], preferred_element_type=jnp.float32)
        h = (_silu(h1) * h3).astype(jnp.bfloat16)
        ys_ref[...] = _pack_bf16_pairs(jnp.dot(h, w2_ref[0], preferred_element_type=jnp.float32))

    @pl.when(b >= nv_ref[0])
    def _():
        ys_ref[...] = jnp.zeros_like(ys_ref)


def _experts(blk_exp, n_valid, xs, w1, w3, w2, bm):
    n_rows, half = xs.shape
    d = 2 * half
    de = w1.shape[2]
    n_blk = n_rows // bm
    blk = lambda b, be, nv: (jnp.minimum(b, nv[0] - 1), 0)
    wsel = lambda b, be, nv: (be[jnp.minimum(b, nv[0] - 1)], 0, 0)
    vmem = 2 * bm * half * 4 + bm * d * 2 + 2 * 3 * d * de * 2 + 2 * bm * half * 4 + 4 * bm * de * 4 + 2 * bm * d * 4
    return pl.pallas_call(
        _experts_kernel,
        grid_spec=pltpu.PrefetchScalarGridSpec(
            num_scalar_prefetch=2,
            grid=(n_blk,),
            in_specs=[
                pl.BlockSpec((bm, half), blk),
                pl.BlockSpec((1, d, de), wsel),
                pl.BlockSpec((1, d, de), wsel),
                pl.BlockSpec((1, de, d), wsel),
            ],
            out_specs=pl.BlockSpec((bm, half), lambda b, be, nv: (b, 0)),
            scratch_shapes=[pltpu.VMEM((bm, d), jnp.bfloat16)],
        ),
        out_shape=jax.ShapeDtypeStruct((n_rows, half), jnp.uint32),
        compiler_params=_params(("arbitrary",), vmem + (8 << 20)),
        name="experts",
    )(blk_exp, n_valid, xs, w1, w3, w2)


def _combine_kernel(dest_ref, x1_ref, p_ref, gt_ref, w1_ref, w3_ref, w2_ref, wpg_ref, bpg_ref, wpe_ref, g_ref, b_ref,
                    ys_ref, o_ref, ybuf_ref, sem, *, tm, alpha):
    for r in range(tm):
        for j in range(TOP_K):
            d = dest_ref[j, r]
            pltpu.make_async_copy(ys_ref.at[pl.ds(d, 1), :], ybuf_ref.at[j, pl.ds(r, 1), :], sem).start()

    x1 = x1_ref[...]
    xb = x1.astype(jnp.bfloat16)
    h1 = jnp.dot(xb, w1_ref[...], preferred_element_type=jnp.float32)
    h3 = jnp.dot(xb, w3_ref[...], preferred_element_type=jnp.float32)
    h = (_silu(h1) * h3).astype(jnp.bfloat16)
    total = alpha * x1 + jnp.dot(h, w2_ref[...], preferred_element_type=jnp.float32)
    gate = _sigmoid(jnp.dot(xb, wpg_ref[...], preferred_element_type=jnp.float32) + bpg_ref[...])
    pe = jnp.dot(p_ref[...].astype(jnp.bfloat16), wpe_ref[...], preferred_element_type=jnp.float32)
    total = total + gate * pe

    for j in range(TOP_K):
        pltpu.make_async_copy(ys_ref.at[pl.ds(0, tm), :], ybuf_ref.at[j], sem).wait()
    gt = gt_ref[...]
    half = ybuf_ref.shape[2]
    moe_lo = jnp.zeros((tm, half), jnp.float32)
    moe_hi = jnp.zeros((tm, half), jnp.float32)
    for j in range(TOP_K):
        w = ybuf_ref[j]
        g = gt[:, j:j + 1]
        moe_lo = moe_lo + g * lax.bitcast_convert_type(w << 16, jnp.float32)
        moe_hi = moe_hi + g * lax.bitcast_convert_type(w & jnp.uint32(0xFFFF0000), jnp.float32)
    total = total + jnp.concatenate([moe_lo, moe_hi], axis=1)
    o_ref[...] = _layer_norm_rows(total, g_ref[...], b_ref[...])


def _combine(dest, x1, p2, gate_t, w_sh1, w_sh3, w_sh2, w_pg, b_pg, w_pe, ln_g, ln_b, ys, alpha, tm):
    n_tok, d = x1.shape
    ds = w_sh1.shape[1]
    dp = p2.shape[1]
    row = lambda i: (i, 0)
    vmem = (TOP_K * tm * (d // 2) * 4 + (3 * d * ds + d * d + dp * d) * 2 + 2 * 2 * tm * d * 4 + 2 * tm * dp * 4
            + 2 * tm * LANES * 4 + 6 * tm * d * 4)
    return pl.pallas_call(
        functools.partial(_combine_kernel, tm=tm, alpha=alpha),
        grid=(n_tok // tm,),
        in_specs=[
            pl.BlockSpec((TOP_K, tm), lambda i: (0, i), memory_space=pltpu.SMEM),
            pl.BlockSpec((tm, d), row),
            pl.BlockSpec((tm, dp), row),
            pl.BlockSpec((tm, LANES), row),
            _const_spec((d, ds)),
            _const_spec((d, ds)),
            _const_spec((ds, d)),
            _const_spec((d, d)),
            _const_spec((1, d)),
            _const_spec((dp, d)),
            _const_spec((1, d)),
            _const_spec((1, d)),
            pl.BlockSpec(memory_space=pl.ANY),
        ],
        out_specs=pl.BlockSpec((tm, d), row),
        out_shape=jax.ShapeDtypeStruct((n_tok, d), jnp.float32),
        scratch_shapes=[pltpu.VMEM((TOP_K, tm, d // 2), jnp.uint32), pltpu.SemaphoreType.DMA],
        compiler_params=_params(("arbitrary",), vmem + (8 << 20)),
        name="combine",
    )(dest, x1, p2, gate_t, w_sh1, w_sh3, w_sh2, w_pg, b_pg.reshape(1, d), w_pe, ln_g.reshape(1, d),
      ln_b.reshape(1, d), ys)


def _tile(n, target):
    t = min(n, target)
    assert n % t == 0, (n, t)
    return t


def kernel(x, p, w_in, b_f, conv_w, conv_b, conv_ln_g, conv_ln_b, gmlp_ln_g, gmlp_ln_b, w_sp, b_sp, out_g, w_o,
           ln1_g, ln1_b, w_r, r_bias, w_e1, w_e3, w_e2, w_sh1, w_sh3, w_sh2, w_pe, w_pg, b_pg, ln2_g, ln2_b):
    bsz, seq, d = x.shape
    depth = w_in.shape[0]
    n_tok = bsz * seq
    alpha = (2.0 * depth) ** 0.25
    n_main = 2 * CONV_CH + 2 * GMLP_CH + 3 * ATT_CH
    assert w_in.shape[2] == n_main + ATT_HEADS and seq % CHUNK == 0

    bm = MOE_BLOCK
    n_blk = -(-n_tok * TOP_K // bm) + N_EXPERTS
    n_rows = n_blk * bm
    bf16 = jnp.bfloat16

    x2 = x.reshape(n_tok, d)
    for i in range(depth):
        w_main = w_in[i, :, :n_main].astype(bf16)
        w_f = jnp.pad(w_in[i, :, n_main:], ((0, 0), (0, LANES - ATT_HEADS))).astype(bf16)
        b_f_row = jnp.pad(b_f[i], (0, LANES - ATT_HEADS)).reshape(1, LANES)

        zc, zg, q, k, v, f = _in_projection(x2, w_main, w_f, _tile(n_tok, 512))
        ya = _conv_module(zc, conv_w[i], conv_b[i], conv_ln_g[i], conv_ln_b[i], out_g[i, :CONV_CH],
                          bsz, seq, _tile(seq, 256))
        yb = _gmlp_module(zg, gmlp_ln_g[i], gmlp_ln_b[i], w_sp[i], b_sp[i],
                          out_g[i, CONV_CH:CONV_CH + GMLP_CH], _tile(n_tok, 512))
        c = _forget_cumsum(f, b_f_row, bsz, seq, _tile(seq, 512))
        tq = _tile(seq, 1024)
        yc = _fox_attention(q, k, v, c, bsz, seq, tq, _tile(tq, 512), _tile(tq, 512))
        x1, x1p = _out_projection(ya, yb, yc, x2, w_o[i].astype(bf16), out_g[i, CONV_CH + GMLP_CH:],
                                  ln1_g[i], ln1_b[i], alpha, _tile(n_tok, 256))

        eidx, gate_t, rank, cnt = _router(x1, w_r[i], r_bias[i], _tile(n_tok, 512))
        counts = cnt[:, 0]
        padded = (counts + bm - 1) // bm * bm
        pend = jnp.cumsum(padded).astype(jnp.int32)
        pstart = pend - padded
        blk_start = jnp.arange(n_blk, dtype=jnp.int32) * bm
        blk_exp = jnp.minimum(jnp.sum(pend[None, :] <= blk_start[:, None], axis=1), N_EXPERTS - 1).astype(jnp.int32)
        n_valid = (pend[N_EXPERTS - 1:] // bm).astype(jnp.int32)

        dest = _dest_rows(pstart, eidx, rank, _tile(n_tok, 4096))
        xs = _dispatch(pend, dest, x1p, n_rows, _tile(n_tok, 256), bm)
        ys = _experts(blk_exp, n_valid, xs, _to_bf16(w_e1, i), _to_bf16(w_e3, i), _to_bf16(w_e2, i), bm)
        x2 = _combine(dest, x1, p[i].reshape(n_tok, -1), gate_t, w_sh1[i].astype(bf16), w_sh3[i].astype(bf16),
                      w_sh2[i].astype(bf16), w_pg[i].astype(bf16), b_pg[i], w_pe[i].astype(bf16),
                      ln2_g[i], ln2_b[i], ys, alpha, _tile(n_tok, 256))
    return x2.reshape(bsz, seq, d)
```

```python
import functools

import jax
import jax.numpy as jnp
from jax import lax
from jax.experimental import pallas as pl
from jax.experimental.pallas import tpu as pltpu

CONV_CH = 512
CONV_WIDTH = 31
GMLP_HEADS = 4
GMLP_HEAD_CH = 128
GMLP_CH = GMLP_HEADS * GMLP_HEAD_CH
CHUNK = 128
ATT_HEADS = 8
HEAD_DIM = 128
ATT_CH = ATT_HEADS * HEAD_DIM
N_EXPERTS = 64
TOP_K = 8
N_GROUPS = 8
GROUP_SIZE = N_EXPERTS // N_GROUPS
TOPK_GROUPS = 4
ROUTE_SCALE = 2.5
LN_EPS = 1e-5
LOG2E = 1.4426950408889634

LANES = 128
SUBLANES = 8
VMEM_BYTES_V7X = 64 * 1024 * 1024

CONV_HALO = 32
CONV_ROWS = 32
MOE_BLOCK = 512


def _vmem_limit(nbytes):
    return int(min(nbytes, VMEM_BYTES_V7X - 8 * 1024 * 1024))


def _params(semantics, vmem_bytes):
    return pltpu.CompilerParams(dimension_semantics=semantics, vmem_limit_bytes=_vmem_limit(vmem_bytes))


def _const_spec(shape):
    nd = len(shape)
    return pl.BlockSpec(shape, lambda *_: (0,) * nd, pipeline_mode=pl.Buffered(1))


def _layer_norm_rows(x, g, b):
    mu = jnp.mean(x, axis=-1, keepdims=True)
    xc = x - mu
    var = jnp.mean(xc * xc, axis=-1, keepdims=True)
    return xc * lax.rsqrt(var + LN_EPS) * g + b


def _rms_rows(x):
    return x * lax.rsqrt(jnp.mean(x * x, axis=-1, keepdims=True) + LN_EPS)


def _sigmoid(x):
    return 1.0 / (1.0 + jnp.exp(-x))


def _silu(x):
    return x * _sigmoid(x)


def _inproj_kernel(x_ref, w_ref, wf_ref, zc_ref, zg_ref, q_ref, k_ref, v_ref, f_ref, *, q_scale):
    xb = x_ref[...].astype(jnp.bfloat16)
    wide = zc_ref.shape[1]

    def slab(n):
        return jnp.dot(xb, w_ref[:, n * wide:(n + 1) * wide], preferred_element_type=jnp.float32)

    f_ref[...] = jnp.dot(xb, wf_ref[...], preferred_element_type=jnp.float32)
    zc_ref[...] = slab(0)
    zg_ref[...] = slab(1)
    q_ref[...] = (slab(2) * q_scale).astype(jnp.bfloat16)
    k_ref[...] = slab(3).astype(jnp.bfloat16)
    v_ref[...] = slab(4).astype(jnp.bfloat16)


def _in_projection(x2, w_main, w_f, tm):
    n_tok, d = x2.shape
    wide = 2 * CONV_CH
    assert w_main.shape[1] == 5 * wide and ATT_CH == wide and 2 * GMLP_CH == wide
    row = lambda i: (i, 0)
    out_shape = (
        jax.ShapeDtypeStruct((n_tok, wide), jnp.float32),
        jax.ShapeDtypeStruct((n_tok, wide), jnp.float32),
        jax.ShapeDtypeStruct((n_tok, ATT_CH), jnp.bfloat16),
        jax.ShapeDtypeStruct((n_tok, ATT_CH), jnp.bfloat16),
        jax.ShapeDtypeStruct((n_tok, ATT_CH), jnp.bfloat16),
        jax.ShapeDtypeStruct((n_tok, LANES), jnp.float32),
    )
    vmem = (2 * tm * d * 4 + tm * d * 2 + d * 5 * wide * 2 + d * LANES * 2
            + 2 * 2 * tm * wide * 4 + 3 * 2 * tm * wide * 2 + 2 * tm * LANES * 4 + 2 * tm * wide * 4)
    return pl.pallas_call(
        functools.partial(_inproj_kernel, q_scale=HEAD_DIM ** -0.5 * LOG2E),
        grid=(n_tok // tm,),
        in_specs=[
            pl.BlockSpec((tm, d), row),
            _const_spec((d, 5 * wide)),
            _const_spec((d, LANES)),
        ],
        out_specs=[
            pl.BlockSpec((tm, wide), row),
            pl.BlockSpec((tm, wide), row),
            pl.BlockSpec((tm, ATT_CH), row),
            pl.BlockSpec((tm, ATT_CH), row),
            pl.BlockSpec((tm, ATT_CH), row),
            pl.BlockSpec((tm, LANES), row),
        ],
        out_shape=out_shape,
        compiler_params=_params(("arbitrary",), vmem + (8 << 20)),
        name="in_projection",
    )(x2, w_main, w_f)


def _conv_kernel(z_ref, w_ref, cb_ref, lng_ref, lnb_ref, og_ref, o_ref, hbuf_ref, sh_ref, *, ts):
    s = pl.program_id(1)
    span = CONV_HALO + ts - SUBLANES

    @pl.when(s == 0)
    def _():
        hbuf_ref[0:CONV_HALO, :] = jnp.zeros((CONV_HALO, CONV_CH), jnp.float32)

    @pl.when(s > 0)
    def _():
        hbuf_ref[0:CONV_HALO, :] = hbuf_ref[ts:ts + CONV_HALO, :]

    hbuf_ref[CONV_HALO:CONV_HALO + ts, :] = z_ref[:, :CONV_CH] * _sigmoid(z_ref[:, CONV_CH:])

    cb = cb_ref[...]
    lng = lng_ref[...]
    lnb = lnb_ref[...]
    og = og_ref[...]
    for r in range(1, SUBLANES):
        sh_ref[r - 1] = hbuf_ref[r:r + span, :]
    first = CONV_HALO - (CONV_WIDTH - 1)
    for c in range(ts // CONV_ROWS):
        acc = jnp.broadcast_to(cb, (CONV_ROWS, CONV_CH))
        for j in range(CONV_WIDTH):
            off = first + j + c * CONV_ROWS
            r = off % SUBLANES
            a = off - r
            rows = hbuf_ref[a:a + CONV_ROWS, :] if r == 0 else sh_ref[r - 1, a:a + CONV_ROWS, :]
            acc = acc + w_ref[j:j + 1, :] * rows
        y = _silu(_layer_norm_rows(acc, lng, lnb))
        o_ref[c * CONV_ROWS:(c + 1) * CONV_ROWS, :] = (_rms_rows(y) * og).astype(jnp.bfloat16)


def _conv_module(zc, conv_w, conv_b, ln_g, ln_b, og, bsz, seq, ts):
    n_tok = zc.shape[0]
    n_s = seq // ts
    vec = lambda v: v.reshape(1, CONV_CH)
    return pl.pallas_call(
        functools.partial(_conv_kernel, ts=ts),
        grid=(bsz, n_s),
        in_specs=[
            pl.BlockSpec((ts, 2 * CONV_CH), lambda b, s: (b * n_s + s, 0)),
            _const_spec((CONV_WIDTH, CONV_CH)),
            _const_spec((1, CONV_CH)),
            _const_spec((1, CONV_CH)),
            _const_spec((1, CONV_CH)),
            _const_spec((1, CONV_CH)),
        ],
        out_specs=pl.BlockSpec((ts, CONV_CH), lambda b, s: (b * n_s + s, 0)),
        out_shape=jax.ShapeDtypeStruct((n_tok, CONV_CH), jnp.bfloat16),
        scratch_shapes=[pltpu.VMEM((CONV_HALO + ts, CONV_CH), jnp.float32),
                        pltpu.VMEM((SUBLANES - 1, CONV_HALO + ts - SUBLANES, CONV_CH), jnp.float32)],
        compiler_params=_params(("arbitrary", "arbitrary"), 32 << 20),
        name="conv_module",
    )(zc, conv_w, vec(conv_b), vec(ln_g), vec(ln_b), vec(og))


def _gelu_tanh(x):
    c = (2.0 / jnp.pi) ** 0.5
    return 0.5 * x * (1.0 + jnp.tanh(c * (x + 0.044715 * (x * x * x))))


def _gmlp_kernel(z_ref, lng_ref, lnb_ref, wsp_ref, bsp_ref, og_ref, o_ref, y_ref, *, tg):
    z = _gelu_tanh(z_ref[...])
    u = z[:, :GMLP_CH]
    v = _layer_norm_rows(z[:, GMLP_CH:], lng_ref[...], lnb_ref[...]).astype(jnp.bfloat16)
    t_idx = lax.broadcasted_iota(jnp.int32, (CHUNK, CHUNK), 0)
    s_idx = lax.broadcasted_iota(jnp.int32, (CHUNK, CHUNK), 1)
    causal = s_idx <= t_idx
    for h in range(GMLP_HEADS):
        ws = jnp.where(causal, wsp_ref[h], 0.0).astype(jnp.bfloat16)
        bias = bsp_ref[h]
        cols = slice(h * GMLP_HEAD_CH, (h + 1) * GMLP_HEAD_CH)
        for c in range(tg // CHUNK):
            rows = slice(c * CHUNK, (c + 1) * CHUNK)
            mixed = jnp.dot(ws, v[rows, cols], preferred_element_type=jnp.float32) + bias
            y_ref[rows, cols] = u[rows, cols] * mixed
    o_ref[...] = (_rms_rows(y_ref[...]) * og_ref[...]).astype(jnp.bfloat16)


def _gmlp_module(zg, ln_g, ln_b, w_sp, b_sp, og, tg):
    n_tok = zg.shape[0]
    vec = lambda v: v.reshape(1, GMLP_CH)
    return pl.pallas_call(
        functools.partial(_gmlp_kernel, tg=tg),
        grid=(n_tok // tg,),
        in_specs=[
            pl.BlockSpec((tg, 2 * GMLP_CH), lambda i: (i, 0)),
            _const_spec((1, GMLP_CH)),
            _const_spec((1, GMLP_CH)),
            _const_spec((GMLP_HEADS, CHUNK, CHUNK)),
            _const_spec((GMLP_HEADS, CHUNK, 1)),
            _const_spec((1, GMLP_CH)),
        ],
        out_specs=pl.BlockSpec((tg, GMLP_CH), lambda i: (i, 0)),
        out_shape=jax.ShapeDtypeStruct((n_tok, GMLP_CH), jnp.bfloat16),
        scratch_shapes=[pltpu.VMEM((tg, GMLP_CH), jnp.float32)],
        compiler_params=_params(("arbitrary",), 32 << 20),
        name="gmlp_module",
    )(zg, vec(ln_g), vec(ln_b), w_sp, b_sp.reshape(GMLP_HEADS, CHUNK, 1), vec(og))


def _split3(x):
    hi = x.astype(jnp.bfloat16)
    r1 = x - hi.astype(jnp.float32)
    mid = r1.astype(jnp.bfloat16)
    lo = (r1 - mid.astype(jnp.float32)).astype(jnp.bfloat16)
    return hi, mid, lo


def _fcum_kernel(f_ref, bf_ref, c_ref, carry_ref, *, ts):
    s = pl.program_id(1)

    @pl.when(s == 0)
    def _():
        carry_ref[...] = jnp.zeros_like(carry_ref)

    x = f_ref[...] + bf_ref[...]
    ls = -(jnp.maximum(-x, 0.0) + jnp.log(1.0 + jnp.exp(-jnp.abs(x))))
    t_idx = lax.broadcasted_iota(jnp.int32, (ts, ts), 0)
    s_idx = lax.broadcasted_iota(jnp.int32, (ts, ts), 1)
    tri = jnp.where(s_idx <= t_idx, 1.0, 0.0).astype(jnp.bfloat16)
    hi, mid, lo = _split3(ls)
    c = (jnp.dot(tri, lo, preferred_element_type=jnp.float32)
         + jnp.dot(tri, mid, preferred_element_type=jnp.float32)
         + jnp.dot(tri, hi, preferred_element_type=jnp.float32)) + carry_ref[...]
    c_ref[...] = c * LOG2E
    carry_ref[...] = c[ts - 1:ts, :]


def _forget_cumsum(f, b_f_row, bsz, seq, ts):
    n_tok = f.shape[0]
    n_s = seq // ts
    return pl.pallas_call(
        functools.partial(_fcum_kernel, ts=ts),
        grid=(bsz, n_s),
        in_specs=[
            pl.BlockSpec((ts, LANES), lambda b, s: (b * n_s + s, 0)),
            _const_spec((1, LANES)),
        ],
        out_specs=pl.BlockSpec((ts, LANES), lambda b, s: (b * n_s + s, 0)),
        out_shape=jax.ShapeDtypeStruct((n_tok, LANES), jnp.float32),
        scratch_shapes=[pltpu.VMEM((1, LANES), jnp.float32)],
        compiler_params=_params(("arbitrary", "arbitrary"), 32 << 20),
        name="forget_cumsum",
    )(f, b_f_row)


def _bias_columns(c_tile, h, as_query):
    rows = c_tile.shape[0]
    lane = lax.broadcasted_iota(jnp.int32, (rows, LANES), 1)
    col = jnp.sum(jnp.where(lane == h, c_tile, 0.0), axis=1, keepdims=True)
    val = jnp.broadcast_to(col if as_query else -col, (rows, LANES))
    hi, mid, lo = (t.astype(jnp.float32) for t in _split3(val))
    base = 0 if as_query else 3
    out = jnp.where(lane < 6, 1.0, 0.0)
    for offset, term in enumerate((hi, mid, lo)):
        out = jnp.where(lane == base + offset, term, out)
    return out.astype(jnp.bfloat16)


def _fox_kernel(q_ref, k_ref, v_ref, cq_ref, ck_ref, o_ref, kaug_ref, vt_ref, qaug_ref, m_ref, l_ref, acc_ref, s_ref,
                *, tq, tk, cw, seq):
    h = pl.program_id(1)
    i = pl.program_id(2)
    n_chain = tq // cw

    @pl.when(i == 0)
    def _():
        for r in range(seq // tk):
            rows = slice(r * tk, (r + 1) * tk)
            kaug_ref[rows, :HEAD_DIM] = k_ref[rows, :]
            kaug_ref[rows, HEAD_DIM:] = _bias_columns(ck_ref[rows, :], h, as_query=False)
            vt_ref[:, rows] = v_ref[rows, :].astype(jnp.float32).T.astype(jnp.bfloat16)

    qaug_ref[:, :HEAD_DIM] = q_ref[...]
    qaug_ref[:, HEAD_DIM:] = _bias_columns(cq_ref[...], h, as_query=True)
    m_ref[...] = jnp.full(m_ref.shape, -jnp.inf, jnp.float32)
    l_ref[...] = jnp.zeros_like(l_ref)
    acc_ref[...] = jnp.zeros_like(acc_ref)

    def scores(slot, c, k0):
        s_ref[slot] = lax.dot_general(kaug_ref[pl.ds(k0, tk), :], qaug_ref[c * cw:(c + 1) * cw, :],
                                      (((1,), (1,)), ((), ())), preferred_element_type=jnp.float32)

    def absorb(slot, c, k0, key_minus_query):
        s = s_ref[slot]
        if key_minus_query is not None:
            key = lax.broadcasted_iota(jnp.int32, (tk, cw), 0) + key_minus_query
            qry = lax.broadcasted_iota(jnp.int32, (tk, cw), 1)
            s = jnp.where(key <= qry, s, -jnp.inf)
        m_prev = m_ref[c]
        m_new = jnp.maximum(m_prev, jnp.max(s, axis=0, keepdims=True))
        alpha = jnp.exp2(m_prev - m_new)
        p = jnp.exp2(s - m_new)
        l_ref[c] = alpha * l_ref[c] + jnp.sum(p, axis=0, keepdims=True)
        pv = jnp.dot(vt_ref[:, pl.ds(k0, tk)], p.astype(jnp.bfloat16), preferred_element_type=jnp.float32)
        acc_ref[c] = alpha * acc_ref[c] + pv
        m_ref[c] = m_new

    n_kb = tq // tk

    def query_tile_keys(base, masks):
        work = [(kb, c) for kb in range(n_kb) for c in range(n_chain) if masks[kb][c] is not False]
        for slot, (kb, c) in enumerate(work):
            scores(slot, c, pl.multiple_of(base + kb * tk, tk))
        for slot, (kb, c) in enumerate(work):
            absorb(slot, c, pl.multiple_of(base + kb * tk, tk), masks[kb][c])

    def body(j, carry):
        query_tile_keys(j * tq, [[None] * n_chain] * n_kb)
        return carry

    lax.fori_loop(0, i, body, 0)
    masks = []
    for kb in range(n_kb):
        row = []
        for c in range(n_chain):
            first_key, first_query = kb * tk, c * cw
            if first_key + tk - 1 <= first_query:
                row.append(None)
            elif first_key <= first_query + cw - 1:
                row.append(first_key - first_query)
            else:
                row.append(False)
        masks.append(row)
    query_tile_keys(i * tq, masks)
    for c in range(n_chain):
        o_ref[c * cw:(c + 1) * cw, :] = (acc_ref[c] / l_ref[c]).T.astype(jnp.bfloat16)


def _fox_attention(q, k, v, c, bsz, seq, tq, tk, cw):
    n_tok = q.shape[0]
    nq = seq // tq
    n_chain = tq // cw
    return pl.pallas_call(
        functools.partial(_fox_kernel, tq=tq, tk=tk, cw=cw, seq=seq),
        grid=(bsz, ATT_HEADS, nq),
        in_specs=[
            pl.BlockSpec((tq, HEAD_DIM), lambda b, h, i: (b * nq + i, h)),
            pl.BlockSpec((seq, HEAD_DIM), lambda b, h, i: (b, h)),
            pl.BlockSpec((seq, HEAD_DIM), lambda b, h, i: (b, h)),
            pl.BlockSpec((tq, LANES), lambda b, h, i: (b * nq + i, 0)),
            pl.BlockSpec((seq, LANES), lambda b, h, i: (b, 0)),
        ],
        out_specs=pl.BlockSpec((tq, HEAD_DIM), lambda b, h, i: (b * nq + i, h)),
        out_shape=jax.ShapeDtypeStruct((n_tok, ATT_CH), jnp.bfloat16),
        scratch_shapes=[
            pltpu.VMEM((seq, 2 * HEAD_DIM), jnp.bfloat16),
            pltpu.VMEM((HEAD_DIM, seq), jnp.bfloat16),
            pltpu.VMEM((tq, 2 * HEAD_DIM), jnp.bfloat16),
            pltpu.VMEM((n_chain, 1, cw), jnp.float32),
            pltpu.VMEM((n_chain, 1, cw), jnp.float32),
            pltpu.VMEM((n_chain, HEAD_DIM, cw), jnp.float32),
            pltpu.VMEM((n_chain * (tq // tk), tk, cw), jnp.float32),
        ],
        compiler_params=_params(("arbitrary", "arbitrary", "arbitrary"), 56 << 20),
        name="fox_attention",
    )(q, k, v, c, c)


def _pack_bf16_pairs(x):
    n = x.shape[1] // 2
    r = x.astype(jnp.bfloat16).astype(jnp.float32)
    lo = lax.bitcast_convert_type(r[:, :n], jnp.uint32)
    hi = lax.bitcast_convert_type(r[:, n:], jnp.uint32)
    return (lo >> 16) | (hi & jnp.uint32(0xFFFF0000))


def _unpack_bf16_pairs(w):
    lo = lax.bitcast_convert_type(w << 16, jnp.float32).astype(jnp.bfloat16)
    hi = lax.bitcast_convert_type(w & jnp.uint32(0xFFFF0000), jnp.float32).astype(jnp.bfloat16)
    return lo, hi


def _outproj_kernel(ya_ref, yb_ref, yc_ref, x_ref, wo_ref, ogc_ref, g_ref, b_ref, x1_ref, x1p_ref, y_ref, *, alpha):
    y_ref[:, 0:CONV_CH] = ya_ref[...]
    y_ref[:, CONV_CH:CONV_CH + GMLP_CH] = yb_ref[...]
    yc = yc_ref[...].astype(jnp.float32)
    y_ref[:, CONV_CH + GMLP_CH:] = (_rms_rows(yc) * ogc_ref[...]).astype(jnp.bfloat16)
    h = jnp.dot(y_ref[...], wo_ref[...], preferred_element_type=jnp.float32)
    x1 = _layer_norm_rows(alpha * x_ref[...] + h, g_ref[...], b_ref[...])
    x1_ref[...] = x1
    x1p_ref[...] = _pack_bf16_pairs(x1)


def _out_projection(ya, yb, yc, x2, w_o, og_c, ln_g, ln_b, alpha, tm):
    n_tok, d = x2.shape
    mix = w_o.shape[0]
    row = lambda i: (i, 0)
    vmem = (mix * d * 2 + 2 * tm * d * 4 * 2 + 2 * tm * (d // 2) * 4 + 2 * tm * mix * 2 + tm * mix * 2 + 3 * tm * d * 4)
    return pl.pallas_call(
        functools.partial(_outproj_kernel, alpha=alpha),
        grid=(n_tok // tm,),
        in_specs=[
            pl.BlockSpec((tm, CONV_CH), row),
            pl.BlockSpec((tm, GMLP_CH), row),
            pl.BlockSpec((tm, ATT_CH), row),
            pl.BlockSpec((tm, d), row),
            _const_spec((mix, d)),
            _const_spec((1, ATT_CH)),
            _const_spec((1, d)),
            _const_spec((1, d)),
        ],
        out_specs=[pl.BlockSpec((tm, d), row), pl.BlockSpec((tm, d // 2), row)],
        out_shape=(
            jax.ShapeDtypeStruct((n_tok, d), jnp.float32),
            jax.ShapeDtypeStruct((n_tok, d // 2), jnp.uint32),
        ),
        scratch_shapes=[pltpu.VMEM((tm, mix), jnp.bfloat16)],
        compiler_params=_params(("arbitrary",), vmem + (8 << 20)),
        name="out_projection",
    )(ya, yb, yc, x2, w_o, og_c.reshape(1, ATT_CH), ln_g.reshape(1, d), ln_b.reshape(1, d))


def _first_argmax_rows(x, idx):
    m = jnp.max(x, axis=0, keepdims=True)
    first = jnp.min(jnp.where(x == m, idx, x.shape[0]), axis=0, keepdims=True)
    return m, first


def _router_kernel(x_ref, wh_ref, wl_ref, rb_ref, eidx_ref, gate_t_ref, rank_ref, cnt_ref, carry_ref, *, tr):
    i = pl.program_id(0)

    @pl.when(i == 0)
    def _():
        carry_ref[...] = jnp.zeros_like(carry_ref)

    x = x_ref[...]
    xh = x.astype(jnp.bfloat16)
    xl = (x - xh.astype(jnp.float32)).astype(jnp.bfloat16)
    nt = (((1,), (1,)), ((), ()))
    wh = wh_ref[...]
    logits = (lax.dot_general(wl_ref[...], xh, nt, preferred_element_type=jnp.float32)
              + lax.dot_general(wh, xl, nt, preferred_element_type=jnp.float32)
              + lax.dot_general(wh, xh, nt, preferred_element_type=jnp.float32))
    scores = _sigmoid(logits)
    sel = scores + rb_ref[...]

    neg = -jnp.inf
    e_idx = lax.broadcasted_iota(jnp.int32, (N_EXPERTS, tr), 0)
    g_idx = lax.broadcasted_iota(jnp.int32, (N_GROUPS, tr), 0)
    in_idx = lax.broadcasted_iota(jnp.int32, (GROUP_SIZE, tr), 0)

    gs_rows = []
    for g in range(N_GROUPS):
        blk = sel[g * GROUP_SIZE:(g + 1) * GROUP_SIZE, :]
        m1, a1 = _first_argmax_rows(blk, in_idx)
        m2 = jnp.max(jnp.where(in_idx == a1, neg, blk), axis=0, keepdims=True)
        gs_rows.append(m1 + m2)
    gs = jnp.concatenate(gs_rows, axis=0)

    gsel = jnp.zeros((N_GROUPS, tr), jnp.float32)
    for _ in range(TOPK_GROUPS):
        _, a = _first_argmax_rows(gs, g_idx)
        pick = g_idx == a
        gsel = jnp.where(pick, 1.0, gsel)
        gs = jnp.where(pick, neg, gs)
    esel = jnp.concatenate(
        [jnp.broadcast_to(gsel[g:g + 1, :], (GROUP_SIZE, tr)) for g in range(N_GROUPS)], axis=0)
    cand = jnp.where(esel > 0.5, sel, neg)

    picks, gates = [], []
    chosen = jnp.zeros((N_EXPERTS, tr), jnp.float32)
    for _ in range(TOP_K):
        _, a = _first_argmax_rows(cand, e_idx)
        pick = e_idx == a
        picks.append(a)
        gates.append(jnp.sum(jnp.where(pick, scores, 0.0), axis=0, keepdims=True))
        chosen = jnp.where(pick, 1.0, chosen)
        cand = jnp.where(pick, neg, cand)
    gate = jnp.concatenate(gates, axis=0)
    gate = gate / jnp.sum(gate, axis=0, keepdims=True) * ROUTE_SCALE

    r_idx = lax.broadcasted_iota(jnp.int32, (tr, tr), 0)
    c_idx = lax.broadcasted_iota(jnp.int32, (tr, tr), 1)
    upper = jnp.where(r_idx <= c_idx, 1.0, 0.0).astype(jnp.bfloat16)
    incl = jnp.dot(chosen.astype(jnp.bfloat16), upper, preferred_element_type=jnp.float32)
    rank_all = carry_ref[...] + incl - chosen
    carry_ref[...] = carry_ref[...] + incl[:, tr - 1:tr]
    ranks = [jnp.sum(jnp.where(e_idx == a, rank_all, 0.0), axis=0, keepdims=True) for a in picks]

    eidx_ref[...] = jnp.concatenate(picks, axis=0)
    rank_ref[...] = jnp.concatenate(ranks, axis=0).astype(jnp.int32)
    pad = jnp.zeros((LANES - TOP_K, tr), jnp.float32)
    gate_t_ref[...] = jnp.concatenate([gate, pad], axis=0).T
    cnt_ref[...] = jnp.broadcast_to(carry_ref[...], (N_EXPERTS, LANES)).astype(jnp.int32)


def _router(x1, w_r, r_bias, tr):
    n_tok, d = x1.shape
    wt = w_r.T
    wh = wt.astype(jnp.bfloat16)
    wl = (wt - wh.astype(jnp.float32)).astype(jnp.bfloat16)
    col = lambda i: (0, i)
    return pl.pallas_call(
        functools.partial(_router_kernel, tr=tr),
        grid=(n_tok // tr,),
        in_specs=[
            pl.BlockSpec((tr, d), lambda i: (i, 0)),
            _const_spec((N_EXPERTS, d)),
            _const_spec((N_EXPERTS, d)),
            _const_spec((N_EXPERTS, 1)),
        ],
        out_specs=[
            pl.BlockSpec((TOP_K, tr), col),
            pl.BlockSpec((tr, LANES), lambda i: (i, 0)),
            pl.BlockSpec((TOP_K, tr), col),
            pl.BlockSpec((N_EXPERTS, LANES), lambda i: (0, 0)),
        ],
        out_shape=(
            jax.ShapeDtypeStruct((TOP_K, n_tok), jnp.int32),
            jax.ShapeDtypeStruct((n_tok, LANES), jnp.float32),
            jax.ShapeDtypeStruct((TOP_K, n_tok), jnp.int32),
            jax.ShapeDtypeStruct((N_EXPERTS, LANES), jnp.int32),
        ),
        scratch_shapes=[pltpu.VMEM((N_EXPERTS, 1), jnp.float32)],
        compiler_params=_params(("arbitrary",), 32 << 20),
        name="router",
    )(x1, wh, wl, r_bias.reshape(N_EXPERTS, 1))


def _dest_kernel(pstart_ref, eidx_ref, rank_ref, dest_ref):
    e = eidx_ref[...]
    base = jnp.zeros(e.shape, jnp.int32)
    for k in range(N_EXPERTS):
        base = jnp.where(e == k, pstart_ref[k], base)
    dest_ref[...] = base + rank_ref[...]


def _dest_rows(pstart, eidx, rank, tc):
    n_tok = eidx.shape[1]
    col = lambda i, ps: (0, i)
    return pl.pallas_call(
        _dest_kernel,
        grid_spec=pltpu.PrefetchScalarGridSpec(
            num_scalar_prefetch=1,
            grid=(n_tok // tc,),
            in_specs=[pl.BlockSpec((TOP_K, tc), col), pl.BlockSpec((TOP_K, tc), col)],
            out_specs=pl.BlockSpec((TOP_K, tc), col),
        ),
        out_shape=jax.ShapeDtypeStruct((TOP_K, n_tok), jnp.int32),
        compiler_params=_params(("arbitrary",), 32 << 20),
        name="dest_rows",
    )(pstart, eidx, rank)


def _dispatch_kernel(pend_ref, dest_ref, x_ref, xs_ref, zero_ref, sem, *, td, bm):
    i = pl.program_id(0)

    @pl.when(i == 0)
    def _():
        zero_ref[...] = jnp.zeros_like(zero_ref)
        for e in range(N_EXPERTS):
            prev = pend_ref[e - 1] if e > 0 else 0

            @pl.when(pend_ref[e] > prev)
            def _():
                start = pl.multiple_of(pend_ref[e] - bm, bm)
                pltpu.make_async_copy(zero_ref, xs_ref.at[pl.ds(start, bm), :], sem).start()
        first_unused = pend_ref[N_EXPERTS - 1] // bm
        n_blk = xs_ref.shape[0] // bm

        def fill(b, carry):
            pltpu.make_async_copy(zero_ref, xs_ref.at[pl.ds(pl.multiple_of(b * bm, bm), bm), :], sem).start()
            return carry

        def fill_done(b, carry):
            pltpu.make_async_copy(zero_ref, xs_ref.at[pl.ds(0, bm), :], sem).wait()
            return carry

        lax.fori_loop(first_unused, n_blk, fill, 0)
        lax.fori_loop(first_unused, n_blk, fill_done, 0)
        for e in range(N_EXPERTS):
            prev = pend_ref[e - 1] if e > 0 else 0

            @pl.when(pend_ref[e] > prev)
            def _():
                pltpu.make_async_copy(zero_ref, xs_ref.at[pl.ds(0, bm), :], sem).wait()

    def issue(r, carry):
        for j in range(TOP_K):
            d = dest_ref[j, r]
            pltpu.make_async_copy(x_ref.at[pl.ds(r, 1), :], xs_ref.at[pl.ds(d, 1), :], sem).start()
        return carry

    lax.fori_loop(0, td, issue, 0)
    for j in range(TOP_K):
        pltpu.make_async_copy(x_ref, xs_ref.at[pl.ds(0, td), :], sem).wait()


def _dispatch(pend, dest, x1p, n_rows, td, bm):
    n_tok, width = x1p.shape
    return pl.pallas_call(
        functools.partial(_dispatch_kernel, td=td, bm=bm),
        grid_spec=pltpu.PrefetchScalarGridSpec(
            num_scalar_prefetch=1,
            grid=(n_tok // td,),
            in_specs=[
                pl.BlockSpec((TOP_K, td), lambda i, pe: (0, i), memory_space=pltpu.SMEM),
                pl.BlockSpec((td, width), lambda i, pe: (i, 0)),
            ],
            out_specs=pl.BlockSpec(memory_space=pl.ANY),
            scratch_shapes=[pltpu.VMEM((bm, width), jnp.uint32), pltpu.SemaphoreType.DMA],
        ),
        out_shape=jax.ShapeDtypeStruct((n_rows, width), jnp.uint32),
        compiler_params=_params(("arbitrary",), 32 << 20),
        name="dispatch",
    )(pend, dest, x1p)


def _cast_kernel(w_ref, o_ref):
    o_ref[...] = w_ref[...].astype(o_ref.dtype)


def _to_bf16(w, layer):
    _, n_e, a, b = w.shape
    return pl.pallas_call(
        _cast_kernel,
        grid=(n_e,),
        in_specs=[pl.BlockSpec((None, 1, a, b), lambda e: (layer, e, 0, 0))],
        out_specs=pl.BlockSpec((1, a, b), lambda e: (e, 0, 0)),
        out_shape=jax.ShapeDtypeStruct((n_e, a, b), jnp.bfloat16),
        compiler_params=_params(("arbitrary",), 32 << 20),
        name="expert_weights_bf16",
    )(w)


def _experts_kernel(be_ref, nv_ref, xs_ref, w1_ref, w3_ref, w2_ref, ys_ref, xb_ref):
    b = pl.program_id(0)

    @pl.when(b < nv_ref[0])
    def _():
        half = xs_ref.shape[1]
        lo, hi = _unpack_bf16_pairs(xs_ref[...])
        xb_ref[:, :half] = lo
        xb_ref[:, half:] = hi
        xb = xb_ref[...]
        h1 = jnp.dot(xb, w1_ref[0], preferred_element_type=jnp.float32)
        h3 = jnp.dot(xb, w3_ref[0], preferred_element_type=jnp.float32)
        h = (_silu(h1) * h3).astype(jnp.bfloat16)
        ys_ref[...] = _pack_bf16_pairs(jnp.dot(h, w2_ref[0], preferred_element_type=jnp.float32))

    @pl.when(b >= nv_ref[0])
    def _():
        ys_ref[...] = jnp.zeros_like(ys_ref)


def _experts(blk_exp, n_valid, xs, w1, w3, w2, bm):
    n_rows, half = xs.shape
    d = 2 * half
    de = w1.shape[2]
    n_blk = n_rows // bm
    blk = lambda b, be, nv: (jnp.minimum(b, nv[0] - 1), 0)
    wsel = lambda b, be, nv: (be[jnp.minimum(b, nv[0] - 1)], 0, 0)
    vmem = 2 * bm * half * 4 + bm * d * 2 + 2 * 3 * d * de * 2 + 2 * bm * half * 4 + 4 * bm * de * 4 + 2 * bm * d * 4
    return pl.pallas_call(
        _experts_kernel,
        grid_spec=pltpu.PrefetchScalarGridSpec(
            num_scalar_prefetch=2,
            grid=(n_blk,),
            in_specs=[
                pl.BlockSpec((bm, half), blk),
                pl.BlockSpec((1, d, de), wsel),
                pl.BlockSpec((1, d, de), wsel),
                pl.BlockSpec((1, de, d), wsel),
            ],
            out_specs=pl.BlockSpec((bm, half), lambda b, be, nv: (b, 0)),
            scratch_shapes=[pltpu.VMEM((bm, d), jnp.bfloat16)],
        ),
        out_shape=jax.ShapeDtypeStruct((n_rows, half), jnp.uint32),
        compiler_params=_params(("arbitrary",), vmem + (8 << 20)),
        name="experts",
    )(blk_exp, n_valid, xs, w1, w3, w2)


def _combine_kernel(dest_ref, dest_next_ref, x1_ref, p_ref, gt_ref, w1_ref, w3_ref, w2_ref, wpg_ref, bpg_ref, wpe_ref,
                    g_ref, b_ref, ys_ref, o_ref, ybuf_ref, sems, *, tm, alpha):
    i = pl.program_id(0)
    slot = lax.rem(i, 2)
    other = 1 - slot

    def row_copy(d_ref, s, j, r):
        return pltpu.make_async_copy(ys_ref.at[pl.ds(d_ref[j, r], 1), :], ybuf_ref.at[s, j, pl.ds(r, 1), :], sems.at[s])

    def drain(s):
        for j in range(TOP_K):
            pltpu.make_async_copy(ys_ref.at[pl.ds(0, tm), :], ybuf_ref.at[s, j], sems.at[s]).wait()

    @pl.when(i == 0)
    def _():
        def first_tile(r, carry):
            for j in range(TOP_K):
                row_copy(dest_ref, 0, j, r).start()
            return carry

        lax.fori_loop(0, tm, first_tile, 0)

    for r in range(tm):
        for j in range(TOP_K):
            row_copy(dest_next_ref, other, j, r).start()

    x1 = x1_ref[...]
    xb = x1.astype(jnp.bfloat16)
    h1 = jnp.dot(xb, w1_ref[...], preferred_element_type=jnp.float32)
    h3 = jnp.dot(xb, w3_ref[...], preferred_element_type=jnp.float32)
    h = (_silu(h1) * h3).astype(jnp.bfloat16)
    total = alpha * x1 + jnp.dot(h, w2_ref[...], preferred_element_type=jnp.float32)
    gate = _sigmoid(jnp.dot(xb, wpg_ref[...], preferred_element_type=jnp.float32) + bpg_ref[...])
    pe = jnp.dot(p_ref[...].astype(jnp.bfloat16), wpe_ref[...], preferred_element_type=jnp.float32)
    total = total + gate * pe

    drain(slot)
    gt = gt_ref[...]
    half = ybuf_ref.shape[3]
    moe_lo = jnp.zeros((tm, half), jnp.float32)
    moe_hi = jnp.zeros((tm, half), jnp.float32)
    for j in range(TOP_K):
        w = ybuf_ref[slot, j]
        g = gt[:, j:j + 1]
        moe_lo = moe_lo + g * lax.bitcast_convert_type(w << 16, jnp.float32)
        moe_hi = moe_hi + g * lax.bitcast_convert_type(w & jnp.uint32(0xFFFF0000), jnp.float32)
    total = total + jnp.concatenate([moe_lo, moe_hi], axis=1)
    o_ref[...] = _layer_norm_rows(total, g_ref[...], b_ref[...])

    @pl.when(i == pl.num_programs(0) - 1)
    def _():
        drain(other)


def _combine(dest, x1, p2, gate_t, w_sh1, w_sh3, w_sh2, w_pg, b_pg, w_pe, ln_g, ln_b, ys, alpha, tm):
    n_tok, d = x1.shape
    ds = w_sh1.shape[1]
    dp = p2.shape[1]
    row = lambda i: (i, 0)
    n_steps = n_tok // tm
    vmem = (2 * TOP_K * tm * (d // 2) * 4 + (3 * d * ds + d * d + dp * d) * 2 + 2 * 2 * tm * d * 4 + 2 * tm * dp * 4
            + 2 * tm * LANES * 4 + 6 * tm * d * 4)
    return pl.pallas_call(
        functools.partial(_combine_kernel, tm=tm, alpha=alpha),
        grid=(n_steps,),
        in_specs=[
            pl.BlockSpec((TOP_K, tm), lambda i: (0, i), memory_space=pltpu.SMEM),
            pl.BlockSpec((TOP_K, tm), lambda i: (0, jnp.minimum(i + 1, n_steps - 1)), memory_space=pltpu.SMEM),
            pl.BlockSpec((tm, d), row),
            pl.BlockSpec((tm, dp), row),
            pl.BlockSpec((tm, LANES), row),
            _const_spec((d, ds)),
            _const_spec((d, ds)),
            _const_spec((ds, d)),
            _const_spec((d, d)),
            _const_spec((1, d)),
            _const_spec((dp, d)),
            _const_spec((1, d)),
            _const_spec((1, d)),
            pl.BlockSpec(memory_space=pl.ANY),
        ],
        out_specs=pl.BlockSpec((tm, d), row),
        out_shape=jax.ShapeDtypeStruct((n_tok, d), jnp.float32),
        scratch_shapes=[pltpu.VMEM((2, TOP_K, tm, d // 2), jnp.uint32), pltpu.SemaphoreType.DMA((2,))],
        compiler_params=_params(("arbitrary",), vmem + (8 << 20)),
        name="combine",
    )(dest, dest, x1, p2, gate_t, w_sh1, w_sh3, w_sh2, w_pg, b_pg.reshape(1, d), w_pe, ln_g.reshape(1, d),
      ln_b.reshape(1, d), ys)


def _tile(n, target):
    t = min(n, target)
    assert n % t == 0, (n, t)
    return t


def kernel(x, p, w_in, b_f, conv_w, conv_b, conv_ln_g, conv_ln_b, gmlp_ln_g, gmlp_ln_b, w_sp, b_sp, out_g, w_o,
           ln1_g, ln1_b, w_r, r_bias, w_e1, w_e3, w_e2, w_sh1, w_sh3, w_sh2, w_pe, w_pg, b_pg, ln2_g, ln2_b):
    bsz, seq, d = x.shape
    depth = w_in.shape[0]
    n_tok = bsz * seq
    alpha = (2.0 * depth) ** 0.25
    n_main = 2 * CONV_CH + 2 * GMLP_CH + 3 * ATT_CH
    assert w_in.shape[2] == n_main + ATT_HEADS and seq % CHUNK == 0

    bm = MOE_BLOCK
    n_blk = -(-n_tok * TOP_K // bm) + N_EXPERTS
    n_rows = n_blk * bm
    bf16 = jnp.bfloat16

    x2 = x.reshape(n_tok, d)
    for i in range(depth):
        w_main = w_in[i, :, :n_main].astype(bf16)
        w_f = jnp.pad(w_in[i, :, n_main:], ((0, 0), (0, LANES - ATT_HEADS))).astype(bf16)
        b_f_row = jnp.pad(b_f[i], (0, LANES - ATT_HEADS)).reshape(1, LANES)

        zc, zg, q, k, v, f = _in_projection(x2, w_main, w_f, _tile(n_tok, 256))
        ya = _conv_module(zc, conv_w[i], conv_b[i], conv_ln_g[i], conv_ln_b[i], out_g[i, :CONV_CH],
                          bsz, seq, _tile(seq, 256))
        yb = _gmlp_module(zg, gmlp_ln_g[i], gmlp_ln_b[i], w_sp[i], b_sp[i],
                          out_g[i, CONV_CH:CONV_CH + GMLP_CH], _tile(n_tok, 512))
        c = _forget_cumsum(f, b_f_row, bsz, seq, _tile(seq, 512))
        tq = _tile(seq, 2048)
        yc = _fox_attention(q, k, v, c, bsz, seq, tq, _tile(tq, 512), _tile(tq, 512))
        x1, x1p = _out_projection(ya, yb, yc, x2, w_o[i].astype(bf16), out_g[i, CONV_CH + GMLP_CH:],
                                  ln1_g[i], ln1_b[i], alpha, _tile(n_tok, 256))

        eidx, gate_t, rank, cnt = _router(x1, w_r[i], r_bias[i], _tile(n_tok, 512))
        counts = cnt[:, 0]
        padded = (counts + bm - 1) // bm * bm
        pend = jnp.cumsum(padded).astype(jnp.int32)
        pstart = pend - padded
        blk_start = jnp.arange(n_blk, dtype=jnp.int32) * bm
        blk_exp = jnp.minimum(jnp.sum(pend[None, :] <= blk_start[:, None], axis=1), N_EXPERTS - 1).astype(jnp.int32)
        n_valid = (pend[N_EXPERTS - 1:] // bm).astype(jnp.int32)

        dest = _dest_rows(pstart, eidx, rank, _tile(n_tok, 4096))
        xs = _dispatch(pend, dest, x1p, n_rows, _tile(n_tok, 256), bm)
        ys = _experts(blk_exp, n_valid, xs, _to_bf16(w_e1, i), _to_bf16(w_e3, i), _to_bf16(w_e2, i), bm)
        x2 = _combine(dest, x1, p[i].reshape(n_tok, -1), gate_t, w_sh1[i].astype(bf16), w_sh3[i].astype(bf16),
                      w_sh2[i].astype(bf16), w_pg[i].astype(bf16), b_pg[i], w_pe[i].astype(bf16),
                      ln2_g[i], ln2_b[i], ys, alpha, _tile(n_tok, 256))
    return x2.reshape(bsz, seq, d)
```

```python
import functools

import jax
import jax.numpy as jnp
from jax import lax
from jax.experimental import pallas as pl
from jax.experimental.pallas import tpu as pltpu

CONV_CH = 512
CONV_WIDTH = 31
GMLP_HEADS = 4
GMLP_HEAD_CH = 128
GMLP_CH = GMLP_HEADS * GMLP_HEAD_CH
CHUNK = 128
ATT_HEADS = 8
HEAD_DIM = 128
ATT_CH = ATT_HEADS * HEAD_DIM
N_EXPERTS = 64
TOP_K = 8
N_GROUPS = 8
GROUP_SIZE = N_EXPERTS // N_GROUPS
TOPK_GROUPS = 4
ROUTE_SCALE = 2.5
LN_EPS = 1e-5
LOG2E = 1.4426950408889634

LANES = 128
SUBLANES = 8
VMEM_BYTES_V7X = 64 * 1024 * 1024

CONV_HALO = 32
CONV_ROWS = 32
MOE_BLOCK = 512


def _vmem_limit(nbytes):
    return int(min(nbytes, VMEM_BYTES_V7X - 8 * 1024 * 1024))


def _params(semantics, vmem_bytes):
    return pltpu.CompilerParams(dimension_semantics=semantics, vmem_limit_bytes=_vmem_limit(vmem_bytes))


def _const_spec(shape):
    nd = len(shape)
    return pl.BlockSpec(shape, lambda *_: (0,) * nd, pipeline_mode=pl.Buffered(1))


def _layer_norm_rows(x, g, b):
    mu = jnp.mean(x, axis=-1, keepdims=True)
    xc = x - mu
    var = jnp.mean(xc * xc, axis=-1, keepdims=True)
    return xc * lax.rsqrt(var + LN_EPS) * g + b


def _rms_rows(x):
    return x * lax.rsqrt(jnp.mean(x * x, axis=-1, keepdims=True) + LN_EPS)


def _sigmoid(x):
    return 1.0 / (1.0 + jnp.exp(-x))


def _silu(x):
    return x * _sigmoid(x)


def _inproj_kernel(x_ref, w_ref, wf_ref, zc_ref, zg_ref, q_ref, k_ref, v_ref, f_ref, *, q_scale):
    xb = x_ref[...].astype(jnp.bfloat16)
    wide = zc_ref.shape[1]

    def slab(n):
        return jnp.dot(xb, w_ref[:, n * wide:(n + 1) * wide], preferred_element_type=jnp.float32)

    f_ref[...] = jnp.dot(xb, wf_ref[...], preferred_element_type=jnp.float32)
    zc_ref[...] = slab(0)
    zg_ref[...] = slab(1)
    q_ref[...] = (slab(2) * q_scale).astype(jnp.bfloat16)
    k_ref[...] = slab(3).astype(jnp.bfloat16)
    v_ref[...] = slab(4).astype(jnp.bfloat16)


def _in_projection(x2, w_main, w_f, tm):
    n_tok, d = x2.shape
    wide = 2 * CONV_CH
    assert w_main.shape[1] == 5 * wide and ATT_CH == wide and 2 * GMLP_CH == wide
    row = lambda i: (i, 0)
    out_shape = (
        jax.ShapeDtypeStruct((n_tok, wide), jnp.float32),
        jax.ShapeDtypeStruct((n_tok, wide), jnp.float32),
        jax.ShapeDtypeStruct((n_tok, ATT_CH), jnp.bfloat16),
        jax.ShapeDtypeStruct((n_tok, ATT_CH), jnp.bfloat16),
        jax.ShapeDtypeStruct((n_tok, ATT_CH), jnp.bfloat16),
        jax.ShapeDtypeStruct((n_tok, LANES), jnp.float32),
    )
    vmem = (2 * tm * d * 4 + tm * d * 2 + d * 5 * wide * 2 + d * LANES * 2
            + 2 * 2 * tm * wide * 4 + 3 * 2 * tm * wide * 2 + 2 * tm * LANES * 4 + 2 * tm * wide * 4)
    return pl.pallas_call(
        functools.partial(_inproj_kernel, q_scale=HEAD_DIM ** -0.5 * LOG2E),
        grid=(n_tok // tm,),
        in_specs=[
            pl.BlockSpec((tm, d), row),
            _const_spec((d, 5 * wide)),
            _const_spec((d, LANES)),
        ],
        out_specs=[
            pl.BlockSpec((tm, wide), row),
            pl.BlockSpec((tm, wide), row),
            pl.BlockSpec((tm, ATT_CH), row),
            pl.BlockSpec((tm, ATT_CH), row),
            pl.BlockSpec((tm, ATT_CH), row),
            pl.BlockSpec((tm, LANES), row),
        ],
        out_shape=out_shape,
        compiler_params=_params(("arbitrary",), vmem + (8 << 20)),
        name="in_projection",
    )(x2, w_main, w_f)


def _conv_kernel(z_ref, w_ref, cb_ref, lng_ref, lnb_ref, og_ref, o_ref, hbuf_ref, sh_ref, *, ts):
    s = pl.program_id(1)
    span = CONV_HALO + ts - SUBLANES

    @pl.when(s == 0)
    def _():
        hbuf_ref[0:CONV_HALO, :] = jnp.zeros((CONV_HALO, CONV_CH), jnp.float32)

    @pl.when(s > 0)
    def _():
        hbuf_ref[0:CONV_HALO, :] = hbuf_ref[ts:ts + CONV_HALO, :]

    hbuf_ref[CONV_HALO:CONV_HALO + ts, :] = z_ref[:, :CONV_CH] * _sigmoid(z_ref[:, CONV_CH:])

    cb = cb_ref[...]
    lng = lng_ref[...]
    lnb = lnb_ref[...]
    og = og_ref[...]
    for r in range(1, SUBLANES):
        sh_ref[r - 1] = hbuf_ref[r:r + span, :]
    first = CONV_HALO - (CONV_WIDTH - 1)
    for c in range(ts // CONV_ROWS):
        acc = jnp.broadcast_to(cb, (CONV_ROWS, CONV_CH))
        for j in range(CONV_WIDTH):
            off = first + j + c * CONV_ROWS
            r = off % SUBLANES
            a = off - r
            rows = hbuf_ref[a:a + CONV_ROWS, :] if r == 0 else sh_ref[r - 1, a:a + CONV_ROWS, :]
            acc = acc + w_ref[j:j + 1, :] * rows
        y = _silu(_layer_norm_rows(acc, lng, lnb))
        o_ref[c * CONV_ROWS:(c + 1) * CONV_ROWS, :] = (_rms_rows(y) * og).astype(jnp.bfloat16)


def _conv_module(zc, conv_w, conv_b, ln_g, ln_b, og, bsz, seq, ts):
    n_tok = zc.shape[0]
    n_s = seq // ts
    vec = lambda v: v.reshape(1, CONV_CH)
    return pl.pallas_call(
        functools.partial(_conv_kernel, ts=ts),
        grid=(bsz, n_s),
        in_specs=[
            pl.BlockSpec((ts, 2 * CONV_CH), lambda b, s: (b * n_s + s, 0)),
            _const_spec((CONV_WIDTH, CONV_CH)),
            _const_spec((1, CONV_CH)),
            _const_spec((1, CONV_CH)),
            _const_spec((1, CONV_CH)),
            _const_spec((1, CONV_CH)),
        ],
        out_specs=pl.BlockSpec((ts, CONV_CH), lambda b, s: (b * n_s + s, 0)),
        out_shape=jax.ShapeDtypeStruct((n_tok, CONV_CH), jnp.bfloat16),
        scratch_shapes=[pltpu.VMEM((CONV_HALO + ts, CONV_CH), jnp.float32),
                        pltpu.VMEM((SUBLANES - 1, CONV_HALO + ts - SUBLANES, CONV_CH), jnp.float32)],
        compiler_params=_params(("arbitrary", "arbitrary"), 32 << 20),
        name="conv_module",
    )(zc, conv_w, vec(conv_b), vec(ln_g), vec(ln_b), vec(og))


def _gelu_tanh(x):
    c = (2.0 / jnp.pi) ** 0.5
    return 0.5 * x * (1.0 + jnp.tanh(c * (x + 0.044715 * (x * x * x))))


def _gmlp_kernel(z_ref, lng_ref, lnb_ref, wsp_ref, bsp_ref, og_ref, o_ref, y_ref, *, tg):
    z = _gelu_tanh(z_ref[...])
    u = z[:, :GMLP_CH]
    v = _layer_norm_rows(z[:, GMLP_CH:], lng_ref[...], lnb_ref[...]).astype(jnp.bfloat16)
    t_idx = lax.broadcasted_iota(jnp.int32, (CHUNK, CHUNK), 0)
    s_idx = lax.broadcasted_iota(jnp.int32, (CHUNK, CHUNK), 1)
    causal = s_idx <= t_idx
    for h in range(GMLP_HEADS):
        ws = jnp.where(causal, wsp_ref[h], 0.0).astype(jnp.bfloat16)
        bias = bsp_ref[h]
        cols = slice(h * GMLP_HEAD_CH, (h + 1) * GMLP_HEAD_CH)
        for c in range(tg // CHUNK):
            rows = slice(c * CHUNK, (c + 1) * CHUNK)
            mixed = jnp.dot(ws, v[rows, cols], preferred_element_type=jnp.float32) + bias
            y_ref[rows, cols] = u[rows, cols] * mixed
    o_ref[...] = (_rms_rows(y_ref[...]) * og_ref[...]).astype(jnp.bfloat16)


def _gmlp_module(zg, ln_g, ln_b, w_sp, b_sp, og, tg):
    n_tok = zg.shape[0]
    vec = lambda v: v.reshape(1, GMLP_CH)
    return pl.pallas_call(
        functools.partial(_gmlp_kernel, tg=tg),
        grid=(n_tok // tg,),
        in_specs=[
            pl.BlockSpec((tg, 2 * GMLP_CH), lambda i: (i, 0)),
            _const_spec((1, GMLP_CH)),
            _const_spec((1, GMLP_CH)),
            _const_spec((GMLP_HEADS, CHUNK, CHUNK)),
            _const_spec((GMLP_HEADS, CHUNK, 1)),
            _const_spec((1, GMLP_CH)),
        ],
        out_specs=pl.BlockSpec((tg, GMLP_CH), lambda i: (i, 0)),
        out_shape=jax.ShapeDtypeStruct((n_tok, GMLP_CH), jnp.bfloat16),
        scratch_shapes=[pltpu.VMEM((tg, GMLP_CH), jnp.float32)],
        compiler_params=_params(("arbitrary",), 32 << 20),
        name="gmlp_module",
    )(zg, vec(ln_g), vec(ln_b), w_sp, b_sp.reshape(GMLP_HEADS, CHUNK, 1), vec(og))


def _split3(x):
    hi = x.astype(jnp.bfloat16)
    r1 = x - hi.astype(jnp.float32)
    mid = r1.astype(jnp.bfloat16)
    lo = (r1 - mid.astype(jnp.float32)).astype(jnp.bfloat16)
    return hi, mid, lo


def _fcum_kernel(f_ref, bf_ref, c_ref, carry_ref, *, ts):
    s = pl.program_id(1)

    @pl.when(s == 0)
    def _():
        carry_ref[...] = jnp.zeros_like(carry_ref)

    x = f_ref[...] + bf_ref[...]
    ls = -(jnp.maximum(-x, 0.0) + jnp.log(1.0 + jnp.exp(-jnp.abs(x))))
    t_idx = lax.broadcasted_iota(jnp.int32, (ts, ts), 0)
    s_idx = lax.broadcasted_iota(jnp.int32, (ts, ts), 1)
    tri = jnp.where(s_idx <= t_idx, 1.0, 0.0).astype(jnp.bfloat16)
    hi, mid, lo = _split3(ls)
    c = (jnp.dot(tri, lo, preferred_element_type=jnp.float32)
         + jnp.dot(tri, mid, preferred_element_type=jnp.float32)
         + jnp.dot(tri, hi, preferred_element_type=jnp.float32)) + carry_ref[...]
    c_ref[...] = c * LOG2E
    carry_ref[...] = c[ts - 1:ts, :]


def _forget_cumsum(f, b_f_row, bsz, seq, ts):
    n_tok = f.shape[0]
    n_s = seq // ts
    return pl.pallas_call(
        functools.partial(_fcum_kernel, ts=ts),
        grid=(bsz, n_s),
        in_specs=[
            pl.BlockSpec((ts, LANES), lambda b, s: (b * n_s + s, 0)),
            _const_spec((1, LANES)),
        ],
        out_specs=pl.BlockSpec((ts, LANES), lambda b, s: (b * n_s + s, 0)),
        out_shape=jax.ShapeDtypeStruct((n_tok, LANES), jnp.float32),
        scratch_shapes=[pltpu.VMEM((1, LANES), jnp.float32)],
        compiler_params=_params(("arbitrary", "arbitrary"), 32 << 20),
        name="forget_cumsum",
    )(f, b_f_row)


def _bias_columns(c_tile, h, as_query):
    rows = c_tile.shape[0]
    lane = lax.broadcasted_iota(jnp.int32, (rows, LANES), 1)
    col = jnp.sum(jnp.where(lane == h, c_tile, 0.0), axis=1, keepdims=True)
    val = jnp.broadcast_to(col if as_query else -col, (rows, LANES))
    hi, mid, lo = (t.astype(jnp.float32) for t in _split3(val))
    base = 0 if as_query else 3
    out = jnp.where(lane < 6, 1.0, 0.0)
    for offset, term in enumerate((hi, mid, lo)):
        out = jnp.where(lane == base + offset, term, out)
    return out.astype(jnp.bfloat16)


def _fox_kernel(q_ref, k_ref, v_ref, cq_ref, ck_ref, o_ref, kaug_ref, vt_ref, qaug_ref, m_ref, l_ref, acc_ref, s_ref,
                *, tq, tk, cw, seq):
    h = pl.program_id(1)
    i = pl.program_id(2)
    n_chain = tq // cw

    @pl.when(i == 0)
    def _():
        for r in range(seq // tk):
            rows = slice(r * tk, (r + 1) * tk)
            kaug_ref[rows, :HEAD_DIM] = k_ref[rows, :]
            kaug_ref[rows, HEAD_DIM:] = _bias_columns(ck_ref[rows, :], h, as_query=False)
            vt_ref[:, rows] = v_ref[rows, :].astype(jnp.float32).T.astype(jnp.bfloat16)

    qaug_ref[:, :HEAD_DIM] = q_ref[...]
    qaug_ref[:, HEAD_DIM:] = _bias_columns(cq_ref[...], h, as_query=True)
    m_ref[...] = jnp.full(m_ref.shape, -jnp.inf, jnp.float32)
    l_ref[...] = jnp.zeros_like(l_ref)
    acc_ref[...] = jnp.zeros_like(acc_ref)

    def scores(slot, c, k0):
        s_ref[slot] = lax.dot_general(kaug_ref[pl.ds(k0, tk), :], qaug_ref[c * cw:(c + 1) * cw, :],
                                      (((1,), (1,)), ((), ())), preferred_element_type=jnp.float32)

    def absorb(slot, c, k0, key_minus_query):
        s = s_ref[slot]
        if key_minus_query is not None:
            key = lax.broadcasted_iota(jnp.int32, (tk, cw), 0) + key_minus_query
            qry = lax.broadcasted_iota(jnp.int32, (tk, cw), 1)
            s = jnp.where(key <= qry, s, -jnp.inf)
        m_prev = m_ref[c]
        m_new = jnp.maximum(m_prev, jnp.max(s, axis=0, keepdims=True))
        alpha = jnp.exp2(m_prev - m_new)
        p = jnp.exp2(s - m_new)
        l_ref[c] = alpha * l_ref[c] + jnp.sum(p, axis=0, keepdims=True)
        pv = jnp.dot(vt_ref[:, pl.ds(k0, tk)], p.astype(jnp.bfloat16), preferred_element_type=jnp.float32)
        acc_ref[c] = alpha * acc_ref[c] + pv
        m_ref[c] = m_new

    n_kb = tq // tk

    def query_tile_keys(base, masks):
        work = [(kb, c) for kb in range(n_kb) for c in range(n_chain) if masks[kb][c] is not False]
        for slot, (kb, c) in enumerate(work):
            scores(slot, c, pl.multiple_of(base + kb * tk, tk))
        for slot, (kb, c) in enumerate(work):
            absorb(slot, c, pl.multiple_of(base + kb * tk, tk), masks[kb][c])

    def body(j, carry):
        query_tile_keys(j * tq, [[None] * n_chain] * n_kb)
        return carry

    lax.fori_loop(0, i, body, 0)
    masks = []
    for kb in range(n_kb):
        row = []
        for c in range(n_chain):
            first_key, first_query = kb * tk, c * cw
            if first_key + tk - 1 <= first_query:
                row.append(None)
            elif first_key <= first_query + cw - 1:
                row.append(first_key - first_query)
            else:
                row.append(False)
        masks.append(row)
    query_tile_keys(i * tq, masks)
    for c in range(n_chain):
        o_ref[c * cw:(c + 1) * cw, :] = (acc_ref[c] / l_ref[c]).T.astype(jnp.bfloat16)


def _fox_attention(q, k, v, c, bsz, seq, tq, tk, cw):
    n_tok = q.shape[0]
    nq = seq // tq
    n_chain = tq // cw
    return pl.pallas_call(
        functools.partial(_fox_kernel, tq=tq, tk=tk, cw=cw, seq=seq),
        grid=(bsz, ATT_HEADS, nq),
        in_specs=[
            pl.BlockSpec((tq, HEAD_DIM), lambda b, h, i: (b * nq + i, h)),
            pl.BlockSpec((seq, HEAD_DIM), lambda b, h, i: (b, h)),
            pl.BlockSpec((seq, HEAD_DIM), lambda b, h, i: (b, h)),
            pl.BlockSpec((tq, LANES), lambda b, h, i: (b * nq + i, 0)),
            pl.BlockSpec((seq, LANES), lambda b, h, i: (b, 0)),
        ],
        out_specs=pl.BlockSpec((tq, HEAD_DIM), lambda b, h, i: (b * nq + i, h)),
        out_shape=jax.ShapeDtypeStruct((n_tok, ATT_CH), jnp.bfloat16),
        scratch_shapes=[
            pltpu.VMEM((seq, 2 * HEAD_DIM), jnp.bfloat16),
            pltpu.VMEM((HEAD_DIM, seq), jnp.bfloat16),
            pltpu.VMEM((tq, 2 * HEAD_DIM), jnp.bfloat16),
            pltpu.VMEM((n_chain, 1, cw), jnp.float32),
            pltpu.VMEM((n_chain, 1, cw), jnp.float32),
            pltpu.VMEM((n_chain, HEAD_DIM, cw), jnp.float32),
            pltpu.VMEM((n_chain * (tq // tk), tk, cw), jnp.float32),
        ],
        compiler_params=_params(("arbitrary", "arbitrary", "arbitrary"), 56 << 20),
        name="fox_attention",
    )(q, k, v, c, c)


def _pack_bf16_pairs(x):
    n = x.shape[1] // 2
    r = x.astype(jnp.bfloat16).astype(jnp.float32)
    lo = lax.bitcast_convert_type(r[:, :n], jnp.uint32)
    hi = lax.bitcast_convert_type(r[:, n:], jnp.uint32)
    return (lo >> 16) | (hi & jnp.uint32(0xFFFF0000))


def _unpack_bf16_pairs(w):
    lo = lax.bitcast_convert_type(w << 16, jnp.float32).astype(jnp.bfloat16)
    hi = lax.bitcast_convert_type(w & jnp.uint32(0xFFFF0000), jnp.float32).astype(jnp.bfloat16)
    return lo, hi


def _outproj_kernel(ya_ref, yb_ref, yc_ref, x_ref, wo_ref, ogc_ref, g_ref, b_ref, x1_ref, x1p_ref, y_ref, *, alpha):
    y_ref[:, 0:CONV_CH] = ya_ref[...]
    y_ref[:, CONV_CH:CONV_CH + GMLP_CH] = yb_ref[...]
    yc = yc_ref[...].astype(jnp.float32)
    y_ref[:, CONV_CH + GMLP_CH:] = (_rms_rows(yc) * ogc_ref[...]).astype(jnp.bfloat16)
    h = jnp.dot(y_ref[...], wo_ref[...], preferred_element_type=jnp.float32)
    x1 = _layer_norm_rows(alpha * x_ref[...] + h, g_ref[...], b_ref[...])
    x1_ref[...] = x1
    x1p_ref[...] = _pack_bf16_pairs(x1)


def _out_projection(ya, yb, yc, x2, w_o, og_c, ln_g, ln_b, alpha, tm):
    n_tok, d = x2.shape
    mix = w_o.shape[0]
    row = lambda i: (i, 0)
    vmem = (mix * d * 2 + 2 * tm * d * 4 * 2 + 2 * tm * (d // 2) * 4 + 2 * tm * mix * 2 + tm * mix * 2 + 3 * tm * d * 4)
    return pl.pallas_call(
        functools.partial(_outproj_kernel, alpha=alpha),
        grid=(n_tok // tm,),
        in_specs=[
            pl.BlockSpec((tm, CONV_CH), row),
            pl.BlockSpec((tm, GMLP_CH), row),
            pl.BlockSpec((tm, ATT_CH), row),
            pl.BlockSpec((tm, d), row),
            _const_spec((mix, d)),
            _const_spec((1, ATT_CH)),
            _const_spec((1, d)),
            _const_spec((1, d)),
        ],
        out_specs=[pl.BlockSpec((tm, d), row), pl.BlockSpec((tm, d // 2), row)],
        out_shape=(
            jax.ShapeDtypeStruct((n_tok, d), jnp.float32),
            jax.ShapeDtypeStruct((n_tok, d // 2), jnp.uint32),
        ),
        scratch_shapes=[pltpu.VMEM((tm, mix), jnp.bfloat16)],
        compiler_params=_params(("arbitrary",), vmem + (8 << 20)),
        name="out_projection",
    )(ya, yb, yc, x2, w_o, og_c.reshape(1, ATT_CH), ln_g.reshape(1, d), ln_b.reshape(1, d))


def _first_argmax_rows(x, idx):
    m = jnp.max(x, axis=0, keepdims=True)
    first = jnp.min(jnp.where(x == m, idx, x.shape[0]), axis=0, keepdims=True)
    return m, first


def _router_kernel(x_ref, wh_ref, wl_ref, rb_ref, eidx_ref, gate_t_ref, rank_ref, cnt_ref, carry_ref, *, tr):
    i = pl.program_id(0)

    @pl.when(i == 0)
    def _():
        carry_ref[...] = jnp.zeros_like(carry_ref)

    x = x_ref[...]
    xh = x.astype(jnp.bfloat16)
    xl = (x - xh.astype(jnp.float32)).astype(jnp.bfloat16)
    nt = (((1,), (1,)), ((), ()))
    wh = wh_ref[...]
    logits = (lax.dot_general(wl_ref[...], xh, nt, preferred_element_type=jnp.float32)
              + lax.dot_general(wh, xl, nt, preferred_element_type=jnp.float32)
              + lax.dot_general(wh, xh, nt, preferred_element_type=jnp.float32))
    scores = _sigmoid(logits)
    sel = scores + rb_ref[...]

    neg = -jnp.inf
    e_idx = lax.broadcasted_iota(jnp.int32, (N_EXPERTS, tr), 0)
    g_idx = lax.broadcasted_iota(jnp.int32, (N_GROUPS, tr), 0)
    in_idx = lax.broadcasted_iota(jnp.int32, (GROUP_SIZE, tr), 0)

    gs_rows = []
    for g in range(N_GROUPS):
        blk = sel[g * GROUP_SIZE:(g + 1) * GROUP_SIZE, :]
        m1, a1 = _first_argmax_rows(blk, in_idx)
        m2 = jnp.max(jnp.where(in_idx == a1, neg, blk), axis=0, keepdims=True)
        gs_rows.append(m1 + m2)
    gs = jnp.concatenate(gs_rows, axis=0)

    gsel = jnp.zeros((N_GROUPS, tr), jnp.float32)
    for _ in range(TOPK_GROUPS):
        _, a = _first_argmax_rows(gs, g_idx)
        pick = g_idx == a
        gsel = jnp.where(pick, 1.0, gsel)
        gs = jnp.where(pick, neg, gs)
    esel = jnp.concatenate(
        [jnp.broadcast_to(gsel[g:g + 1, :], (GROUP_SIZE, tr)) for g in range(N_GROUPS)], axis=0)
    cand = jnp.where(esel > 0.5, sel, neg)

    picks, gates = [], []
    chosen = jnp.zeros((N_EXPERTS, tr), jnp.float32)
    for _ in range(TOP_K):
        _, a = _first_argmax_rows(cand, e_idx)
        pick = e_idx == a
        picks.append(a)
        gates.append(jnp.sum(jnp.where(pick, scores, 0.0), axis=0, keepdims=True))
        chosen = jnp.where(pick, 1.0, chosen)
        cand = jnp.where(pick, neg, cand)
    gate = jnp.concatenate(gates, axis=0)
    gate = gate / jnp.sum(gate, axis=0, keepdims=True) * ROUTE_SCALE

    r_idx = lax.broadcasted_iota(jnp.int32, (tr, tr), 0)
    c_idx = lax.broadcasted_iota(jnp.int32, (tr, tr), 1)
    upper = jnp.where(r_idx <= c_idx, 1.0, 0.0).astype(jnp.bfloat16)
    incl = jnp.dot(chosen.astype(jnp.bfloat16), upper, preferred_element_type=jnp.float32)
    rank_all = carry_ref[...] + incl - chosen
    carry_ref[...] = carry_ref[...] + incl[:, tr - 1:tr]
    ranks = [jnp.sum(jnp.where(e_idx == a, rank_all, 0.0), axis=0, keepdims=True) for a in picks]

    eidx_ref[...] = jnp.concatenate(picks, axis=0)
    rank_ref[...] = jnp.concatenate(ranks, axis=0).astype(jnp.int32)
    pad = jnp.zeros((LANES - TOP_K, tr), jnp.float32)
    gate_t_ref[...] = jnp.concatenate([gate, pad], axis=0).T
    cnt_ref[...] = jnp.broadcast_to(carry_ref[...], (N_EXPERTS, LANES)).astype(jnp.int32)


def _router(x1, w_r, r_bias, tr):
    n_tok, d = x1.shape
    wt = w_r.T
    wh = wt.astype(jnp.bfloat16)
    wl = (wt - wh.astype(jnp.float32)).astype(jnp.bfloat16)
    col = lambda i: (0, i)
    return pl.pallas_call(
        functools.partial(_router_kernel, tr=tr),
        grid=(n_tok // tr,),
        in_specs=[
            pl.BlockSpec((tr, d), lambda i: (i, 0)),
            _const_spec((N_EXPERTS, d)),
            _const_spec((N_EXPERTS, d)),
            _const_spec((N_EXPERTS, 1)),
        ],
        out_specs=[
            pl.BlockSpec((TOP_K, tr), col),
            pl.BlockSpec((tr, LANES), lambda i: (i, 0)),
            pl.BlockSpec((TOP_K, tr), col),
            pl.BlockSpec((N_EXPERTS, LANES), lambda i: (0, 0)),
        ],
        out_shape=(
            jax.ShapeDtypeStruct((TOP_K, n_tok), jnp.int32),
            jax.ShapeDtypeStruct((n_tok, LANES), jnp.float32),
            jax.ShapeDtypeStruct((TOP_K, n_tok), jnp.int32),
            jax.ShapeDtypeStruct((N_EXPERTS, LANES), jnp.int32),
        ),
        scratch_shapes=[pltpu.VMEM((N_EXPERTS, 1), jnp.float32)],
        compiler_params=_params(("arbitrary",), 32 << 20),
        name="router",
    )(x1, wh, wl, r_bias.reshape(N_EXPERTS, 1))


def _dest_kernel(pstart_ref, eidx_ref, rank_ref, dest_ref):
    e = eidx_ref[...]
    base = jnp.zeros(e.shape, jnp.int32)
    for k in range(N_EXPERTS):
        base = jnp.where(e == k, pstart_ref[k], base)
    dest_ref[...] = base + rank_ref[...]


def _dest_rows(pstart, eidx, rank, tc):
    n_tok = eidx.shape[1]
    col = lambda i, ps: (0, i)
    return pl.pallas_call(
        _dest_kernel,
        grid_spec=pltpu.PrefetchScalarGridSpec(
            num_scalar_prefetch=1,
            grid=(n_tok // tc,),
            in_specs=[pl.BlockSpec((TOP_K, tc), col), pl.BlockSpec((TOP_K, tc), col)],
            out_specs=pl.BlockSpec((TOP_K, tc), col),
        ),
        out_shape=jax.ShapeDtypeStruct((TOP_K, n_tok), jnp.int32),
        compiler_params=_params(("arbitrary",), 32 << 20),
        name="dest_rows",
    )(pstart, eidx, rank)


def _dispatch_kernel(pend_ref, dest_ref, x_ref, xs_ref, zero_ref, sem, *, td, bm):
    i = pl.program_id(0)

    @pl.when(i == 0)
    def _():
        zero_ref[...] = jnp.zeros_like(zero_ref)
        for e in range(N_EXPERTS):
            prev = pend_ref[e - 1] if e > 0 else 0

            @pl.when(pend_ref[e] > prev)
            def _():
                start = pl.multiple_of(pend_ref[e] - bm, bm)
                pltpu.make_async_copy(zero_ref, xs_ref.at[pl.ds(start, bm), :], sem).start()
        first_unused = pend_ref[N_EXPERTS - 1] // bm
        n_blk = xs_ref.shape[0] // bm

        def fill(b, carry):
            pltpu.make_async_copy(zero_ref, xs_ref.at[pl.ds(pl.multiple_of(b * bm, bm), bm), :], sem).start()
            return carry

        def fill_done(b, carry):
            pltpu.make_async_copy(zero_ref, xs_ref.at[pl.ds(0, bm), :], sem).wait()
            return carry

        lax.fori_loop(first_unused, n_blk, fill, 0)
        lax.fori_loop(first_unused, n_blk, fill_done, 0)
        for e in range(N_EXPERTS):
            prev = pend_ref[e - 1] if e > 0 else 0

            @pl.when(pend_ref[e] > prev)
            def _():
                pltpu.make_async_copy(zero_ref, xs_ref.at[pl.ds(0, bm), :], sem).wait()

    for r in range(td):
        for j in range(TOP_K):
            d = dest_ref[j, r]
            pltpu.make_async_copy(x_ref.at[pl.ds(r, 1), :], xs_ref.at[pl.ds(d, 1), :], sem).start()
    for j in range(TOP_K):
        pltpu.make_async_copy(x_ref, xs_ref.at[pl.ds(0, td), :], sem).wait()


def _dispatch(pend, dest, x1p, n_rows, td, bm):
    n_tok, width = x1p.shape
    return pl.pallas_call(
        functools.partial(_dispatch_kernel, td=td, bm=bm),
        grid_spec=pltpu.PrefetchScalarGridSpec(
            num_scalar_prefetch=1,
            grid=(n_tok // td,),
            in_specs=[
                pl.BlockSpec((TOP_K, td), lambda i, pe: (0, i), memory_space=pltpu.SMEM),
                pl.BlockSpec((td, width), lambda i, pe: (i, 0)),
            ],
            out_specs=pl.BlockSpec(memory_space=pl.ANY),
            scratch_shapes=[pltpu.VMEM((bm, width), jnp.uint32), pltpu.SemaphoreType.DMA],
        ),
        out_shape=jax.ShapeDtypeStruct((n_rows, width), jnp.uint32),
        compiler_params=_params(("arbitrary",), 32 << 20),
        name="dispatch",
    )(pend, dest, x1p)


def _experts_kernel(be_ref, nv_ref, xs_ref, w1_ref, w3_ref, w2_ref, ys_ref, xb_ref, w1b_ref, w3b_ref, w2b_ref):
    b = pl.program_id(0)

    @pl.when(b < nv_ref[0])
    def _():
        @pl.when((b == 0) | (be_ref[b] != be_ref[jnp.maximum(b - 1, 0)]))
        def _():
            w1b_ref[...] = w1_ref[0].astype(jnp.bfloat16)
            w3b_ref[...] = w3_ref[0].astype(jnp.bfloat16)
            w2b_ref[...] = w2_ref[0].astype(jnp.bfloat16)

        half = xs_ref.shape[1]
        lo, hi = _unpack_bf16_pairs(xs_ref[...])
        xb_ref[:, :half] = lo
        xb_ref[:, half:] = hi
        xb = xb_ref[...]
        h1 = jnp.dot(xb, w1b_ref[...], preferred_element_type=jnp.float32)
        h3 = jnp.dot(xb, w3b_ref[...], preferred_element_type=jnp.float32)
        h = (_silu(h1) * h3).astype(jnp.bfloat16)
        ys_ref[...] = _pack_bf16_pairs(jnp.dot(h, w2b_ref[...], preferred_element_type=jnp.float32))

    @pl.when(b >= nv_ref[0])
    def _():
        ys_ref[...] = jnp.zeros_like(ys_ref)


def _experts(blk_exp, n_valid, xs, w1, w3, w2, layer, bm):
    n_rows, half = xs.shape
    d = 2 * half
    de = w1.shape[3]
    n_blk = n_rows // bm
    blk = lambda b, be, nv: (jnp.minimum(b, nv[0] - 1), 0)
    wsel = lambda b, be, nv: (layer, be[jnp.minimum(b, nv[0] - 1)], 0, 0)
    vmem = (2 * bm * half * 4 + bm * d * 2 + 2 * 3 * d * de * 4 + 3 * d * de * 2 + 2 * bm * half * 4
            + 4 * bm * de * 4 + 2 * bm * d * 4)
    return pl.pallas_call(
        _experts_kernel,
        grid_spec=pltpu.PrefetchScalarGridSpec(
            num_scalar_prefetch=2,
            grid=(n_blk,),
            in_specs=[
                pl.BlockSpec((bm, half), blk),
                pl.BlockSpec((None, 1, d, de), wsel),
                pl.BlockSpec((None, 1, d, de), wsel),
                pl.BlockSpec((None, 1, de, d), wsel),
            ],
            out_specs=pl.BlockSpec((bm, half), lambda b, be, nv: (b, 0)),
            scratch_shapes=[pltpu.VMEM((bm, d), jnp.bfloat16), pltpu.VMEM((d, de), jnp.bfloat16),
                            pltpu.VMEM((d, de), jnp.bfloat16), pltpu.VMEM((de, d), jnp.bfloat16)],
        ),
        out_shape=jax.ShapeDtypeStruct((n_rows, half), jnp.uint32),
        compiler_params=_params(("arbitrary",), vmem + (8 << 20)),
        name="experts",
    )(blk_exp, n_valid, xs, w1, w3, w2)


def _combine_kernel(dest_ref, dest_next_ref, x1_ref, p_ref, gt_ref, w1_ref, w3_ref, w2_ref, wpg_ref, bpg_ref, wpe_ref,
                    g_ref, b_ref, ys_ref, o_ref, ybuf_ref, sems, *, tm, alpha):
    i = pl.program_id(0)
    slot = lax.rem(i, 2)
    other = 1 - slot

    def row_copy(d_ref, s, j, r):
        return pltpu.make_async_copy(ys_ref.at[pl.ds(d_ref[j, r], 1), :], ybuf_ref.at[s, j, pl.ds(r, 1), :], sems.at[s])

    def drain(s):
        for j in range(TOP_K):
            pltpu.make_async_copy(ys_ref.at[pl.ds(0, tm), :], ybuf_ref.at[s, j], sems.at[s]).wait()

    @pl.when(i == 0)
    def _():
        def first_tile(r, carry):
            for j in range(TOP_K):
                row_copy(dest_ref, 0, j, r).start()
            return carry

        lax.fori_loop(0, tm, first_tile, 0)

    for r in range(tm):
        for j in range(TOP_K):
            row_copy(dest_next_ref, other, j, r).start()

    x1 = x1_ref[...]
    xb = x1.astype(jnp.bfloat16)
    h1 = jnp.dot(xb, w1_ref[...], preferred_element_type=jnp.float32)
    h3 = jnp.dot(xb, w3_ref[...], preferred_element_type=jnp.float32)
    h = (_silu(h1) * h3).astype(jnp.bfloat16)
    total = alpha * x1 + jnp.dot(h, w2_ref[...], preferred_element_type=jnp.float32)
    gate = _sigmoid(jnp.dot(xb, wpg_ref[...], preferred_element_type=jnp.float32) + bpg_ref[...])
    pe = jnp.dot(p_ref[...].astype(jnp.bfloat16), wpe_ref[...], preferred_element_type=jnp.float32)
    total = total + gate * pe

    drain(slot)
    gt = gt_ref[...]
    half = ybuf_ref.shape[3]
    moe_lo = jnp.zeros((tm, half), jnp.float32)
    moe_hi = jnp.zeros((tm, half), jnp.float32)
    for j in range(TOP_K):
        w = ybuf_ref[slot, j]
        g = gt[:, j:j + 1]
        moe_lo = moe_lo + g * lax.bitcast_convert_type(w << 16, jnp.float32)
        moe_hi = moe_hi + g * lax.bitcast_convert_type(w & jnp.uint32(0xFFFF0000), jnp.float32)
    total = total + jnp.concatenate([moe_lo, moe_hi], axis=1)
    o_ref[...] = _layer_norm_rows(total, g_ref[...], b_ref[...])

    @pl.when(i == pl.num_programs(0) - 1)
    def _():
        drain(other)


def _combine(dest, x1, p2, gate_t, w_sh1, w_sh3, w_sh2, w_pg, b_pg, w_pe, ln_g, ln_b, ys, alpha, tm):
    n_tok, d = x1.shape
    ds = w_sh1.shape[1]
    dp = p2.shape[1]
    row = lambda i: (i, 0)
    n_steps = n_tok // tm
    vmem = (2 * TOP_K * tm * (d // 2) * 4 + (3 * d * ds + d * d + dp * d) * 2 + 2 * 2 * tm * d * 4 + 2 * tm * dp * 4
            + 2 * tm * LANES * 4 + 6 * tm * d * 4)
    return pl.pallas_call(
        functools.partial(_combine_kernel, tm=tm, alpha=alpha),
        grid=(n_steps,),
        in_specs=[
            pl.BlockSpec((TOP_K, tm), lambda i: (0, i), memory_space=pltpu.SMEM),
            pl.BlockSpec((TOP_K, tm), lambda i: (0, jnp.minimum(i + 1, n_steps - 1)), memory_space=pltpu.SMEM),
            pl.BlockSpec((tm, d), row),
            pl.BlockSpec((tm, dp), row),
            pl.BlockSpec((tm, LANES), row),
            _const_spec((d, ds)),
            _const_spec((d, ds)),
            _const_spec((ds, d)),
            _const_spec((d, d)),
            _const_spec((1, d)),
            _const_spec((dp, d)),
            _const_spec((1, d)),
            _const_spec((1, d)),
            pl.BlockSpec(memory_space=pl.ANY),
        ],
        out_specs=pl.BlockSpec((tm, d), row),
        out_shape=jax.ShapeDtypeStruct((n_tok, d), jnp.float32),
        scratch_shapes=[pltpu.VMEM((2, TOP_K, tm, d // 2), jnp.uint32), pltpu.SemaphoreType.DMA((2,))],
        compiler_params=_params(("arbitrary",), vmem + (8 << 20)),
        name="combine",
    )(dest, dest, x1, p2, gate_t, w_sh1, w_sh3, w_sh2, w_pg, b_pg.reshape(1, d), w_pe, ln_g.reshape(1, d),
      ln_b.reshape(1, d), ys)


def _tile(n, target):
    t = min(n, target)
    assert n % t == 0, (n, t)
    return t


def kernel(x, p, w_in, b_f, conv_w, conv_b, conv_ln_g, conv_ln_b, gmlp_ln_g, gmlp_ln_b, w_sp, b_sp, out_g, w_o,
           ln1_g, ln1_b, w_r, r_bias, w_e1, w_e3, w_e2, w_sh1, w_sh3, w_sh2, w_pe, w_pg, b_pg, ln2_g, ln2_b):
    bsz, seq, d = x.shape
    depth = w_in.shape[0]
    n_tok = bsz * seq
    alpha = (2.0 * depth) ** 0.25
    n_main = 2 * CONV_CH + 2 * GMLP_CH + 3 * ATT_CH
    assert w_in.shape[2] == n_main + ATT_HEADS and seq % CHUNK == 0

    bm = MOE_BLOCK
    n_blk = -(-n_tok * TOP_K // bm) + N_EXPERTS
    n_rows = n_blk * bm
    bf16 = jnp.bfloat16

    x2 = x.reshape(n_tok, d)
    for i in range(depth):
        w_main = w_in[i, :, :n_main].astype(bf16)
        w_f = jnp.pad(w_in[i, :, n_main:], ((0, 0), (0, LANES - ATT_HEADS))).astype(bf16)
        b_f_row = jnp.pad(b_f[i], (0, LANES - ATT_HEADS)).reshape(1, LANES)

        zc, zg, q, k, v, f = _in_projection(x2, w_main, w_f, _tile(n_tok, 256))
        ya = _conv_module(zc, conv_w[i], conv_b[i], conv_ln_g[i], conv_ln_b[i], out_g[i, :CONV_CH],
                          bsz, seq, _tile(seq, 256))
        yb = _gmlp_module(zg, gmlp_ln_g[i], gmlp_ln_b[i], w_sp[i], b_sp[i],
                          out_g[i, CONV_CH:CONV_CH + GMLP_CH], _tile(n_tok, 512))
        c = _forget_cumsum(f, b_f_row, bsz, seq, _tile(seq, 512))
        tq = _tile(seq, 2048)
        yc = _fox_attention(q, k, v, c, bsz, seq, tq, _tile(tq, 512), _tile(tq, 512))
        x1, x1p = _out_projection(ya, yb, yc, x2, w_o[i].astype(bf16), out_g[i, CONV_CH + GMLP_CH:],
                                  ln1_g[i], ln1_b[i], alpha, _tile(n_tok, 256))

        eidx, gate_t, rank, cnt = _router(x1, w_r[i], r_bias[i], _tile(n_tok, 512))
        counts = cnt[:, 0]
        padded = (counts + bm - 1) // bm * bm
        pend = jnp.cumsum(padded).astype(jnp.int32)
        pstart = pend - padded
        blk_start = jnp.arange(n_blk, dtype=jnp.int32) * bm
        blk_exp = jnp.minimum(jnp.sum(pend[None, :] <= blk_start[:, None], axis=1), N_EXPERTS - 1).astype(jnp.int32)
        n_valid = (pend[N_EXPERTS - 1:] // bm).astype(jnp.int32)

        dest = _dest_rows(pstart, eidx, rank, _tile(n_tok, 4096))
        xs = _dispatch(pend, dest, x1p, n_rows, _tile(n_tok, 256), bm)
        ys = _experts(blk_exp, n_valid, xs, w_e1, w_e3, w_e2, i, bm)
        x2 = _combine(dest, x1, p[i].reshape(n_tok, -1), gate_t, w_sh1[i].astype(bf16), w_sh3[i].astype(bf16),
                      w_sh2[i].astype(bf16), w_pg[i].astype(bf16), b_pg[i], w_pe[i].astype(bf16),
                      ln2_g[i], ln2_b[i], ys, alpha, _tile(n_tok, 256))
    return x2.reshape(bsz, seq, d)
```

```python
import functools

import jax
import jax.numpy as jnp
from jax import lax
from jax.experimental import pallas as pl
from jax.experimental.pallas import tpu as pltpu
from jax.experimental.pallas import tpu_sc as plsc

CONV_CH = 512
CONV_WIDTH = 31
GMLP_HEADS = 4
GMLP_HEAD_CH = 128
GMLP_CH = GMLP_HEADS * GMLP_HEAD_CH
CHUNK = 128
ATT_HEADS = 8
HEAD_DIM = 128
ATT_CH = ATT_HEADS * HEAD_DIM
N_EXPERTS = 64
TOP_K = 8
N_GROUPS = 8
GROUP_SIZE = N_EXPERTS // N_GROUPS
TOPK_GROUPS = 4
ROUTE_SCALE = 2.5
LN_EPS = 1e-5
LOG2E = 1.4426950408889634

LANES = 128
SUBLANES = 8
VMEM_BYTES_V7X = 64 * 1024 * 1024
SC_CORES_V7X = 2
SC_SUBCORES_V7X = 16

CONV_HALO = 32
CONV_ROWS = 32
MOE_BLOCK = 512
SC_DISPATCH_CHUNK = 64


def _vmem_limit(nbytes):
    return int(min(nbytes, VMEM_BYTES_V7X - 8 * 1024 * 1024))


def _params(semantics, vmem_bytes):
    return pltpu.CompilerParams(dimension_semantics=semantics, vmem_limit_bytes=_vmem_limit(vmem_bytes))


def _const_spec(shape):
    nd = len(shape)
    return pl.BlockSpec(shape, lambda *_: (0,) * nd, pipeline_mode=pl.Buffered(1))


def _layer_norm_rows(x, g, b):
    mu = jnp.mean(x, axis=-1, keepdims=True)
    xc = x - mu
    var = jnp.mean(xc * xc, axis=-1, keepdims=True)
    return xc * lax.rsqrt(var + LN_EPS) * g + b


def _rms_rows(x):
    return x * lax.rsqrt(jnp.mean(x * x, axis=-1, keepdims=True) + LN_EPS)


def _sigmoid(x):
    return 1.0 / (1.0 + jnp.exp(-x))


def _silu(x):
    return x * _sigmoid(x)


def _inproj_kernel(x_ref, w_ref, wf_ref, zc_ref, zg_ref, q_ref, k_ref, v_ref, f_ref, *, q_scale):
    xb = x_ref[...].astype(jnp.bfloat16)
    wide = zc_ref.shape[1]

    def slab(n):
        return jnp.dot(xb, w_ref[:, n * wide:(n + 1) * wide], preferred_element_type=jnp.float32)

    f_ref[...] = jnp.dot(xb, wf_ref[...], preferred_element_type=jnp.float32)
    zc_ref[...] = slab(0)
    zg_ref[...] = slab(1)
    q_ref[...] = (slab(2) * q_scale).astype(jnp.bfloat16)
    k_ref[...] = slab(3).astype(jnp.bfloat16)
    v_ref[...] = slab(4).astype(jnp.bfloat16)


def _in_projection(x2, w_main, w_f, tm):
    n_tok, d = x2.shape
    wide = 2 * CONV_CH
    assert w_main.shape[1] == 5 * wide and ATT_CH == wide and 2 * GMLP_CH == wide
    row = lambda i: (i, 0)
    out_shape = (
        jax.ShapeDtypeStruct((n_tok, wide), jnp.float32),
        jax.ShapeDtypeStruct((n_tok, wide), jnp.float32),
        jax.ShapeDtypeStruct((n_tok, ATT_CH), jnp.bfloat16),
        jax.ShapeDtypeStruct((n_tok, ATT_CH), jnp.bfloat16),
        jax.ShapeDtypeStruct((n_tok, ATT_CH), jnp.bfloat16),
        jax.ShapeDtypeStruct((n_tok, LANES), jnp.float32),
    )
    vmem = (2 * tm * d * 4 + tm * d * 2 + d * 5 * wide * 2 + d * LANES * 2
            + 2 * 2 * tm * wide * 4 + 3 * 2 * tm * wide * 2 + 2 * tm * LANES * 4 + 2 * tm * wide * 4)
    return pl.pallas_call(
        functools.partial(_inproj_kernel, q_scale=HEAD_DIM ** -0.5 * LOG2E),
        grid=(n_tok // tm,),
        in_specs=[
            pl.BlockSpec((tm, d), row),
            _const_spec((d, 5 * wide)),
            _const_spec((d, LANES)),
        ],
        out_specs=[
            pl.BlockSpec((tm, wide), row),
            pl.BlockSpec((tm, wide), row),
            pl.BlockSpec((tm, ATT_CH), row),
            pl.BlockSpec((tm, ATT_CH), row),
            pl.BlockSpec((tm, ATT_CH), row),
            pl.BlockSpec((tm, LANES), row),
        ],
        out_shape=out_shape,
        compiler_params=_params(("arbitrary",), vmem + (8 << 20)),
        name="in_projection",
    )(x2, w_main, w_f)


def _conv_kernel(z_ref, w_ref, cb_ref, lng_ref, lnb_ref, og_ref, o_ref, hbuf_ref, sh_ref, *, ts):
    s = pl.program_id(1)
    span = CONV_HALO + ts - SUBLANES

    @pl.when(s == 0)
    def _():
        hbuf_ref[0:CONV_HALO, :] = jnp.zeros((CONV_HALO, CONV_CH), jnp.float32)

    @pl.when(s > 0)
    def _():
        hbuf_ref[0:CONV_HALO, :] = hbuf_ref[ts:ts + CONV_HALO, :]

    hbuf_ref[CONV_HALO:CONV_HALO + ts, :] = z_ref[:, :CONV_CH] * _sigmoid(z_ref[:, CONV_CH:])

    cb = cb_ref[...]
    lng = lng_ref[...]
    lnb = lnb_ref[...]
    og = og_ref[...]
    for r in range(1, SUBLANES):
        sh_ref[r - 1] = hbuf_ref[r:r + span, :]
    first = CONV_HALO - (CONV_WIDTH - 1)
    for c in range(ts // CONV_ROWS):
        acc = jnp.broadcast_to(cb, (CONV_ROWS, CONV_CH))
        for j in range(CONV_WIDTH):
            off = first + j + c * CONV_ROWS
            r = off % SUBLANES
            a = off - r
            rows = hbuf_ref[a:a + CONV_ROWS, :] if r == 0 else sh_ref[r - 1, a:a + CONV_ROWS, :]
            acc = acc + w_ref[j:j + 1, :] * rows
        y = _silu(_layer_norm_rows(acc, lng, lnb))
        o_ref[c * CONV_ROWS:(c + 1) * CONV_ROWS, :] = (_rms_rows(y) * og).astype(jnp.bfloat16)


def _conv_module(zc, conv_w, conv_b, ln_g, ln_b, og, bsz, seq, ts):
    n_tok = zc.shape[0]
    n_s = seq // ts
    vec = lambda v: v.reshape(1, CONV_CH)
    return pl.pallas_call(
        functools.partial(_conv_kernel, ts=ts),
        grid=(bsz, n_s),
        in_specs=[
            pl.BlockSpec((ts, 2 * CONV_CH), lambda b, s: (b * n_s + s, 0)),
            _const_spec((CONV_WIDTH, CONV_CH)),
            _const_spec((1, CONV_CH)),
            _const_spec((1, CONV_CH)),
            _const_spec((1, CONV_CH)),
            _const_spec((1, CONV_CH)),
        ],
        out_specs=pl.BlockSpec((ts, CONV_CH), lambda b, s: (b * n_s + s, 0)),
        out_shape=jax.ShapeDtypeStruct((n_tok, CONV_CH), jnp.bfloat16),
        scratch_shapes=[pltpu.VMEM((CONV_HALO + ts, CONV_CH), jnp.float32),
                        pltpu.VMEM((SUBLANES - 1, CONV_HALO + ts - SUBLANES, CONV_CH), jnp.float32)],
        compiler_params=_params(("arbitrary", "arbitrary"), 32 << 20),
        name="conv_module",
    )(zc, conv_w, vec(conv_b), vec(ln_g), vec(ln_b), vec(og))


def _gelu_tanh(x):
    c = (2.0 / jnp.pi) ** 0.5
    return 0.5 * x * (1.0 + jnp.tanh(c * (x + 0.044715 * (x * x * x))))


def _gmlp_kernel(z_ref, lng_ref, lnb_ref, wsp_ref, bsp_ref, og_ref, o_ref, y_ref, *, tg):
    z = _gelu_tanh(z_ref[...])
    u = z[:, :GMLP_CH]
    v = _layer_norm_rows(z[:, GMLP_CH:], lng_ref[...], lnb_ref[...]).astype(jnp.bfloat16)
    t_idx = lax.broadcasted_iota(jnp.int32, (CHUNK, CHUNK), 0)
    s_idx = lax.broadcasted_iota(jnp.int32, (CHUNK, CHUNK), 1)
    causal = s_idx <= t_idx
    for h in range(GMLP_HEADS):
        ws = jnp.where(causal, wsp_ref[h], 0.0).astype(jnp.bfloat16)
        bias = bsp_ref[h]
        cols = slice(h * GMLP_HEAD_CH, (h + 1) * GMLP_HEAD_CH)
        for c in range(tg // CHUNK):
            rows = slice(c * CHUNK, (c + 1) * CHUNK)
            mixed = jnp.dot(ws, v[rows, cols], preferred_element_type=jnp.float32) + bias
            y_ref[rows, cols] = u[rows, cols] * mixed
    o_ref[...] = (_rms_rows(y_ref[...]) * og_ref[...]).astype(jnp.bfloat16)


def _gmlp_module(zg, ln_g, ln_b, w_sp, b_sp, og, tg):
    n_tok = zg.shape[0]
    vec = lambda v: v.reshape(1, GMLP_CH)
    return pl.pallas_call(
        functools.partial(_gmlp_kernel, tg=tg),
        grid=(n_tok // tg,),
        in_specs=[
            pl.BlockSpec((tg, 2 * GMLP_CH), lambda i: (i, 0)),
            _const_spec((1, GMLP_CH)),
            _const_spec((1, GMLP_CH)),
            _const_spec((GMLP_HEADS, CHUNK, CHUNK)),
            _const_spec((GMLP_HEADS, CHUNK, 1)),
            _const_spec((1, GMLP_CH)),
        ],
        out_specs=pl.BlockSpec((tg, GMLP_CH), lambda i: (i, 0)),
        out_shape=jax.ShapeDtypeStruct((n_tok, GMLP_CH), jnp.bfloat16),
        scratch_shapes=[pltpu.VMEM((tg, GMLP_CH), jnp.float32)],
        compiler_params=_params(("arbitrary",), 32 << 20),
        name="gmlp_module",
    )(zg, vec(ln_g), vec(ln_b), w_sp, b_sp.reshape(GMLP_HEADS, CHUNK, 1), vec(og))


def _split3(x):
    hi = x.astype(jnp.bfloat16)
    r1 = x - hi.astype(jnp.float32)
    mid = r1.astype(jnp.bfloat16)
    lo = (r1 - mid.astype(jnp.float32)).astype(jnp.bfloat16)
    return hi, mid, lo


def _fcum_kernel(f_ref, bf_ref, c_ref, carry_ref, *, ts):
    s = pl.program_id(1)

    @pl.when(s == 0)
    def _():
        carry_ref[...] = jnp.zeros_like(carry_ref)

    x = f_ref[...] + bf_ref[...]
    ls = -(jnp.maximum(-x, 0.0) + jnp.log(1.0 + jnp.exp(-jnp.abs(x))))
    t_idx = lax.broadcasted_iota(jnp.int32, (ts, ts), 0)
    s_idx = lax.broadcasted_iota(jnp.int32, (ts, ts), 1)
    tri = jnp.where(s_idx <= t_idx, 1.0, 0.0).astype(jnp.bfloat16)
    hi, mid, lo = _split3(ls)
    c = (jnp.dot(tri, lo, preferred_element_type=jnp.float32)
         + jnp.dot(tri, mid, preferred_element_type=jnp.float32)
         + jnp.dot(tri, hi, preferred_element_type=jnp.float32)) + carry_ref[...]
    c_ref[...] = c * LOG2E
    carry_ref[...] = c[ts - 1:ts, :]


def _forget_cumsum(f, b_f_row, bsz, seq, ts):
    n_tok = f.shape[0]
    n_s = seq // ts
    return pl.pallas_call(
        functools.partial(_fcum_kernel, ts=ts),
        grid=(bsz, n_s),
        in_specs=[
            pl.BlockSpec((ts, LANES), lambda b, s: (b * n_s + s, 0)),
            _const_spec((1, LANES)),
        ],
        out_specs=pl.BlockSpec((ts, LANES), lambda b, s: (b * n_s + s, 0)),
        out_shape=jax.ShapeDtypeStruct((n_tok, LANES), jnp.float32),
        scratch_shapes=[pltpu.VMEM((1, LANES), jnp.float32)],
        compiler_params=_params(("arbitrary", "arbitrary"), 32 << 20),
        name="forget_cumsum",
    )(f, b_f_row)


def _bias_columns(c_tile, h, as_query):
    rows = c_tile.shape[0]
    lane = lax.broadcasted_iota(jnp.int32, (rows, LANES), 1)
    col = jnp.sum(jnp.where(lane == h, c_tile, 0.0), axis=1, keepdims=True)
    val = jnp.broadcast_to(col if as_query else -col, (rows, LANES))
    hi, mid, lo = (t.astype(jnp.float32) for t in _split3(val))
    base = 0 if as_query else 3
    out = jnp.where(lane < 6, 1.0, 0.0)
    for offset, term in enumerate((hi, mid, lo)):
        out = jnp.where(lane == base + offset, term, out)
    return out.astype(jnp.bfloat16)


def _fox_kernel(q_ref, k_ref, v_ref, cq_ref, ck_ref, o_ref, kaug_ref, vt_ref, qaug_ref, m_ref, l_ref, acc_ref, s_ref,
                *, tq, tk, cw, seq):
    h = pl.program_id(1)
    i = pl.program_id(2)
    n_chain = tq // cw

    @pl.when(i == 0)
    def _():
        for r in range(seq // tk):
            rows = slice(r * tk, (r + 1) * tk)
            kaug_ref[rows, :HEAD_DIM] = k_ref[rows, :]
            kaug_ref[rows, HEAD_DIM:] = _bias_columns(ck_ref[rows, :], h, as_query=False)
            vt_ref[:, rows] = v_ref[rows, :].astype(jnp.float32).T.astype(jnp.bfloat16)

    qaug_ref[:, :HEAD_DIM] = q_ref[...]
    qaug_ref[:, HEAD_DIM:] = _bias_columns(cq_ref[...], h, as_query=True)
    m_ref[...] = jnp.full(m_ref.shape, -jnp.inf, jnp.float32)
    l_ref[...] = jnp.zeros_like(l_ref)
    acc_ref[...] = jnp.zeros_like(acc_ref)

    def scores(slot, c, k0):
        s_ref[slot] = lax.dot_general(kaug_ref[pl.ds(k0, tk), :], qaug_ref[c * cw:(c + 1) * cw, :],
                                      (((1,), (1,)), ((), ())), preferred_element_type=jnp.float32)

    def absorb(slot, c, k0, key_minus_query):
        s = s_ref[slot]
        if key_minus_query is not None:
            key = lax.broadcasted_iota(jnp.int32, (tk, cw), 0) + key_minus_query
            qry = lax.broadcasted_iota(jnp.int32, (tk, cw), 1)
            s = jnp.where(key <= qry, s, -jnp.inf)
        m_prev = m_ref[c]
        m_new = jnp.maximum(m_prev, jnp.max(s, axis=0, keepdims=True))
        alpha = jnp.exp2(m_prev - m_new)
        p = jnp.exp2(s - m_new)
        l_ref[c] = alpha * l_ref[c] + jnp.sum(p, axis=0, keepdims=True)
        pv = jnp.dot(vt_ref[:, pl.ds(k0, tk)], p.astype(jnp.bfloat16), preferred_element_type=jnp.float32)
        acc_ref[c] = alpha * acc_ref[c] + pv
        m_ref[c] = m_new

    n_kb = tq // tk

    def query_tile_keys(base, masks):
        work = [(kb, c) for kb in range(n_kb) for c in range(n_chain) if masks[kb][c] is not False]
        for slot, (kb, c) in enumerate(work):
            scores(slot, c, pl.multiple_of(base + kb * tk, tk))
        for slot, (kb, c) in enumerate(work):
            absorb(slot, c, pl.multiple_of(base + kb * tk, tk), masks[kb][c])

    def body(j, carry):
        query_tile_keys(j * tq, [[None] * n_chain] * n_kb)
        return carry

    lax.fori_loop(0, i, body, 0)
    masks = []
    for kb in range(n_kb):
        row = []
        for c in range(n_chain):
            first_key, first_query = kb * tk, c * cw
            if first_key + tk - 1 <= first_query:
                row.append(None)
            elif first_key <= first_query + cw - 1:
                row.append(first_key - first_query)
            else:
                row.append(False)
        masks.append(row)
    query_tile_keys(i * tq, masks)
    for c in range(n_chain):
        o_ref[c * cw:(c + 1) * cw, :] = (acc_ref[c] / l_ref[c]).T.astype(jnp.bfloat16)


def _fox_attention(q, k, v, c, bsz, seq, tq, tk, cw):
    n_tok = q.shape[0]
    nq = seq // tq
    n_chain = tq // cw
    return pl.pallas_call(
        functools.partial(_fox_kernel, tq=tq, tk=tk, cw=cw, seq=seq),
        grid=(bsz, ATT_HEADS, nq),
        in_specs=[
            pl.BlockSpec((tq, HEAD_DIM), lambda b, h, i: (b * nq + i, h)),
            pl.BlockSpec((seq, HEAD_DIM), lambda b, h, i: (b, h)),
            pl.BlockSpec((seq, HEAD_DIM), lambda b, h, i: (b, h)),
            pl.BlockSpec((tq, LANES), lambda b, h, i: (b * nq + i, 0)),
            pl.BlockSpec((seq, LANES), lambda b, h, i: (b, 0)),
        ],
        out_specs=pl.BlockSpec((tq, HEAD_DIM), lambda b, h, i: (b * nq + i, h)),
        out_shape=jax.ShapeDtypeStruct((n_tok, ATT_CH), jnp.bfloat16),
        scratch_shapes=[
            pltpu.VMEM((seq, 2 * HEAD_DIM), jnp.bfloat16),
            pltpu.VMEM((HEAD_DIM, seq), jnp.bfloat16),
            pltpu.VMEM((tq, 2 * HEAD_DIM), jnp.bfloat16),
            pltpu.VMEM((n_chain, 1, cw), jnp.float32),
            pltpu.VMEM((n_chain, 1, cw), jnp.float32),
            pltpu.VMEM((n_chain, HEAD_DIM, cw), jnp.float32),
            pltpu.VMEM((n_chain * (tq // tk), tk, cw), jnp.float32),
        ],
        compiler_params=_params(("arbitrary", "arbitrary", "arbitrary"), 56 << 20),
        name="fox_attention",
    )(q, k, v, c, c)


def _pack_bf16_pairs(x):
    n = x.shape[1] // 2
    r = x.astype(jnp.bfloat16).astype(jnp.float32)
    lo = lax.bitcast_convert_type(r[:, :n], jnp.uint32)
    hi = lax.bitcast_convert_type(r[:, n:], jnp.uint32)
    return (lo >> 16) | (hi & jnp.uint32(0xFFFF0000))


def _unpack_bf16_pairs(w):
    lo = lax.bitcast_convert_type(w << 16, jnp.float32).astype(jnp.bfloat16)
    hi = lax.bitcast_convert_type(w & jnp.uint32(0xFFFF0000), jnp.float32).astype(jnp.bfloat16)
    return lo, hi


def _outproj_kernel(ya_ref, yb_ref, yc_ref, x_ref, wo_ref, ogc_ref, g_ref, b_ref, x1_ref, x1p_ref, y_ref, *, alpha):
    y_ref[:, 0:CONV_CH] = ya_ref[...]
    y_ref[:, CONV_CH:CONV_CH + GMLP_CH] = yb_ref[...]
    yc = yc_ref[...].astype(jnp.float32)
    y_ref[:, CONV_CH + GMLP_CH:] = (_rms_rows(yc) * ogc_ref[...]).astype(jnp.bfloat16)
    h = jnp.dot(y_ref[...], wo_ref[...], preferred_element_type=jnp.float32)
    x1 = _layer_norm_rows(alpha * x_ref[...] + h, g_ref[...], b_ref[...])
    x1_ref[...] = x1
    x1p_ref[...] = _pack_bf16_pairs(x1)


def _out_projection(ya, yb, yc, x2, w_o, og_c, ln_g, ln_b, alpha, tm):
    n_tok, d = x2.shape
    mix = w_o.shape[0]
    row = lambda i: (i, 0)
    vmem = (mix * d * 2 + 2 * tm * d * 4 * 2 + 2 * tm * (d // 2) * 4 + 2 * tm * mix * 2 + tm * mix * 2 + 3 * tm * d * 4)
    return pl.pallas_call(
        functools.partial(_outproj_kernel, alpha=alpha),
        grid=(n_tok // tm,),
        in_specs=[
            pl.BlockSpec((tm, CONV_CH), row),
            pl.BlockSpec((tm, GMLP_CH), row),
            pl.BlockSpec((tm, ATT_CH), row),
            pl.BlockSpec((tm, d), row),
            _const_spec((mix, d)),
            _const_spec((1, ATT_CH)),
            _const_spec((1, d)),
            _const_spec((1, d)),
        ],
        out_specs=[pl.BlockSpec((tm, d), row), pl.BlockSpec((tm, d // 2), row)],
        out_shape=(
            jax.ShapeDtypeStruct((n_tok, d), jnp.float32),
            jax.ShapeDtypeStruct((n_tok, d // 2), jnp.uint32),
        ),
        scratch_shapes=[pltpu.VMEM((tm, mix), jnp.bfloat16)],
        compiler_params=_params(("arbitrary",), vmem + (8 << 20)),
        name="out_projection",
    )(ya, yb, yc, x2, w_o, og_c.reshape(1, ATT_CH), ln_g.reshape(1, d), ln_b.reshape(1, d))


def _first_argmax_rows(x, idx):
    m = jnp.max(x, axis=0, keepdims=True)
    first = jnp.min(jnp.where(x == m, idx, x.shape[0]), axis=0, keepdims=True)
    return m, first


def _router_kernel(x_ref, wh_ref, wl_ref, rb_ref, eidx_ref, gate_t_ref, rank_ref, cnt_ref, carry_ref, *, tr):
    i = pl.program_id(0)

    @pl.when(i == 0)
    def _():
        carry_ref[...] = jnp.zeros_like(carry_ref)

    x = x_ref[...]
    xh = x.astype(jnp.bfloat16)
    xl = (x - xh.astype(jnp.float32)).astype(jnp.bfloat16)
    nt = (((1,), (1,)), ((), ()))
    wh = wh_ref[...]
    logits = (lax.dot_general(wl_ref[...], xh, nt, preferred_element_type=jnp.float32)
              + lax.dot_general(wh, xl, nt, preferred_element_type=jnp.float32)
              + lax.dot_general(wh, xh, nt, preferred_element_type=jnp.float32))
    scores = _sigmoid(logits)
    sel = scores + rb_ref[...]

    neg = -jnp.inf
    e_idx = lax.broadcasted_iota(jnp.int32, (N_EXPERTS, tr), 0)
    g_idx = lax.broadcasted_iota(jnp.int32, (N_GROUPS, tr), 0)
    in_idx = lax.broadcasted_iota(jnp.int32, (GROUP_SIZE, tr), 0)

    gs_rows = []
    for g in range(N_GROUPS):
        blk = sel[g * GROUP_SIZE:(g + 1) * GROUP_SIZE, :]
        m1, a1 = _first_argmax_rows(blk, in_idx)
        m2 = jnp.max(jnp.where(in_idx == a1, neg, blk), axis=0, keepdims=True)
        gs_rows.append(m1 + m2)
    gs = jnp.concatenate(gs_rows, axis=0)

    gsel = jnp.zeros((N_GROUPS, tr), jnp.float32)
    for _ in range(TOPK_GROUPS):
        _, a = _first_argmax_rows(gs, g_idx)
        pick = g_idx == a
        gsel = jnp.where(pick, 1.0, gsel)
        gs = jnp.where(pick, neg, gs)
    esel = jnp.concatenate(
        [jnp.broadcast_to(gsel[g:g + 1, :], (GROUP_SIZE, tr)) for g in range(N_GROUPS)], axis=0)
    cand = jnp.where(esel > 0.5, sel, neg)

    picks, gates = [], []
    chosen = jnp.zeros((N_EXPERTS, tr), jnp.float32)
    for _ in range(TOP_K):
        _, a = _first_argmax_rows(cand, e_idx)
        pick = e_idx == a
        picks.append(a)
        gates.append(jnp.sum(jnp.where(pick, scores, 0.0), axis=0, keepdims=True))
        chosen = jnp.where(pick, 1.0, chosen)
        cand = jnp.where(pick, neg, cand)
    gate = jnp.concatenate(gates, axis=0)
    gate = gate / jnp.sum(gate, axis=0, keepdims=True) * ROUTE_SCALE

    r_idx = lax.broadcasted_iota(jnp.int32, (tr, tr), 0)
    c_idx = lax.broadcasted_iota(jnp.int32, (tr, tr), 1)
    upper = jnp.where(r_idx <= c_idx, 1.0, 0.0).astype(jnp.bfloat16)
    incl = jnp.dot(chosen.astype(jnp.bfloat16), upper, preferred_element_type=jnp.float32)
    rank_all = carry_ref[...] + incl - chosen
    carry_ref[...] = carry_ref[...] + incl[:, tr - 1:tr]
    ranks = [jnp.sum(jnp.where(e_idx == a, rank_all, 0.0), axis=0, keepdims=True) for a in picks]

    eidx_ref[...] = jnp.concatenate(picks, axis=0)
    rank_ref[...] = jnp.concatenate(ranks, axis=0).astype(jnp.int32)
    pad = jnp.zeros((LANES - TOP_K, tr), jnp.float32)
    gate_t_ref[...] = jnp.concatenate([gate, pad], axis=0).T
    cnt_ref[...] = jnp.broadcast_to(carry_ref[...], (N_EXPERTS, LANES)).astype(jnp.int32)


def _router(x1, w_r, r_bias, tr):
    n_tok, d = x1.shape
    wt = w_r.T
    wh = wt.astype(jnp.bfloat16)
    wl = (wt - wh.astype(jnp.float32)).astype(jnp.bfloat16)
    col = lambda i: (0, i)
    return pl.pallas_call(
        functools.partial(_router_kernel, tr=tr),
        grid=(n_tok // tr,),
        in_specs=[
            pl.BlockSpec((tr, d), lambda i: (i, 0)),
            _const_spec((N_EXPERTS, d)),
            _const_spec((N_EXPERTS, d)),
            _const_spec((N_EXPERTS, 1)),
        ],
        out_specs=[
            pl.BlockSpec((TOP_K, tr), col),
            pl.BlockSpec((tr, LANES), lambda i: (i, 0)),
            pl.BlockSpec((TOP_K, tr), col),
            pl.BlockSpec((N_EXPERTS, LANES), lambda i: (0, 0)),
        ],
        out_shape=(
            jax.ShapeDtypeStruct((TOP_K, n_tok), jnp.int32),
            jax.ShapeDtypeStruct((n_tok, LANES), jnp.float32),
            jax.ShapeDtypeStruct((TOP_K, n_tok), jnp.int32),
            jax.ShapeDtypeStruct((N_EXPERTS, LANES), jnp.int32),
        ),
        scratch_shapes=[pltpu.VMEM((N_EXPERTS, 1), jnp.float32)],
        compiler_params=_params(("arbitrary",), 32 << 20),
        name="router",
    )(x1, wh, wl, r_bias.reshape(N_EXPERTS, 1))


def _dest_kernel(pstart_ref, eidx_ref, rank_ref, dest_ref):
    e = eidx_ref[...]
    base = jnp.zeros(e.shape, jnp.int32)
    for k in range(N_EXPERTS):
        base = jnp.where(e == k, pstart_ref[k], base)
    dest_ref[...] = base + rank_ref[...]


def _dest_rows(pstart, eidx, rank, tc):
    n_tok = eidx.shape[1]
    col = lambda i, ps: (0, i)
    return pl.pallas_call(
        _dest_kernel,
        grid_spec=pltpu.PrefetchScalarGridSpec(
            num_scalar_prefetch=1,
            grid=(n_tok // tc,),
            in_specs=[pl.BlockSpec((TOP_K, tc), col), pl.BlockSpec((TOP_K, tc), col)],
            out_specs=pl.BlockSpec((TOP_K, tc), col),
        ),
        out_shape=jax.ShapeDtypeStruct((TOP_K, n_tok), jnp.int32),
        compiler_params=_params(("arbitrary",), 32 << 20),
        name="dest_rows",
    )(pstart, eidx, rank)


def _dispatch(dest, x1p, n_rows):
    n_tok, width = x1p.shape
    c = SC_DISPATCH_CHUNK
    n_workers = SC_CORES_V7X * SC_SUBCORES_V7X
    n_chunks = n_tok // c
    assert n_tok % c == 0 and n_chunks % n_workers == 0
    per_worker = n_chunks // n_workers
    idx = dest.reshape(TOP_K, n_chunks, c).transpose(1, 0, 2)
    mesh = plsc.VectorSubcoreMesh(core_axis_name="c", subcore_axis_name="s")

    @functools.partial(
        pl.kernel, mesh=mesh,
        out_type=jax.ShapeDtypeStruct((n_rows, width), x1p.dtype),
        scratch_types=[pltpu.VMEM((TOP_K, c), jnp.int32), pltpu.VMEM((c, width), x1p.dtype), pltpu.SemaphoreType.DMA],
        name="dispatch",
    )
    def scatter_rows(idx_hbm, x_hbm, xs_hbm, idx_v, rows_v, sem):
        worker = lax.axis_index("s") * SC_CORES_V7X + lax.axis_index("c")

        @pl.loop(0, per_worker)
        def _(g):
            chunk = worker * per_worker + g
            pltpu.sync_copy(idx_hbm.at[chunk], idx_v)
            pltpu.sync_copy(x_hbm.at[pl.ds(chunk * c, c)], rows_v)
            copies = [pltpu.make_async_copy(rows_v, xs_hbm.at[idx_v.at[j]], sem) for j in range(TOP_K)]
            for cp in copies:
                cp.start()
            for cp in copies:
                cp.wait()

    return scatter_rows(idx, x1p)


def _experts_kernel(be_ref, nv_ref, rows_ref, xs_ref, w1_ref, w3_ref, w2_ref, ys_ref, xb_ref, w1b_ref, w3b_ref,
                    w2b_ref):
    b = pl.program_id(0)

    @pl.when(b < nv_ref[0])
    def _():
        @pl.when((b == 0) | (be_ref[b] != be_ref[jnp.maximum(b - 1, 0)]))
        def _():
            w1b_ref[...] = w1_ref[0].astype(jnp.bfloat16)
            w3b_ref[...] = w3_ref[0].astype(jnp.bfloat16)
            w2b_ref[...] = w2_ref[0].astype(jnp.bfloat16)

        bm, half = xs_ref.shape
        routed = lax.broadcasted_iota(jnp.int32, (bm, half), 0) < rows_ref[b]
        lo, hi = _unpack_bf16_pairs(jnp.where(routed, xs_ref[...], jnp.uint32(0)))
        xb_ref[:, :half] = lo
        xb_ref[:, half:] = hi
        xb = xb_ref[...]
        h1 = jnp.dot(xb, w1b_ref[...], preferred_element_type=jnp.float32)
        h3 = jnp.dot(xb, w3b_ref[...], preferred_element_type=jnp.float32)
        h = (_silu(h1) * h3).astype(jnp.bfloat16)
        ys_ref[...] = _pack_bf16_pairs(jnp.dot(h, w2b_ref[...], preferred_element_type=jnp.float32))

    @pl.when(b >= nv_ref[0])
    def _():
        ys_ref[...] = jnp.zeros_like(ys_ref)


def _experts(blk_exp, n_valid, blk_rows, xs, w1, w3, w2, layer, bm):
    n_rows, half = xs.shape
    d = 2 * half
    de = w1.shape[3]
    n_blk = n_rows // bm
    blk = lambda b, be, nv, br: (jnp.minimum(b, nv[0] - 1), 0)
    wsel = lambda b, be, nv, br: (layer, be[jnp.minimum(b, nv[0] - 1)], 0, 0)
    vmem = (2 * bm * half * 4 + bm * d * 2 + 2 * 3 * d * de * 4 + 3 * d * de * 2 + 2 * bm * half * 4
            + 4 * bm * de * 4 + 2 * bm * d * 4)
    return pl.pallas_call(
        _experts_kernel,
        grid_spec=pltpu.PrefetchScalarGridSpec(
            num_scalar_prefetch=3,
            grid=(n_blk,),
            in_specs=[
                pl.BlockSpec((bm, half), blk),
                pl.BlockSpec((None, 1, d, de), wsel),
                pl.BlockSpec((None, 1, d, de), wsel),
                pl.BlockSpec((None, 1, de, d), wsel),
            ],
            out_specs=pl.BlockSpec((bm, half), lambda b, be, nv, br: (b, 0)),
            scratch_shapes=[pltpu.VMEM((bm, d), jnp.bfloat16), pltpu.VMEM((d, de), jnp.bfloat16),
                            pltpu.VMEM((d, de), jnp.bfloat16), pltpu.VMEM((de, d), jnp.bfloat16)],
        ),
        out_shape=jax.ShapeDtypeStruct((n_rows, half), jnp.uint32),
        compiler_params=_params(("arbitrary",), vmem + (8 << 20)),
        name="experts",
    )(blk_exp, n_valid, blk_rows, xs, w1, w3, w2)


def _combine_kernel(dest_ref, dest_next_ref, x1_ref, p_ref, gt_ref, w1_ref, w3_ref, w2_ref, wpg_ref, bpg_ref, wpe_ref,
                    g_ref, b_ref, ys_ref, o_ref, ybuf_ref, sems, *, tm, alpha):
    i = pl.program_id(0)
    slot = lax.rem(i, 2)
    other = 1 - slot

    def row_copy(d_ref, s, j, r):
        return pltpu.make_async_copy(ys_ref.at[pl.ds(d_ref[j, r], 1), :], ybuf_ref.at[s, j, pl.ds(r, 1), :], sems.at[s])

    def drain(s):
        for j in range(TOP_K):
            pltpu.make_async_copy(ys_ref.at[pl.ds(0, tm), :], ybuf_ref.at[s, j], sems.at[s]).wait()

    @pl.when(i == 0)
    def _():
        def first_tile(r, carry):
            for j in range(TOP_K):
                row_copy(dest_ref, 0, j, r).start()
            return carry

        lax.fori_loop(0, tm, first_tile, 0)

    for r in range(tm):
        for j in range(TOP_K):
            row_copy(dest_next_ref, other, j, r).start()

    x1 = x1_ref[...]
    xb = x1.astype(jnp.bfloat16)
    h1 = jnp.dot(xb, w1_ref[...], preferred_element_type=jnp.float32)
    h3 = jnp.dot(xb, w3_ref[...], preferred_element_type=jnp.float32)
    h = (_silu(h1) * h3).astype(jnp.bfloat16)
    total = alpha * x1 + jnp.dot(h, w2_ref[...], preferred_element_type=jnp.float32)
    gate = _sigmoid(jnp.dot(xb, wpg_ref[...], preferred_element_type=jnp.float32) + bpg_ref[...])
    pe = jnp.dot(p_ref[...].astype(jnp.bfloat16), wpe_ref[...], preferred_element_type=jnp.float32)
    total = total + gate * pe

    drain(slot)
    gt = gt_ref[...]
    half = ybuf_ref.shape[3]
    moe_lo = jnp.zeros((tm, half), jnp.float32)
    moe_hi = jnp.zeros((tm, half), jnp.float32)
    for j in range(TOP_K):
        w = ybuf_ref[slot, j]
        g = gt[:, j:j + 1]
        moe_lo = moe_lo + g * lax.bitcast_convert_type(w << 16, jnp.float32)
        moe_hi = moe_hi + g * lax.bitcast_convert_type(w & jnp.uint32(0xFFFF0000), jnp.float32)
    total = total + jnp.concatenate([moe_lo, moe_hi], axis=1)
    o_ref[...] = _layer_norm_rows(total, g_ref[...], b_ref[...])

    @pl.when(i == pl.num_programs(0) - 1)
    def _():
        drain(other)


def _combine(dest, x1, p2, gate_t, w_sh1, w_sh3, w_sh2, w_pg, b_pg, w_pe, ln_g, ln_b, ys, alpha, tm):
    n_tok, d = x1.shape
    ds = w_sh1.shape[1]
    dp = p2.shape[1]
    row = lambda i: (i, 0)
    n_steps = n_tok // tm
    vmem = (2 * TOP_K * tm * (d // 2) * 4 + (3 * d * ds + d * d + dp * d) * 2 + 2 * 2 * tm * d * 4 + 2 * tm * dp * 4
            + 2 * tm * LANES * 4 + 6 * tm * d * 4)
    return pl.pallas_call(
        functools.partial(_combine_kernel, tm=tm, alpha=alpha),
        grid=(n_steps,),
        in_specs=[
            pl.BlockSpec((TOP_K, tm), lambda i: (0, i), memory_space=pltpu.SMEM),
            pl.BlockSpec((TOP_K, tm), lambda i: (0, jnp.minimum(i + 1, n_steps - 1)), memory_space=pltpu.SMEM),
            pl.BlockSpec((tm, d), row),
            pl.BlockSpec((tm, dp), row),
            pl.BlockSpec((tm, LANES), row),
            _const_spec((d, ds)),
            _const_spec((d, ds)),
            _const_spec((ds, d)),
            _const_spec((d, d)),
            _const_spec((1, d)),
            _const_spec((dp, d)),
            _const_spec((1, d)),
            _const_spec((1, d)),
            pl.BlockSpec(memory_space=pl.ANY),
        ],
        out_specs=pl.BlockSpec((tm, d), row),
        out_shape=jax.ShapeDtypeStruct((n_tok, d), jnp.float32),
        scratch_shapes=[pltpu.VMEM((2, TOP_K, tm, d // 2), jnp.uint32), pltpu.SemaphoreType.DMA((2,))],
        compiler_params=_params(("arbitrary",), vmem + (8 << 20)),
        name="combine",
    )(dest, dest, x1, p2, gate_t, w_sh1, w_sh3, w_sh2, w_pg, b_pg.reshape(1, d), w_pe, ln_g.reshape(1, d),
      ln_b.reshape(1, d), ys)


def _tile(n, target):
    t = min(n, target)
    assert n % t == 0, (n, t)
    return t


def kernel(x, p, w_in, b_f, conv_w, conv_b, conv_ln_g, conv_ln_b, gmlp_ln_g, gmlp_ln_b, w_sp, b_sp, out_g, w_o,
           ln1_g, ln1_b, w_r, r_bias, w_e1, w_e3, w_e2, w_sh1, w_sh3, w_sh2, w_pe, w_pg, b_pg, ln2_g, ln2_b):
    bsz, seq, d = x.shape
    depth = w_in.shape[0]
    n_tok = bsz * seq
    alpha = (2.0 * depth) ** 0.25
    n_main = 2 * CONV_CH + 2 * GMLP_CH + 3 * ATT_CH
    assert w_in.shape[2] == n_main + ATT_HEADS and seq % CHUNK == 0

    bm = MOE_BLOCK
    n_blk = -(-n_tok * TOP_K // bm) + N_EXPERTS
    n_rows = n_blk * bm
    bf16 = jnp.bfloat16

    x2 = x.reshape(n_tok, d)
    for i in range(depth):
        w_main = w_in[i, :, :n_main].astype(bf16)
        w_f = jnp.pad(w_in[i, :, n_main:], ((0, 0), (0, LANES - ATT_HEADS))).astype(bf16)
        b_f_row = jnp.pad(b_f[i], (0, LANES - ATT_HEADS)).reshape(1, LANES)

        zc, zg, q, k, v, f = _in_projection(x2, w_main, w_f, _tile(n_tok, 256))
        ya = _conv_module(zc, conv_w[i], conv_b[i], conv_ln_g[i], conv_ln_b[i], out_g[i, :CONV_CH],
                          bsz, seq, _tile(seq, 256))
        yb = _gmlp_module(zg, gmlp_ln_g[i], gmlp_ln_b[i], w_sp[i], b_sp[i],
                          out_g[i, CONV_CH:CONV_CH + GMLP_CH], _tile(n_tok, 512))
        c = _forget_cumsum(f, b_f_row, bsz, seq, _tile(seq, 512))
        tq = _tile(seq, 2048)
        yc = _fox_attention(q, k, v, c, bsz, seq, tq, _tile(tq, 512), _tile(tq, 512))
        x1, x1p = _out_projection(ya, yb, yc, x2, w_o[i].astype(bf16), out_g[i, CONV_CH + GMLP_CH:],
                                  ln1_g[i], ln1_b[i], alpha, _tile(n_tok, 256))

        eidx, gate_t, rank, cnt = _router(x1, w_r[i], r_bias[i], _tile(n_tok, 512))
        counts = cnt[:, 0]
        padded = (counts + bm - 1) // bm * bm
        pend = jnp.cumsum(padded).astype(jnp.int32)
        pstart = pend - padded
        blk_start = jnp.arange(n_blk, dtype=jnp.int32) * bm
        blk_exp = jnp.minimum(jnp.sum(pend[None, :] <= blk_start[:, None], axis=1), N_EXPERTS - 1).astype(jnp.int32)
        n_valid = (pend[N_EXPERTS - 1:] // bm).astype(jnp.int32)
        blk_rows = jnp.clip((pstart + counts)[blk_exp] - blk_start, 0, bm).astype(jnp.int32)

        dest = _dest_rows(pstart, eidx, rank, _tile(n_tok, 4096))
        xs = _dispatch(dest, x1p, n_rows)
        ys = _experts(blk_exp, n_valid, blk_rows, xs, w_e1, w_e3, w_e2, i, bm)
        x2 = _combine(dest, x1, p[i].reshape(n_tok, -1), gate_t, w_sh1[i].astype(bf16), w_sh3[i].astype(bf16),
                      w_sh2[i].astype(bf16), w_pg[i].astype(bf16), b_pg[i], w_pe[i].astype(bf16),
                      ln2_g[i], ln2_b[i], ys, alpha, _tile(n_tok, 256))
    return x2.reshape(bsz, seq, d)
```

```python
import functools

import jax
import jax.numpy as jnp
from jax import lax
from jax.experimental import pallas as pl
from jax.experimental.pallas import tpu as pltpu
from jax.experimental.pallas import tpu_sc as plsc

CONV_CH = 512
CONV_WIDTH = 31
GMLP_HEADS = 4
GMLP_HEAD_CH = 128
GMLP_CH = GMLP_HEADS * GMLP_HEAD_CH
CHUNK = 128
ATT_HEADS = 8
HEAD_DIM = 128
ATT_CH = ATT_HEADS * HEAD_DIM
N_EXPERTS = 64
TOP_K = 8
N_GROUPS = 8
GROUP_SIZE = N_EXPERTS // N_GROUPS
TOPK_GROUPS = 4
ROUTE_SCALE = 2.5
LN_EPS = 1e-5
LOG2E = 1.4426950408889634

LANES = 128
SUBLANES = 8
VMEM_BYTES_V7X = 64 * 1024 * 1024
SC_CORES_V7X = 2
SC_SUBCORES_V7X = 16

CONV_HALO = 32
CONV_ROWS = 32
MOE_BLOCK = 512
SC_DISPATCH_CHUNK = 64
COMBINE_SPLITS = 2


def _vmem_limit(nbytes):
    return int(min(nbytes, VMEM_BYTES_V7X - 8 * 1024 * 1024))


def _params(semantics, vmem_bytes):
    return pltpu.CompilerParams(dimension_semantics=semantics, vmem_limit_bytes=_vmem_limit(vmem_bytes))


def _const_spec(shape):
    nd = len(shape)
    return pl.BlockSpec(shape, lambda *_: (0,) * nd, pipeline_mode=pl.Buffered(1))


def _layer_norm_rows(x, g, b):
    mu = jnp.mean(x, axis=-1, keepdims=True)
    xc = x - mu
    var = jnp.mean(xc * xc, axis=-1, keepdims=True)
    return xc * lax.rsqrt(var + LN_EPS) * g + b


def _rms_rows(x):
    return x * lax.rsqrt(jnp.mean(x * x, axis=-1, keepdims=True) + LN_EPS)


def _sigmoid(x):
    return 1.0 / (1.0 + jnp.exp(-x))


def _silu(x):
    return x * _sigmoid(x)


def _inproj_kernel(x_ref, w_ref, wf_ref, zc_ref, zg_ref, q_ref, k_ref, v_ref, f_ref, *, q_scale):
    xb = x_ref[...].astype(jnp.bfloat16)
    wide = zc_ref.shape[1]

    def slab(n):
        return jnp.dot(xb, w_ref[:, n * wide:(n + 1) * wide], preferred_element_type=jnp.float32)

    f_ref[...] = jnp.dot(xb, wf_ref[...], preferred_element_type=jnp.float32)
    zc_ref[...] = slab(0)
    zg_ref[...] = slab(1)
    q_ref[...] = (slab(2) * q_scale).astype(jnp.bfloat16)
    k_ref[...] = slab(3).astype(jnp.bfloat16)
    v_ref[...] = slab(4).astype(jnp.bfloat16)


def _in_projection(x2, w_main, w_f, tm):
    n_tok, d = x2.shape
    wide = 2 * CONV_CH
    assert w_main.shape[1] == 5 * wide and ATT_CH == wide and 2 * GMLP_CH == wide
    row = lambda i: (i, 0)
    out_shape = (
        jax.ShapeDtypeStruct((n_tok, wide), jnp.float32),
        jax.ShapeDtypeStruct((n_tok, wide), jnp.float32),
        jax.ShapeDtypeStruct((n_tok, ATT_CH), jnp.bfloat16),
        jax.ShapeDtypeStruct((n_tok, ATT_CH), jnp.bfloat16),
        jax.ShapeDtypeStruct((n_tok, ATT_CH), jnp.bfloat16),
        jax.ShapeDtypeStruct((n_tok, LANES), jnp.float32),
    )
    vmem = (2 * tm * d * 4 + tm * d * 2 + d * 5 * wide * 2 + d * LANES * 2
            + 2 * 2 * tm * wide * 4 + 3 * 2 * tm * wide * 2 + 2 * tm * LANES * 4 + 2 * tm * wide * 4)
    return pl.pallas_call(
        functools.partial(_inproj_kernel, q_scale=HEAD_DIM ** -0.5 * LOG2E),
        grid=(n_tok // tm,),
        in_specs=[
            pl.BlockSpec((tm, d), row),
            _const_spec((d, 5 * wide)),
            _const_spec((d, LANES)),
        ],
        out_specs=[
            pl.BlockSpec((tm, wide), row),
            pl.BlockSpec((tm, wide), row),
            pl.BlockSpec((tm, ATT_CH), row),
            pl.BlockSpec((tm, ATT_CH), row),
            pl.BlockSpec((tm, ATT_CH), row),
            pl.BlockSpec((tm, LANES), row),
        ],
        out_shape=out_shape,
        compiler_params=_params(("arbitrary",), vmem + (8 << 20)),
        name="in_projection",
    )(x2, w_main, w_f)


def _conv_kernel(z_ref, w_ref, cb_ref, lng_ref, lnb_ref, og_ref, o_ref, hbuf_ref, sh_ref, *, ts):
    s = pl.program_id(1)
    span = CONV_HALO + ts - SUBLANES

    @pl.when(s == 0)
    def _():
        hbuf_ref[0:CONV_HALO, :] = jnp.zeros((CONV_HALO, CONV_CH), jnp.float32)

    @pl.when(s > 0)
    def _():
        hbuf_ref[0:CONV_HALO, :] = hbuf_ref[ts:ts + CONV_HALO, :]

    hbuf_ref[CONV_HALO:CONV_HALO + ts, :] = z_ref[:, :CONV_CH] * _sigmoid(z_ref[:, CONV_CH:])

    cb = cb_ref[...]
    lng = lng_ref[...]
    lnb = lnb_ref[...]
    og = og_ref[...]
    for r in range(1, SUBLANES):
        sh_ref[r - 1] = hbuf_ref[r:r + span, :]
    first = CONV_HALO - (CONV_WIDTH - 1)
    for c in range(ts // CONV_ROWS):
        acc = jnp.broadcast_to(cb, (CONV_ROWS, CONV_CH))
        for j in range(CONV_WIDTH):
            off = first + j + c * CONV_ROWS
            r = off % SUBLANES
            a = off - r
            rows = hbuf_ref[a:a + CONV_ROWS, :] if r == 0 else sh_ref[r - 1, a:a + CONV_ROWS, :]
            acc = acc + w_ref[j:j + 1, :] * rows
        y = _silu(_layer_norm_rows(acc, lng, lnb))
        o_ref[c * CONV_ROWS:(c + 1) * CONV_ROWS, :] = (_rms_rows(y) * og).astype(jnp.bfloat16)


def _conv_module(zc, conv_w, conv_b, ln_g, ln_b, og, bsz, seq, ts):
    n_tok = zc.shape[0]
    n_s = seq // ts
    vec = lambda v: v.reshape(1, CONV_CH)
    return pl.pallas_call(
        functools.partial(_conv_kernel, ts=ts),
        grid=(bsz, n_s),
        in_specs=[
            pl.BlockSpec((ts, 2 * CONV_CH), lambda b, s: (b * n_s + s, 0)),
            _const_spec((CONV_WIDTH, CONV_CH)),
            _const_spec((1, CONV_CH)),
            _const_spec((1, CONV_CH)),
            _const_spec((1, CONV_CH)),
            _const_spec((1, CONV_CH)),
        ],
        out_specs=pl.BlockSpec((ts, CONV_CH), lambda b, s: (b * n_s + s, 0)),
        out_shape=jax.ShapeDtypeStruct((n_tok, CONV_CH), jnp.bfloat16),
        scratch_shapes=[pltpu.VMEM((CONV_HALO + ts, CONV_CH), jnp.float32),
                        pltpu.VMEM((SUBLANES - 1, CONV_HALO + ts - SUBLANES, CONV_CH), jnp.float32)],
        compiler_params=_params(("arbitrary", "arbitrary"), 32 << 20),
        name="conv_module",
    )(zc, conv_w, vec(conv_b), vec(ln_g), vec(ln_b), vec(og))


def _gelu_tanh(x):
    c = (2.0 / jnp.pi) ** 0.5
    return 0.5 * x * (1.0 + jnp.tanh(c * (x + 0.044715 * (x * x * x))))


def _gmlp_kernel(z_ref, lng_ref, lnb_ref, wsp_ref, bsp_ref, og_ref, o_ref, y_ref, *, tg):
    z = _gelu_tanh(z_ref[...])
    u = z[:, :GMLP_CH]
    v = _layer_norm_rows(z[:, GMLP_CH:], lng_ref[...], lnb_ref[...]).astype(jnp.bfloat16)
    t_idx = lax.broadcasted_iota(jnp.int32, (CHUNK, CHUNK), 0)
    s_idx = lax.broadcasted_iota(jnp.int32, (CHUNK, CHUNK), 1)
    causal = s_idx <= t_idx
    for h in range(GMLP_HEADS):
        ws = jnp.where(causal, wsp_ref[h], 0.0).astype(jnp.bfloat16)
        bias = bsp_ref[h]
        cols = slice(h * GMLP_HEAD_CH, (h + 1) * GMLP_HEAD_CH)
        for c in range(tg // CHUNK):
            rows = slice(c * CHUNK, (c + 1) * CHUNK)
            mixed = jnp.dot(ws, v[rows, cols], preferred_element_type=jnp.float32) + bias
            y_ref[rows, cols] = u[rows, cols] * mixed
    o_ref[...] = (_rms_rows(y_ref[...]) * og_ref[...]).astype(jnp.bfloat16)


def _gmlp_module(zg, ln_g, ln_b, w_sp, b_sp, og, tg):
    n_tok = zg.shape[0]
    vec = lambda v: v.reshape(1, GMLP_CH)
    return pl.pallas_call(
        functools.partial(_gmlp_kernel, tg=tg),
        grid=(n_tok // tg,),
        in_specs=[
            pl.BlockSpec((tg, 2 * GMLP_CH), lambda i: (i, 0)),
            _const_spec((1, GMLP_CH)),
            _const_spec((1, GMLP_CH)),
            _const_spec((GMLP_HEADS, CHUNK, CHUNK)),
            _const_spec((GMLP_HEADS, CHUNK, 1)),
            _const_spec((1, GMLP_CH)),
        ],
        out_specs=pl.BlockSpec((tg, GMLP_CH), lambda i: (i, 0)),
        out_shape=jax.ShapeDtypeStruct((n_tok, GMLP_CH), jnp.bfloat16),
        scratch_shapes=[pltpu.VMEM((tg, GMLP_CH), jnp.float32)],
        compiler_params=_params(("arbitrary",), 32 << 20),
        name="gmlp_module",
    )(zg, vec(ln_g), vec(ln_b), w_sp, b_sp.reshape(GMLP_HEADS, CHUNK, 1), vec(og))


def _split3(x):
    hi = x.astype(jnp.bfloat16)
    r1 = x - hi.astype(jnp.float32)
    mid = r1.astype(jnp.bfloat16)
    lo = (r1 - mid.astype(jnp.float32)).astype(jnp.bfloat16)
    return hi, mid, lo


def _fcum_kernel(f_ref, bf_ref, c_ref, carry_ref, *, ts):
    s = pl.program_id(1)

    @pl.when(s == 0)
    def _():
        carry_ref[...] = jnp.zeros_like(carry_ref)

    x = f_ref[...] + bf_ref[...]
    ls = -(jnp.maximum(-x, 0.0) + jnp.log(1.0 + jnp.exp(-jnp.abs(x))))
    t_idx = lax.broadcasted_iota(jnp.int32, (ts, ts), 0)
    s_idx = lax.broadcasted_iota(jnp.int32, (ts, ts), 1)
    tri = jnp.where(s_idx <= t_idx, 1.0, 0.0).astype(jnp.bfloat16)
    hi, mid, lo = _split3(ls)
    c = (jnp.dot(tri, lo, preferred_element_type=jnp.float32)
         + jnp.dot(tri, mid, preferred_element_type=jnp.float32)
         + jnp.dot(tri, hi, preferred_element_type=jnp.float32)) + carry_ref[...]
    c_ref[...] = c * LOG2E
    carry_ref[...] = c[ts - 1:ts, :]


def _forget_cumsum(f, b_f_row, bsz, seq, ts):
    n_tok = f.shape[0]
    n_s = seq // ts
    return pl.pallas_call(
        functools.partial(_fcum_kernel, ts=ts),
        grid=(bsz, n_s),
        in_specs=[
            pl.BlockSpec((ts, LANES), lambda b, s: (b * n_s + s, 0)),
            _const_spec((1, LANES)),
        ],
        out_specs=pl.BlockSpec((ts, LANES), lambda b, s: (b * n_s + s, 0)),
        out_shape=jax.ShapeDtypeStruct((n_tok, LANES), jnp.float32),
        scratch_shapes=[pltpu.VMEM((1, LANES), jnp.float32)],
        compiler_params=_params(("arbitrary", "arbitrary"), 32 << 20),
        name="forget_cumsum",
    )(f, b_f_row)


def _bias_columns(c_tile, h, as_query):
    rows = c_tile.shape[0]
    lane = lax.broadcasted_iota(jnp.int32, (rows, LANES), 1)
    col = jnp.sum(jnp.where(lane == h, c_tile, 0.0), axis=1, keepdims=True)
    val = jnp.broadcast_to(col if as_query else -col, (rows, LANES))
    hi, mid, lo = (t.astype(jnp.float32) for t in _split3(val))
    base = 0 if as_query else 3
    out = jnp.where(lane < 6, 1.0, 0.0)
    for offset, term in enumerate((hi, mid, lo)):
        out = jnp.where(lane == base + offset, term, out)
    return out.astype(jnp.bfloat16)


def _fox_kernel(q_ref, k_ref, v_ref, cq_ref, ck_ref, o_ref, kaug_ref, vt_ref, qaug_ref, m_ref, l_ref, acc_ref, s_ref,
                *, tq, tk, cw, seq):
    h = pl.program_id(1)
    i = pl.program_id(2)
    n_chain = tq // cw

    @pl.when(i == 0)
    def _():
        for r in range(seq // tk):
            rows = slice(r * tk, (r + 1) * tk)
            kaug_ref[rows, :HEAD_DIM] = k_ref[rows, :]
            kaug_ref[rows, HEAD_DIM:] = _bias_columns(ck_ref[rows, :], h, as_query=False)
            vt_ref[:, rows] = v_ref[rows, :].astype(jnp.float32).T.astype(jnp.bfloat16)

    qaug_ref[:, :HEAD_DIM] = q_ref[...]
    qaug_ref[:, HEAD_DIM:] = _bias_columns(cq_ref[...], h, as_query=True)
    m_ref[...] = jnp.full(m_ref.shape, -jnp.inf, jnp.float32)
    l_ref[...] = jnp.zeros_like(l_ref)
    acc_ref[...] = jnp.zeros_like(acc_ref)

    def scores(slot, c, k0):
        s_ref[slot] = lax.dot_general(kaug_ref[pl.ds(k0, tk), :], qaug_ref[c * cw:(c + 1) * cw, :],
                                      (((1,), (1,)), ((), ())), preferred_element_type=jnp.float32)

    def absorb(slot, c, k0, key_minus_query):
        s = s_ref[slot]
        if key_minus_query is not None:
            key = lax.broadcasted_iota(jnp.int32, (tk, cw), 0) + key_minus_query
            qry = lax.broadcasted_iota(jnp.int32, (tk, cw), 1)
            s = jnp.where(key <= qry, s, -jnp.inf)
        m_prev = m_ref[c]
        m_new = jnp.maximum(m_prev, jnp.max(s, axis=0, keepdims=True))
        alpha = jnp.exp2(m_prev - m_new)
        p = jnp.exp2(s - m_new)
        l_ref[c] = alpha * l_ref[c] + jnp.sum(p, axis=0, keepdims=True)
        pv = jnp.dot(vt_ref[:, pl.ds(k0, tk)], p.astype(jnp.bfloat16), preferred_element_type=jnp.float32)
        acc_ref[c] = alpha * acc_ref[c] + pv
        m_ref[c] = m_new

    n_kb = tq // tk

    def query_tile_keys(base, masks):
        work = [(kb, c) for kb in range(n_kb) for c in range(n_chain) if masks[kb][c] is not False]
        for slot, (kb, c) in enumerate(work):
            scores(slot, c, pl.multiple_of(base + kb * tk, tk))
        for slot, (kb, c) in enumerate(work):
            absorb(slot, c, pl.multiple_of(base + kb * tk, tk), masks[kb][c])

    def body(j, carry):
        query_tile_keys(j * tq, [[None] * n_chain] * n_kb)
        return carry

    lax.fori_loop(0, i, body, 0)
    masks = []
    for kb in range(n_kb):
        row = []
        for c in range(n_chain):
            first_key, first_query = kb * tk, c * cw
            if first_key + tk - 1 <= first_query:
                row.append(None)
            elif first_key <= first_query + cw - 1:
                row.append(first_key - first_query)
            else:
                row.append(False)
        masks.append(row)
    query_tile_keys(i * tq, masks)
    for c in range(n_chain):
        o_ref[c * cw:(c + 1) * cw, :] = (acc_ref[c] / l_ref[c]).T.astype(jnp.bfloat16)


def _fox_attention(q, k, v, c, bsz, seq, tq, tk, cw):
    n_tok = q.shape[0]
    nq = seq // tq
    n_chain = tq // cw
    return pl.pallas_call(
        functools.partial(_fox_kernel, tq=tq, tk=tk, cw=cw, seq=seq),
        grid=(bsz, ATT_HEADS, nq),
        in_specs=[
            pl.BlockSpec((tq, HEAD_DIM), lambda b, h, i: (b * nq + i, h)),
            pl.BlockSpec((seq, HEAD_DIM), lambda b, h, i: (b, h)),
            pl.BlockSpec((seq, HEAD_DIM), lambda b, h, i: (b, h)),
            pl.BlockSpec((tq, LANES), lambda b, h, i: (b * nq + i, 0)),
            pl.BlockSpec((seq, LANES), lambda b, h, i: (b, 0)),
        ],
        out_specs=pl.BlockSpec((tq, HEAD_DIM), lambda b, h, i: (b * nq + i, h)),
        out_shape=jax.ShapeDtypeStruct((n_tok, ATT_CH), jnp.bfloat16),
        scratch_shapes=[
            pltpu.VMEM((seq, 2 * HEAD_DIM), jnp.bfloat16),
            pltpu.VMEM((HEAD_DIM, seq), jnp.bfloat16),
            pltpu.VMEM((tq, 2 * HEAD_DIM), jnp.bfloat16),
            pltpu.VMEM((n_chain, 1, cw), jnp.float32),
            pltpu.VMEM((n_chain, 1, cw), jnp.float32),
            pltpu.VMEM((n_chain, HEAD_DIM, cw), jnp.float32),
            pltpu.VMEM((n_chain * (tq // tk), tk, cw), jnp.float32),
        ],
        compiler_params=_params(("arbitrary", "arbitrary", "arbitrary"), 56 << 20),
        name="fox_attention",
    )(q, k, v, c, c)


def _pack_bf16_pairs(x):
    n = x.shape[1] // 2
    r = x.astype(jnp.bfloat16).astype(jnp.float32)
    lo = lax.bitcast_convert_type(r[:, :n], jnp.uint32)
    hi = lax.bitcast_convert_type(r[:, n:], jnp.uint32)
    return (lo >> 16) | (hi & jnp.uint32(0xFFFF0000))


def _unpack_bf16_pairs(w):
    lo = lax.bitcast_convert_type(w << 16, jnp.float32).astype(jnp.bfloat16)
    hi = lax.bitcast_convert_type(w & jnp.uint32(0xFFFF0000), jnp.float32).astype(jnp.bfloat16)
    return lo, hi


def _outproj_kernel(ya_ref, yb_ref, yc_ref, x_ref, wo_ref, ogc_ref, g_ref, b_ref, x1_ref, x1p_ref, y_ref, *, alpha):
    y_ref[:, 0:CONV_CH] = ya_ref[...]
    y_ref[:, CONV_CH:CONV_CH + GMLP_CH] = yb_ref[...]
    yc = yc_ref[...].astype(jnp.float32)
    y_ref[:, CONV_CH + GMLP_CH:] = (_rms_rows(yc) * ogc_ref[...]).astype(jnp.bfloat16)
    h = jnp.dot(y_ref[...], wo_ref[...], preferred_element_type=jnp.float32)
    x1 = _layer_norm_rows(alpha * x_ref[...] + h, g_ref[...], b_ref[...])
    x1_ref[...] = x1
    x1p_ref[...] = _pack_bf16_pairs(x1)


def _out_projection(ya, yb, yc, x2, w_o, og_c, ln_g, ln_b, alpha, tm):
    n_tok, d = x2.shape
    mix = w_o.shape[0]
    row = lambda i: (i, 0)
    vmem = (mix * d * 2 + 2 * tm * d * 4 * 2 + 2 * tm * (d // 2) * 4 + 2 * tm * mix * 2 + tm * mix * 2 + 3 * tm * d * 4)
    return pl.pallas_call(
        functools.partial(_outproj_kernel, alpha=alpha),
        grid=(n_tok // tm,),
        in_specs=[
            pl.BlockSpec((tm, CONV_CH), row),
            pl.BlockSpec((tm, GMLP_CH), row),
            pl.BlockSpec((tm, ATT_CH), row),
            pl.BlockSpec((tm, d), row),
            _const_spec((mix, d)),
            _const_spec((1, ATT_CH)),
            _const_spec((1, d)),
            _const_spec((1, d)),
        ],
        out_specs=[pl.BlockSpec((tm, d), row), pl.BlockSpec((tm, d // 2), row)],
        out_shape=(
            jax.ShapeDtypeStruct((n_tok, d), jnp.float32),
            jax.ShapeDtypeStruct((n_tok, d // 2), jnp.uint32),
        ),
        scratch_shapes=[pltpu.VMEM((tm, mix), jnp.bfloat16)],
        compiler_params=_params(("arbitrary",), vmem + (8 << 20)),
        name="out_projection",
    )(ya, yb, yc, x2, w_o, og_c.reshape(1, ATT_CH), ln_g.reshape(1, d), ln_b.reshape(1, d))


def _first_argmax_rows(x, idx):
    m = jnp.max(x, axis=0, keepdims=True)
    first = jnp.min(jnp.where(x == m, idx, x.shape[0]), axis=0, keepdims=True)
    return m, first


def _router_kernel(x_ref, wh_ref, wl_ref, rb_ref, eidx_ref, gate_t_ref, rank_ref, cnt_ref, carry_ref, *, tr):
    i = pl.program_id(0)

    @pl.when(i == 0)
    def _():
        carry_ref[...] = jnp.zeros_like(carry_ref)

    x = x_ref[...]
    xh = x.astype(jnp.bfloat16)
    xl = (x - xh.astype(jnp.float32)).astype(jnp.bfloat16)
    nt = (((1,), (1,)), ((), ()))
    wh = wh_ref[...]
    logits = (lax.dot_general(wl_ref[...], xh, nt, preferred_element_type=jnp.float32)
              + lax.dot_general(wh, xl, nt, preferred_element_type=jnp.float32)
              + lax.dot_general(wh, xh, nt, preferred_element_type=jnp.float32))
    scores = _sigmoid(logits)
    sel = scores + rb_ref[...]

    neg = -jnp.inf
    e_idx = lax.broadcasted_iota(jnp.int32, (N_EXPERTS, tr), 0)
    g_idx = lax.broadcasted_iota(jnp.int32, (N_GROUPS, tr), 0)
    in_idx = lax.broadcasted_iota(jnp.int32, (GROUP_SIZE, tr), 0)

    gs_rows = []
    for g in range(N_GROUPS):
        blk = sel[g * GROUP_SIZE:(g + 1) * GROUP_SIZE, :]
        m1, a1 = _first_argmax_rows(blk, in_idx)
        m2 = jnp.max(jnp.where(in_idx == a1, neg, blk), axis=0, keepdims=True)
        gs_rows.append(m1 + m2)
    gs = jnp.concatenate(gs_rows, axis=0)

    gsel = jnp.zeros((N_GROUPS, tr), jnp.float32)
    for _ in range(TOPK_GROUPS):
        _, a = _first_argmax_rows(gs, g_idx)
        pick = g_idx == a
        gsel = jnp.where(pick, 1.0, gsel)
        gs = jnp.where(pick, neg, gs)
    esel = jnp.concatenate(
        [jnp.broadcast_to(gsel[g:g + 1, :], (GROUP_SIZE, tr)) for g in range(N_GROUPS)], axis=0)
    cand = jnp.where(esel > 0.5, sel, neg)

    picks, gates = [], []
    chosen = jnp.zeros((N_EXPERTS, tr), jnp.float32)
    for _ in range(TOP_K):
        _, a = _first_argmax_rows(cand, e_idx)
        pick = e_idx == a
        picks.append(a)
        gates.append(jnp.sum(jnp.where(pick, scores, 0.0), axis=0, keepdims=True))
        chosen = jnp.where(pick, 1.0, chosen)
        cand = jnp.where(pick, neg, cand)
    gate = jnp.concatenate(gates, axis=0)
    gate = gate / jnp.sum(gate, axis=0, keepdims=True) * ROUTE_SCALE

    r_idx = lax.broadcasted_iota(jnp.int32, (tr, tr), 0)
    c_idx = lax.broadcasted_iota(jnp.int32, (tr, tr), 1)
    upper = jnp.where(r_idx <= c_idx, 1.0, 0.0).astype(jnp.bfloat16)
    incl = jnp.dot(chosen.astype(jnp.bfloat16), upper, preferred_element_type=jnp.float32)
    rank_all = carry_ref[...] + incl - chosen
    carry_ref[...] = carry_ref[...] + incl[:, tr - 1:tr]
    ranks = [jnp.sum(jnp.where(e_idx == a, rank_all, 0.0), axis=0, keepdims=True) for a in picks]

    eidx_ref[...] = jnp.concatenate(picks, axis=0)
    rank_ref[...] = jnp.concatenate(ranks, axis=0).astype(jnp.int32)
    pad = jnp.zeros((LANES - TOP_K, tr), jnp.float32)
    gate_t_ref[...] = jnp.concatenate([gate, pad], axis=0).T
    cnt_ref[...] = jnp.broadcast_to(carry_ref[...], (N_EXPERTS, LANES)).astype(jnp.int32)


def _router(x1, w_r, r_bias, tr):
    n_tok, d = x1.shape
    wt = w_r.T
    wh = wt.astype(jnp.bfloat16)
    wl = (wt - wh.astype(jnp.float32)).astype(jnp.bfloat16)
    col = lambda i: (0, i)
    return pl.pallas_call(
        functools.partial(_router_kernel, tr=tr),
        grid=(n_tok // tr,),
        in_specs=[
            pl.BlockSpec((tr, d), lambda i: (i, 0)),
            _const_spec((N_EXPERTS, d)),
            _const_spec((N_EXPERTS, d)),
            _const_spec((N_EXPERTS, 1)),
        ],
        out_specs=[
            pl.BlockSpec((TOP_K, tr), col),
            pl.BlockSpec((tr, LANES), lambda i: (i, 0)),
            pl.BlockSpec((TOP_K, tr), col),
            pl.BlockSpec((N_EXPERTS, LANES), lambda i: (0, 0)),
        ],
        out_shape=(
            jax.ShapeDtypeStruct((TOP_K, n_tok), jnp.int32),
            jax.ShapeDtypeStruct((n_tok, LANES), jnp.float32),
            jax.ShapeDtypeStruct((TOP_K, n_tok), jnp.int32),
            jax.ShapeDtypeStruct((N_EXPERTS, LANES), jnp.int32),
        ),
        scratch_shapes=[pltpu.VMEM((N_EXPERTS, 1), jnp.float32)],
        compiler_params=_params(("arbitrary",), 32 << 20),
        name="router",
    )(x1, wh, wl, r_bias.reshape(N_EXPERTS, 1))


def _dest_kernel(pstart_ref, eidx_ref, rank_ref, dest_ref):
    e = eidx_ref[...]
    base = jnp.zeros(e.shape, jnp.int32)
    for k in range(N_EXPERTS):
        base = jnp.where(e == k, pstart_ref[k], base)
    dest_ref[...] = base + rank_ref[...]


def _dest_rows(pstart, eidx, rank, tc):
    n_tok = eidx.shape[1]
    col = lambda i, ps: (0, i)
    return pl.pallas_call(
        _dest_kernel,
        grid_spec=pltpu.PrefetchScalarGridSpec(
            num_scalar_prefetch=1,
            grid=(n_tok // tc,),
            in_specs=[pl.BlockSpec((TOP_K, tc), col), pl.BlockSpec((TOP_K, tc), col)],
            out_specs=pl.BlockSpec((TOP_K, tc), col),
        ),
        out_shape=jax.ShapeDtypeStruct((TOP_K, n_tok), jnp.int32),
        compiler_params=_params(("arbitrary",), 32 << 20),
        name="dest_rows",
    )(pstart, eidx, rank)


def _dispatch(dest, x1p, n_rows):
    n_tok, width = x1p.shape
    c = SC_DISPATCH_CHUNK
    n_workers = SC_CORES_V7X * SC_SUBCORES_V7X
    n_chunks = n_tok // c
    assert n_tok % c == 0 and n_chunks % n_workers == 0
    per_worker = n_chunks // n_workers
    idx = dest.reshape(TOP_K, n_chunks, c).transpose(1, 0, 2)
    mesh = plsc.VectorSubcoreMesh(core_axis_name="c", subcore_axis_name="s")

    @functools.partial(
        pl.kernel, mesh=mesh,
        out_type=jax.ShapeDtypeStruct((n_rows, width), x1p.dtype),
        scratch_types=[pltpu.VMEM((TOP_K, c), jnp.int32), pltpu.VMEM((c, width), x1p.dtype), pltpu.SemaphoreType.DMA],
        name="dispatch",
    )
    def scatter_rows(idx_hbm, x_hbm, xs_hbm, idx_v, rows_v, sem):
        worker = lax.axis_index("s") * SC_CORES_V7X + lax.axis_index("c")

        @pl.loop(0, per_worker)
        def _(g):
            chunk = worker * per_worker + g
            pltpu.sync_copy(idx_hbm.at[chunk], idx_v)
            pltpu.sync_copy(x_hbm.at[pl.ds(chunk * c, c)], rows_v)
            copies = [pltpu.make_async_copy(rows_v, xs_hbm.at[idx_v.at[j]], sem) for j in range(TOP_K)]
            for cp in copies:
                cp.start()
            for cp in copies:
                cp.wait()

    return scatter_rows(idx, x1p)


def _experts_kernel(be_ref, nv_ref, rows_ref, xs_ref, w1_ref, w3_ref, w2_ref, ys_ref, xb_ref, w1b_ref, w3b_ref,
                    w2b_ref):
    b = pl.program_id(0)

    @pl.when(b < nv_ref[0])
    def _():
        @pl.when((b == 0) | (be_ref[b] != be_ref[jnp.maximum(b - 1, 0)]))
        def _():
            w1b_ref[...] = w1_ref[0].astype(jnp.bfloat16)
            w3b_ref[...] = w3_ref[0].astype(jnp.bfloat16)
            w2b_ref[...] = w2_ref[0].astype(jnp.bfloat16)

        bm, half = xs_ref.shape
        routed = lax.broadcasted_iota(jnp.int32, (bm, half), 0) < rows_ref[b]
        lo, hi = _unpack_bf16_pairs(jnp.where(routed, xs_ref[...], jnp.uint32(0)))
        xb_ref[:, :half] = lo
        xb_ref[:, half:] = hi
        xb = xb_ref[...]
        h1 = jnp.dot(xb, w1b_ref[...], preferred_element_type=jnp.float32)
        h3 = jnp.dot(xb, w3b_ref[...], preferred_element_type=jnp.float32)
        h = (_silu(h1) * h3).astype(jnp.bfloat16)
        ys_ref[...] = _pack_bf16_pairs(jnp.dot(h, w2b_ref[...], preferred_element_type=jnp.float32))

    @pl.when(b >= nv_ref[0])
    def _():
        ys_ref[...] = jnp.zeros_like(ys_ref)


def _experts(blk_exp, n_valid, blk_rows, xs, w1, w3, w2, layer, bm):
    n_rows, half = xs.shape
    d = 2 * half
    de = w1.shape[3]
    n_blk = n_rows // bm
    blk = lambda b, be, nv, br: (jnp.minimum(b, nv[0] - 1), 0)
    wsel = lambda b, be, nv, br: (layer, be[jnp.minimum(b, nv[0] - 1)], 0, 0)
    vmem = (2 * bm * half * 4 + bm * d * 2 + 2 * 3 * d * de * 4 + 3 * d * de * 2 + 2 * bm * half * 4
            + 4 * bm * de * 4 + 2 * bm * d * 4)
    return pl.pallas_call(
        _experts_kernel,
        grid_spec=pltpu.PrefetchScalarGridSpec(
            num_scalar_prefetch=3,
            grid=(n_blk,),
            in_specs=[
                pl.BlockSpec((bm, half), blk),
                pl.BlockSpec((None, 1, d, de), wsel),
                pl.BlockSpec((None, 1, d, de), wsel),
                pl.BlockSpec((None, 1, de, d), wsel),
            ],
            out_specs=pl.BlockSpec((bm, half), lambda b, be, nv, br: (b, 0)),
            scratch_shapes=[pltpu.VMEM((bm, d), jnp.bfloat16), pltpu.VMEM((d, de), jnp.bfloat16),
                            pltpu.VMEM((d, de), jnp.bfloat16), pltpu.VMEM((de, d), jnp.bfloat16)],
        ),
        out_shape=jax.ShapeDtypeStruct((n_rows, half), jnp.uint32),
        compiler_params=_params(("arbitrary",), vmem + (8 << 20)),
        name="experts",
    )(blk_exp, n_valid, blk_rows, xs, w1, w3, w2)


def _gather_rows(idx, ys):
    n_groups, _, c = idx.shape
    width = ys.shape[1]
    n_workers = SC_CORES_V7X * SC_SUBCORES_V7X
    assert n_groups % n_workers == 0
    per_worker = n_groups // n_workers
    mesh = plsc.VectorSubcoreMesh(core_axis_name="c", subcore_axis_name="s")

    @functools.partial(
        pl.kernel, mesh=mesh,
        out_type=jax.ShapeDtypeStruct((n_groups * TOP_K * c, width), ys.dtype),
        scratch_types=[pltpu.VMEM((TOP_K, c), jnp.int32), pltpu.VMEM((c, width), ys.dtype), pltpu.SemaphoreType.DMA],
        name="combine_gather",
    )
    def gather(idx_hbm, ys_hbm, out_hbm, idx_v, rows_v, sem):
        worker = lax.axis_index("s") * SC_CORES_V7X + lax.axis_index("c")

        @pl.loop(0, per_worker)
        def _(g):
            group = worker * per_worker + g
            pltpu.sync_copy(idx_hbm.at[group], idx_v)
            for j in range(TOP_K):
                pltpu.async_copy(ys_hbm.at[idx_v.at[j]], rows_v, sem).wait()
                pltpu.sync_copy(rows_v, out_hbm.at[pl.ds((group * TOP_K + j) * c, c)])

    return gather(idx, ys)


def _combine_kernel(*refs, tm, alpha, aliased):
    refs = refs[1:] if aliased else refs
    x1_ref, p_ref, gt_ref, w1_ref, w3_ref, w2_ref, wpg_ref, bpg_ref, wpe_ref, g_ref, b_ref, yb_ref, o_ref = refs
    x1 = x1_ref[...]
    xb = x1.astype(jnp.bfloat16)
    h1 = jnp.dot(xb, w1_ref[...], preferred_element_type=jnp.float32)
    h3 = jnp.dot(xb, w3_ref[...], preferred_element_type=jnp.float32)
    h = (_silu(h1) * h3).astype(jnp.bfloat16)
    total = alpha * x1 + jnp.dot(h, w2_ref[...], preferred_element_type=jnp.float32)
    gate = _sigmoid(jnp.dot(xb, wpg_ref[...], preferred_element_type=jnp.float32) + bpg_ref[...])
    pe = jnp.dot(p_ref[...].astype(jnp.bfloat16), wpe_ref[...], preferred_element_type=jnp.float32)
    total = total + gate * pe

    gt = gt_ref[...]
    half = yb_ref.shape[3]
    moe_lo = jnp.zeros((tm, half), jnp.float32)
    moe_hi = jnp.zeros((tm, half), jnp.float32)
    for j in range(TOP_K):
        w = yb_ref[0, j]
        g = gt[:, j:j + 1]
        moe_lo = moe_lo + g * lax.bitcast_convert_type(w << 16, jnp.float32)
        moe_hi = moe_hi + g * lax.bitcast_convert_type(w & jnp.uint32(0xFFFF0000), jnp.float32)
    total = total + jnp.concatenate([moe_lo, moe_hi], axis=1)
    o_ref[...] = _layer_norm_rows(total, g_ref[...], b_ref[...])


def _combine(x1, p2, gate_t, w_sh1, w_sh3, w_sh2, w_pg, b_pg, w_pe, ln_g, ln_b, yb, alpha, tm, first_tile, out_buf):
    n_tok, d = x1.shape
    ds = w_sh1.shape[1]
    dp = p2.shape[1]
    n_tiles = yb.shape[0]
    row = lambda i: (first_tile + i, 0)
    vmem = (2 * TOP_K * tm * (d // 2) * 4 + (3 * d * ds + d * d + dp * d) * 2 + 2 * 2 * tm * d * 4 + 2 * tm * dp * 4
            + 2 * tm * LANES * 4 + 6 * tm * d * 4)
    aliased = out_buf is not None
    in_specs = [
        pl.BlockSpec((tm, d), row),
        pl.BlockSpec((tm, dp), row),
        pl.BlockSpec((tm, LANES), row),
        _const_spec((d, ds)),
        _const_spec((d, ds)),
        _const_spec((ds, d)),
        _const_spec((d, d)),
        _const_spec((1, d)),
        _const_spec((dp, d)),
        _const_spec((1, d)),
        _const_spec((1, d)),
        pl.BlockSpec((1, TOP_K, tm, d // 2), lambda i: (i, 0, 0, 0)),
    ]
    args = [x1, p2, gate_t, w_sh1, w_sh3, w_sh2, w_pg, b_pg.reshape(1, d), w_pe, ln_g.reshape(1, d),
            ln_b.reshape(1, d), yb]
    if aliased:
        in_specs = [pl.BlockSpec(memory_space=pl.ANY)] + in_specs
        args = [out_buf] + args
    return pl.pallas_call(
        functools.partial(_combine_kernel, tm=tm, alpha=alpha, aliased=aliased),
        grid=(n_tiles,),
        in_specs=in_specs,
        out_specs=pl.BlockSpec((tm, d), row),
        out_shape=jax.ShapeDtypeStruct((n_tok, d), jnp.float32),
        input_output_aliases={0: 0} if aliased else {},
        compiler_params=_params(("arbitrary",), vmem + (8 << 20)),
        name="combine",
    )(*args)


def _tile(n, target):
    t = min(n, target)
    assert n % t == 0, (n, t)
    return t


def kernel(x, p, w_in, b_f, conv_w, conv_b, conv_ln_g, conv_ln_b, gmlp_ln_g, gmlp_ln_b, w_sp, b_sp, out_g, w_o,
           ln1_g, ln1_b, w_r, r_bias, w_e1, w_e3, w_e2, w_sh1, w_sh3, w_sh2, w_pe, w_pg, b_pg, ln2_g, ln2_b):
    bsz, seq, d = x.shape
    depth = w_in.shape[0]
    n_tok = bsz * seq
    alpha = (2.0 * depth) ** 0.25
    n_main = 2 * CONV_CH + 2 * GMLP_CH + 3 * ATT_CH
    assert w_in.shape[2] == n_main + ATT_HEADS and seq % CHUNK == 0

    bm = MOE_BLOCK
    n_blk = -(-n_tok * TOP_K // bm) + N_EXPERTS
    n_rows = n_blk * bm
    bf16 = jnp.bfloat16

    x2 = x.reshape(n_tok, d)
    for i in range(depth):
        w_main = w_in[i, :, :n_main].astype(bf16)
        w_f = jnp.pad(w_in[i, :, n_main:], ((0, 0), (0, LANES - ATT_HEADS))).astype(bf16)
        b_f_row = jnp.pad(b_f[i], (0, LANES - ATT_HEADS)).reshape(1, LANES)

        zc, zg, q, k, v, f = _in_projection(x2, w_main, w_f, _tile(n_tok, 256))
        ya = _conv_module(zc, conv_w[i], conv_b[i], conv_ln_g[i], conv_ln_b[i], out_g[i, :CONV_CH],
                          bsz, seq, _tile(seq, 256))
        yb = _gmlp_module(zg, gmlp_ln_g[i], gmlp_ln_b[i], w_sp[i], b_sp[i],
                          out_g[i, CONV_CH:CONV_CH + GMLP_CH], _tile(n_tok, 512))
        c = _forget_cumsum(f, b_f_row, bsz, seq, _tile(seq, 512))
        tq = _tile(seq, 2048)
        yc = _fox_attention(q, k, v, c, bsz, seq, tq, _tile(tq, 512), _tile(tq, 512))
        x1, x1p = _out_projection(ya, yb, yc, x2, w_o[i].astype(bf16), out_g[i, CONV_CH + GMLP_CH:],
                                  ln1_g[i], ln1_b[i], alpha, _tile(n_tok, 256))

        eidx, gate_t, rank, cnt = _router(x1, w_r[i], r_bias[i], _tile(n_tok, 512))
        counts = cnt[:, 0]
        padded = (counts + bm - 1) // bm * bm
        pend = jnp.cumsum(padded).astype(jnp.int32)
        pstart = pend - padded
        blk_start = jnp.arange(n_blk, dtype=jnp.int32) * bm
        blk_exp = jnp.minimum(jnp.sum(pend[None, :] <= blk_start[:, None], axis=1), N_EXPERTS - 1).astype(jnp.int32)
        n_valid = (pend[N_EXPERTS - 1:] // bm).astype(jnp.int32)
        blk_rows = jnp.clip((pstart + counts)[blk_exp] - blk_start, 0, bm).astype(jnp.int32)

        dest = _dest_rows(pstart, eidx, rank, _tile(n_tok, 4096))
        xs = _dispatch(dest, x1p, n_rows)
        ys = _experts(blk_exp, n_valid, blk_rows, xs, w_e1, w_e3, w_e2, i, bm)
        tm = _tile(n_tok, 256)
        n_tiles = n_tok // tm
        gather_idx = dest.reshape(TOP_K, n_tiles, tm).transpose(1, 0, 2).reshape(-1, TOP_K, SC_DISPATCH_CHUNK)
        groups = gather_idx.shape[0] // COMBINE_SPLITS
        tiles = n_tiles // COMBINE_SPLITS
        x2 = None
        for part in range(COMBINE_SPLITS):
            yb = _gather_rows(gather_idx[part * groups:(part + 1) * groups], ys).reshape(tiles, TOP_K, tm, d // 2)
            x2 = _combine(x1, p[i].reshape(n_tok, -1), gate_t, w_sh1[i].astype(bf16), w_sh3[i].astype(bf16),
                          w_sh2[i].astype(bf16), w_pg[i].astype(bf16), b_pg[i], w_pe[i].astype(bf16),
                          ln2_g[i], ln2_b[i], yb, alpha, tm, part * tiles, x2)
    return x2.reshape(bsz, seq, d)
```

```python
import functools

import jax
import jax.numpy as jnp
from jax import lax
from jax.experimental import pallas as pl
from jax.experimental.pallas import tpu as pltpu
from jax.experimental.pallas import tpu_sc as plsc

CONV_CH = 512
CONV_WIDTH = 31
GMLP_HEADS = 4
GMLP_HEAD_CH = 128
GMLP_CH = GMLP_HEADS * GMLP_HEAD_CH
CHUNK = 128
ATT_HEADS = 8
HEAD_DIM = 128
ATT_CH = ATT_HEADS * HEAD_DIM
N_EXPERTS = 64
TOP_K = 8
N_GROUPS = 8
GROUP_SIZE = N_EXPERTS // N_GROUPS
TOPK_GROUPS = 4
ROUTE_SCALE = 2.5
LN_EPS = 1e-5
LOG2E = 1.4426950408889634

LANES = 128
SUBLANES = 8
VMEM_BYTES_V7X = 64 * 1024 * 1024
SC_CORES_V7X = 2
SC_SUBCORES_V7X = 16

CONV_HALO = 32
CONV_ROWS = 32
MOE_BLOCK = 512
SC_DISPATCH_CHUNK = 64
COMBINE_SPLITS = 4


def _vmem_limit(nbytes):
    return int(min(nbytes, VMEM_BYTES_V7X - 8 * 1024 * 1024))


def _params(semantics, vmem_bytes):
    return pltpu.CompilerParams(dimension_semantics=semantics, vmem_limit_bytes=_vmem_limit(vmem_bytes))


def _const_spec(shape):
    nd = len(shape)
    return pl.BlockSpec(shape, lambda *_: (0,) * nd, pipeline_mode=pl.Buffered(1))


def _layer_norm_rows(x, g, b):
    mu = jnp.mean(x, axis=-1, keepdims=True)
    xc = x - mu
    var = jnp.mean(xc * xc, axis=-1, keepdims=True)
    return xc * lax.rsqrt(var + LN_EPS) * g + b


def _rms_rows(x):
    return x * lax.rsqrt(jnp.mean(x * x, axis=-1, keepdims=True) + LN_EPS)


def _sigmoid(x):
    return 1.0 / (1.0 + jnp.exp(-x))


def _silu(x):
    return x * _sigmoid(x)


def _inproj_kernel(x_ref, w_ref, wf_ref, zc_ref, zg_ref, q_ref, k_ref, v_ref, f_ref, *, q_scale):
    xb = x_ref[...].astype(jnp.bfloat16)
    wide = zc_ref.shape[1]

    def slab(n):
        return jnp.dot(xb, w_ref[:, n * wide:(n + 1) * wide], preferred_element_type=jnp.float32)

    f_ref[...] = jnp.dot(xb, wf_ref[...], preferred_element_type=jnp.float32)
    zc_ref[...] = slab(0)
    zg_ref[...] = slab(1)
    q_ref[...] = (slab(2) * q_scale).astype(jnp.bfloat16)
    k_ref[...] = slab(3).astype(jnp.bfloat16)
    v_ref[...] = slab(4).astype(jnp.bfloat16)


def _in_projection(x2, w_main, w_f, tm):
    n_tok, d = x2.shape
    wide = 2 * CONV_CH
    assert w_main.shape[1] == 5 * wide and ATT_CH == wide and 2 * GMLP_CH == wide
    row = lambda i: (i, 0)
    out_shape = (
        jax.ShapeDtypeStruct((n_tok, wide), jnp.float32),
        jax.ShapeDtypeStruct((n_tok, wide), jnp.float32),
        jax.ShapeDtypeStruct((n_tok, ATT_CH), jnp.bfloat16),
        jax.ShapeDtypeStruct((n_tok, ATT_CH), jnp.bfloat16),
        jax.ShapeDtypeStruct((n_tok, ATT_CH), jnp.bfloat16),
        jax.ShapeDtypeStruct((n_tok, LANES), jnp.float32),
    )
    vmem = (2 * tm * d * 4 + tm * d * 2 + d * 5 * wide * 2 + d * LANES * 2
            + 2 * 2 * tm * wide * 4 + 3 * 2 * tm * wide * 2 + 2 * tm * LANES * 4 + 2 * tm * wide * 4)
    return pl.pallas_call(
        functools.partial(_inproj_kernel, q_scale=HEAD_DIM ** -0.5 * LOG2E),
        grid=(n_tok // tm,),
        in_specs=[
            pl.BlockSpec((tm, d), row),
            _const_spec((d, 5 * wide)),
            _const_spec((d, LANES)),
        ],
        out_specs=[
            pl.BlockSpec((tm, wide), row),
            pl.BlockSpec((tm, wide), row),
            pl.BlockSpec((tm, ATT_CH), row),
            pl.BlockSpec((tm, ATT_CH), row),
            pl.BlockSpec((tm, ATT_CH), row),
            pl.BlockSpec((tm, LANES), row),
        ],
        out_shape=out_shape,
        compiler_params=_params(("arbitrary",), vmem + (8 << 20)),
        name="in_projection",
    )(x2, w_main, w_f)


def _conv_kernel(z_ref, w_ref, cb_ref, lng_ref, lnb_ref, og_ref, o_ref, hbuf_ref, sh_ref, *, ts):
    s = pl.program_id(1)
    span = CONV_HALO + ts - SUBLANES

    @pl.when(s == 0)
    def _():
        hbuf_ref[0:CONV_HALO, :] = jnp.zeros((CONV_HALO, CONV_CH), jnp.float32)

    @pl.when(s > 0)
    def _():
        hbuf_ref[0:CONV_HALO, :] = hbuf_ref[ts:ts + CONV_HALO, :]

    hbuf_ref[CONV_HALO:CONV_HALO + ts, :] = z_ref[:, :CONV_CH] * _sigmoid(z_ref[:, CONV_CH:])

    cb = cb_ref[...]
    lng = lng_ref[...]
    lnb = lnb_ref[...]
    og = og_ref[...]
    for r in range(1, SUBLANES):
        sh_ref[r - 1] = hbuf_ref[r:r + span, :]
    first = CONV_HALO - (CONV_WIDTH - 1)
    for c in range(ts // CONV_ROWS):
        acc = jnp.broadcast_to(cb, (CONV_ROWS, CONV_CH))
        for j in range(CONV_WIDTH):
            off = first + j + c * CONV_ROWS
            r = off % SUBLANES
            a = off - r
            rows = hbuf_ref[a:a + CONV_ROWS, :] if r == 0 else sh_ref[r - 1, a:a + CONV_ROWS, :]
            acc = acc + w_ref[j:j + 1, :] * rows
        y = _silu(_layer_norm_rows(acc, lng, lnb))
        o_ref[c * CONV_ROWS:(c + 1) * CONV_ROWS, :] = (_rms_rows(y) * og).astype(jnp.bfloat16)


def _conv_module(zc, conv_w, conv_b, ln_g, ln_b, og, bsz, seq, ts):
    n_tok = zc.shape[0]
    n_s = seq // ts
    vec = lambda v: v.reshape(1, CONV_CH)
    return pl.pallas_call(
        functools.partial(_conv_kernel, ts=ts),
        grid=(bsz, n_s),
        in_specs=[
            pl.BlockSpec((ts, 2 * CONV_CH), lambda b, s: (b * n_s + s, 0)),
            _const_spec((CONV_WIDTH, CONV_CH)),
            _const_spec((1, CONV_CH)),
            _const_spec((1, CONV_CH)),
            _const_spec((1, CONV_CH)),
            _const_spec((1, CONV_CH)),
        ],
        out_specs=pl.BlockSpec((ts, CONV_CH), lambda b, s: (b * n_s + s, 0)),
        out_shape=jax.ShapeDtypeStruct((n_tok, CONV_CH), jnp.bfloat16),
        scratch_shapes=[pltpu.VMEM((CONV_HALO + ts, CONV_CH), jnp.float32),
                        pltpu.VMEM((SUBLANES - 1, CONV_HALO + ts - SUBLANES, CONV_CH), jnp.float32)],
        compiler_params=_params(("arbitrary", "arbitrary"), 32 << 20),
        name="conv_module",
    )(zc, conv_w, vec(conv_b), vec(ln_g), vec(ln_b), vec(og))


def _gelu_tanh(x):
    c = (2.0 / jnp.pi) ** 0.5
    return 0.5 * x * (1.0 + jnp.tanh(c * (x + 0.044715 * (x * x * x))))


def _gmlp_kernel(z_ref, lng_ref, lnb_ref, wsp_ref, bsp_ref, og_ref, o_ref, y_ref, *, tg):
    z = _gelu_tanh(z_ref[...])
    u = z[:, :GMLP_CH]
    v = _layer_norm_rows(z[:, GMLP_CH:], lng_ref[...], lnb_ref[...]).astype(jnp.bfloat16)
    t_idx = lax.broadcasted_iota(jnp.int32, (CHUNK, CHUNK), 0)
    s_idx = lax.broadcasted_iota(jnp.int32, (CHUNK, CHUNK), 1)
    causal = s_idx <= t_idx
    for h in range(GMLP_HEADS):
        ws = jnp.where(causal, wsp_ref[h], 0.0).astype(jnp.bfloat16)
        bias = bsp_ref[h]
        cols = slice(h * GMLP_HEAD_CH, (h + 1) * GMLP_HEAD_CH)
        for c in range(tg // CHUNK):
            rows = slice(c * CHUNK, (c + 1) * CHUNK)
            mixed = jnp.dot(ws, v[rows, cols], preferred_element_type=jnp.float32) + bias
            y_ref[rows, cols] = u[rows, cols] * mixed
    o_ref[...] = (_rms_rows(y_ref[...]) * og_ref[...]).astype(jnp.bfloat16)


def _gmlp_module(zg, ln_g, ln_b, w_sp, b_sp, og, tg):
    n_tok = zg.shape[0]
    vec = lambda v: v.reshape(1, GMLP_CH)
    return pl.pallas_call(
        functools.partial(_gmlp_kernel, tg=tg),
        grid=(n_tok // tg,),
        in_specs=[
            pl.BlockSpec((tg, 2 * GMLP_CH), lambda i: (i, 0)),
            _const_spec((1, GMLP_CH)),
            _const_spec((1, GMLP_CH)),
            _const_spec((GMLP_HEADS, CHUNK, CHUNK)),
            _const_spec((GMLP_HEADS, CHUNK, 1)),
            _const_spec((1, GMLP_CH)),
        ],
        out_specs=pl.BlockSpec((tg, GMLP_CH), lambda i: (i, 0)),
        out_shape=jax.ShapeDtypeStruct((n_tok, GMLP_CH), jnp.bfloat16),
        scratch_shapes=[pltpu.VMEM((tg, GMLP_CH), jnp.float32)],
        compiler_params=_params(("arbitrary",), 32 << 20),
        name="gmlp_module",
    )(zg, vec(ln_g), vec(ln_b), w_sp, b_sp.reshape(GMLP_HEADS, CHUNK, 1), vec(og))


def _split3(x):
    hi = x.astype(jnp.bfloat16)
    r1 = x - hi.astype(jnp.float32)
    mid = r1.astype(jnp.bfloat16)
    lo = (r1 - mid.astype(jnp.float32)).astype(jnp.bfloat16)
    return hi, mid, lo


def _fcum_kernel(f_ref, bf_ref, c_ref, carry_ref, *, ts):
    s = pl.program_id(1)

    @pl.when(s == 0)
    def _():
        carry_ref[...] = jnp.zeros_like(carry_ref)

    x = f_ref[...] + bf_ref[...]
    ls = -(jnp.maximum(-x, 0.0) + jnp.log(1.0 + jnp.exp(-jnp.abs(x))))
    t_idx = lax.broadcasted_iota(jnp.int32, (ts, ts), 0)
    s_idx = lax.broadcasted_iota(jnp.int32, (ts, ts), 1)
    tri = jnp.where(s_idx <= t_idx, 1.0, 0.0).astype(jnp.bfloat16)
    hi, mid, lo = _split3(ls)
    c = (jnp.dot(tri, lo, preferred_element_type=jnp.float32)
         + jnp.dot(tri, mid, preferred_element_type=jnp.float32)
         + jnp.dot(tri, hi, preferred_element_type=jnp.float32)) + carry_ref[...]
    c_ref[...] = c * LOG2E
    carry_ref[...] = c[ts - 1:ts, :]


def _forget_cumsum(f, b_f_row, bsz, seq, ts):
    n_tok = f.shape[0]
    n_s = seq // ts
    return pl.pallas_call(
        functools.partial(_fcum_kernel, ts=ts),
        grid=(bsz, n_s),
        in_specs=[
            pl.BlockSpec((ts, LANES), lambda b, s: (b * n_s + s, 0)),
            _const_spec((1, LANES)),
        ],
        out_specs=pl.BlockSpec((ts, LANES), lambda b, s: (b * n_s + s, 0)),
        out_shape=jax.ShapeDtypeStruct((n_tok, LANES), jnp.float32),
        scratch_shapes=[pltpu.VMEM((1, LANES), jnp.float32)],
        compiler_params=_params(("arbitrary", "arbitrary"), 32 << 20),
        name="forget_cumsum",
    )(f, b_f_row)


def _bias_columns(c_tile, h, as_query):
    rows = c_tile.shape[0]
    lane = lax.broadcasted_iota(jnp.int32, (rows, LANES), 1)
    col = jnp.sum(jnp.where(lane == h, c_tile, 0.0), axis=1, keepdims=True)
    val = jnp.broadcast_to(col if as_query else -col, (rows, LANES))
    hi, mid, lo = (t.astype(jnp.float32) for t in _split3(val))
    base = 0 if as_query else 3
    out = jnp.where(lane < 6, 1.0, 0.0)
    for offset, term in enumerate((hi, mid, lo)):
        out = jnp.where(lane == base + offset, term, out)
    return out.astype(jnp.bfloat16)


def _fox_kernel(q_ref, k_ref, v_ref, cq_ref, ck_ref, o_ref, kaug_ref, vt_ref, qaug_ref, m_ref, l_ref, acc_ref, s_ref,
                *, tq, tk, cw, seq):
    h = pl.program_id(1)
    i = pl.program_id(2)
    n_chain = tq // cw

    @pl.when(i == 0)
    def _():
        for r in range(seq // tk):
            rows = slice(r * tk, (r + 1) * tk)
            kaug_ref[rows, :HEAD_DIM] = k_ref[rows, :]
            kaug_ref[rows, HEAD_DIM:] = _bias_columns(ck_ref[rows, :], h, as_query=False)
            vt_ref[:, rows] = v_ref[rows, :].astype(jnp.float32).T.astype(jnp.bfloat16)

    qaug_ref[:, :HEAD_DIM] = q_ref[...]
    qaug_ref[:, HEAD_DIM:] = _bias_columns(cq_ref[...], h, as_query=True)
    m_ref[...] = jnp.full(m_ref.shape, -jnp.inf, jnp.float32)
    l_ref[...] = jnp.zeros_like(l_ref)
    acc_ref[...] = jnp.zeros_like(acc_ref)

    def scores(slot, c, k0):
        s_ref[slot] = lax.dot_general(kaug_ref[pl.ds(k0, tk), :], qaug_ref[c * cw:(c + 1) * cw, :],
                                      (((1,), (1,)), ((), ())), preferred_element_type=jnp.float32)

    def absorb(slot, c, k0, key_minus_query):
        s = s_ref[slot]
        if key_minus_query is not None:
            key = lax.broadcasted_iota(jnp.int32, (tk, cw), 0) + key_minus_query
            qry = lax.broadcasted_iota(jnp.int32, (tk, cw), 1)
            s = jnp.where(key <= qry, s, -jnp.inf)
        m_prev = m_ref[c]
        m_new = jnp.maximum(m_prev, jnp.max(s, axis=0, keepdims=True))
        alpha = jnp.exp2(m_prev - m_new)
        p = jnp.exp2(s - m_new)
        l_ref[c] = alpha * l_ref[c] + jnp.sum(p, axis=0, keepdims=True)
        pv = jnp.dot(vt_ref[:, pl.ds(k0, tk)], p.astype(jnp.bfloat16), preferred_element_type=jnp.float32)
        acc_ref[c] = alpha * acc_ref[c] + pv
        m_ref[c] = m_new

    n_kb = tq // tk

    def query_tile_keys(base, masks):
        work = [(kb, c) for kb in range(n_kb) for c in range(n_chain) if masks[kb][c] is not False]
        for slot, (kb, c) in enumerate(work):
            scores(slot, c, pl.multiple_of(base + kb * tk, tk))
        for slot, (kb, c) in enumerate(work):
            absorb(slot, c, pl.multiple_of(base + kb * tk, tk), masks[kb][c])

    def body(j, carry):
        query_tile_keys(j * tq, [[None] * n_chain] * n_kb)
        return carry

    lax.fori_loop(0, i, body, 0)
    masks = []
    for kb in range(n_kb):
        row = []
        for c in range(n_chain):
            first_key, first_query = kb * tk, c * cw
            if first_key + tk - 1 <= first_query:
                row.append(None)
            elif first_key <= first_query + cw - 1:
                row.append(first_key - first_query)
            else:
                row.append(False)
        masks.append(row)
    query_tile_keys(i * tq, masks)
    for c in range(n_chain):
        o_ref[c * cw:(c + 1) * cw, :] = (acc_ref[c] / l_ref[c]).T.astype(jnp.bfloat16)


def _fox_attention(q, k, v, c, bsz, seq, tq, tk, cw):
    n_tok = q.shape[0]
    nq = seq // tq
    n_chain = tq // cw
    return pl.pallas_call(
        functools.partial(_fox_kernel, tq=tq, tk=tk, cw=cw, seq=seq),
        grid=(bsz, ATT_HEADS, nq),
        in_specs=[
            pl.BlockSpec((tq, HEAD_DIM), lambda b, h, i: (b * nq + i, h)),
            pl.BlockSpec((seq, HEAD_DIM), lambda b, h, i: (b, h)),
            pl.BlockSpec((seq, HEAD_DIM), lambda b, h, i: (b, h)),
            pl.BlockSpec((tq, LANES), lambda b, h, i: (b * nq + i, 0)),
            pl.BlockSpec((seq, LANES), lambda b, h, i: (b, 0)),
        ],
        out_specs=pl.BlockSpec((tq, HEAD_DIM), lambda b, h, i: (b * nq + i, h)),
        out_shape=jax.ShapeDtypeStruct((n_tok, ATT_CH), jnp.bfloat16),
        scratch_shapes=[
            pltpu.VMEM((seq, 2 * HEAD_DIM), jnp.bfloat16),
            pltpu.VMEM((HEAD_DIM, seq), jnp.bfloat16),
            pltpu.VMEM((tq, 2 * HEAD_DIM), jnp.bfloat16),
            pltpu.VMEM((n_chain, 1, cw), jnp.float32),
            pltpu.VMEM((n_chain, 1, cw), jnp.float32),
            pltpu.VMEM((n_chain, HEAD_DIM, cw), jnp.float32),
            pltpu.VMEM((n_chain * (tq // tk), tk, cw), jnp.float32),
        ],
        compiler_params=_params(("arbitrary", "arbitrary", "arbitrary"), 56 << 20),
        name="fox_attention",
    )(q, k, v, c, c)


def _pack_bf16_pairs(x):
    n = x.shape[1] // 2
    r = x.astype(jnp.bfloat16).astype(jnp.float32)
    lo = lax.bitcast_convert_type(r[:, :n], jnp.uint32)
    hi = lax.bitcast_convert_type(r[:, n:], jnp.uint32)
    return (lo >> 16) | (hi & jnp.uint32(0xFFFF0000))


def _unpack_bf16_pairs(w):
    lo = lax.bitcast_convert_type(w << 16, jnp.float32).astype(jnp.bfloat16)
    hi = lax.bitcast_convert_type(w & jnp.uint32(0xFFFF0000), jnp.float32).astype(jnp.bfloat16)
    return lo, hi


def _outproj_kernel(ya_ref, yb_ref, yc_ref, x_ref, wo_ref, ogc_ref, g_ref, b_ref, x1_ref, x1p_ref, y_ref, *, alpha):
    y_ref[:, 0:CONV_CH] = ya_ref[...]
    y_ref[:, CONV_CH:CONV_CH + GMLP_CH] = yb_ref[...]
    yc = yc_ref[...].astype(jnp.float32)
    y_ref[:, CONV_CH + GMLP_CH:] = (_rms_rows(yc) * ogc_ref[...]).astype(jnp.bfloat16)
    h = jnp.dot(y_ref[...], wo_ref[...], preferred_element_type=jnp.float32)
    x1 = _layer_norm_rows(alpha * x_ref[...] + h, g_ref[...], b_ref[...])
    x1_ref[...] = x1
    x1p_ref[...] = _pack_bf16_pairs(x1)


def _out_projection(ya, yb, yc, x2, w_o, og_c, ln_g, ln_b, alpha, tm):
    n_tok, d = x2.shape
    mix = w_o.shape[0]
    row = lambda i: (i, 0)
    vmem = (mix * d * 2 + 2 * tm * d * 4 * 2 + 2 * tm * (d // 2) * 4 + 2 * tm * mix * 2 + tm * mix * 2 + 3 * tm * d * 4)
    return pl.pallas_call(
        functools.partial(_outproj_kernel, alpha=alpha),
        grid=(n_tok // tm,),
        in_specs=[
            pl.BlockSpec((tm, CONV_CH), row),
            pl.BlockSpec((tm, GMLP_CH), row),
            pl.BlockSpec((tm, ATT_CH), row),
            pl.BlockSpec((tm, d), row),
            _const_spec((mix, d)),
            _const_spec((1, ATT_CH)),
            _const_spec((1, d)),
            _const_spec((1, d)),
        ],
        out_specs=[pl.BlockSpec((tm, d), row), pl.BlockSpec((tm, d // 2), row)],
        out_shape=(
            jax.ShapeDtypeStruct((n_tok, d), jnp.float32),
            jax.ShapeDtypeStruct((n_tok, d // 2), jnp.uint32),
        ),
        scratch_shapes=[pltpu.VMEM((tm, mix), jnp.bfloat16)],
        compiler_params=_params(("arbitrary",), vmem + (8 << 20)),
        name="out_projection",
    )(ya, yb, yc, x2, w_o, og_c.reshape(1, ATT_CH), ln_g.reshape(1, d), ln_b.reshape(1, d))


def _first_argmax_rows(x, idx):
    m = jnp.max(x, axis=0, keepdims=True)
    first = jnp.min(jnp.where(x == m, idx, x.shape[0]), axis=0, keepdims=True)
    return m, first


def _router_kernel(x_ref, wh_ref, wl_ref, rb_ref, eidx_ref, gate_t_ref, rank_ref, cnt_ref, carry_ref, *, tr):
    i = pl.program_id(0)

    @pl.when(i == 0)
    def _():
        carry_ref[...] = jnp.zeros_like(carry_ref)

    x = x_ref[...]
    xh = x.astype(jnp.bfloat16)
    xl = (x - xh.astype(jnp.float32)).astype(jnp.bfloat16)
    nt = (((1,), (1,)), ((), ()))
    wh = wh_ref[...]
    logits = (lax.dot_general(wl_ref[...], xh, nt, preferred_element_type=jnp.float32)
              + lax.dot_general(wh, xl, nt, preferred_element_type=jnp.float32)
              + lax.dot_general(wh, xh, nt, preferred_element_type=jnp.float32))
    scores = _sigmoid(logits)
    sel = scores + rb_ref[...]

    neg = -jnp.inf
    e_idx = lax.broadcasted_iota(jnp.int32, (N_EXPERTS, tr), 0)
    g_idx = lax.broadcasted_iota(jnp.int32, (N_GROUPS, tr), 0)
    in_idx = lax.broadcasted_iota(jnp.int32, (GROUP_SIZE, tr), 0)

    gs_rows = []
    for g in range(N_GROUPS):
        blk = sel[g * GROUP_SIZE:(g + 1) * GROUP_SIZE, :]
        m1, a1 = _first_argmax_rows(blk, in_idx)
        m2 = jnp.max(jnp.where(in_idx == a1, neg, blk), axis=0, keepdims=True)
        gs_rows.append(m1 + m2)
    gs = jnp.concatenate(gs_rows, axis=0)

    gsel = jnp.zeros((N_GROUPS, tr), jnp.float32)
    for _ in range(TOPK_GROUPS):
        _, a = _first_argmax_rows(gs, g_idx)
        pick = g_idx == a
        gsel = jnp.where(pick, 1.0, gsel)
        gs = jnp.where(pick, neg, gs)
    esel = jnp.concatenate(
        [jnp.broadcast_to(gsel[g:g + 1, :], (GROUP_SIZE, tr)) for g in range(N_GROUPS)], axis=0)
    cand = jnp.where(esel > 0.5, sel, neg)

    picks, gates = [], []
    chosen = jnp.zeros((N_EXPERTS, tr), jnp.float32)
    for _ in range(TOP_K):
        _, a = _first_argmax_rows(cand, e_idx)
        pick = e_idx == a
        picks.append(a)
        gates.append(jnp.sum(jnp.where(pick, scores, 0.0), axis=0, keepdims=True))
        chosen = jnp.where(pick, 1.0, chosen)
        cand = jnp.where(pick, neg, cand)
    gate = jnp.concatenate(gates, axis=0)
    gate = gate / jnp.sum(gate, axis=0, keepdims=True) * ROUTE_SCALE

    r_idx = lax.broadcasted_iota(jnp.int32, (tr, tr), 0)
    c_idx = lax.broadcasted_iota(jnp.int32, (tr, tr), 1)
    upper = jnp.where(r_idx <= c_idx, 1.0, 0.0).astype(jnp.bfloat16)
    incl = jnp.dot(chosen.astype(jnp.bfloat16), upper, preferred_element_type=jnp.float32)
    rank_all = carry_ref[...] + incl - chosen
    carry_ref[...] = carry_ref[...] + incl[:, tr - 1:tr]
    ranks = [jnp.sum(jnp.where(e_idx == a, rank_all, 0.0), axis=0, keepdims=True) for a in picks]

    eidx_ref[...] = jnp.concatenate(picks, axis=0)
    rank_ref[...] = jnp.concatenate(ranks, axis=0).astype(jnp.int32)
    pad = jnp.zeros((LANES - TOP_K, tr), jnp.float32)
    gate_t_ref[...] = jnp.concatenate([gate, pad], axis=0).T
    cnt_ref[...] = jnp.broadcast_to(carry_ref[...], (N_EXPERTS, LANES)).astype(jnp.int32)


def _router(x1, w_r, r_bias, tr):
    n_tok, d = x1.shape
    wt = w_r.T
    wh = wt.astype(jnp.bfloat16)
    wl = (wt - wh.astype(jnp.float32)).astype(jnp.bfloat16)
    col = lambda i: (0, i)
    return pl.pallas_call(
        functools.partial(_router_kernel, tr=tr),
        grid=(n_tok // tr,),
        in_specs=[
            pl.BlockSpec((tr, d), lambda i: (i, 0)),
            _const_spec((N_EXPERTS, d)),
            _const_spec((N_EXPERTS, d)),
            _const_spec((N_EXPERTS, 1)),
        ],
        out_specs=[
            pl.BlockSpec((TOP_K, tr), col),
            pl.BlockSpec((tr, LANES), lambda i: (i, 0)),
            pl.BlockSpec((TOP_K, tr), col),
            pl.BlockSpec((N_EXPERTS, LANES), lambda i: (0, 0)),
        ],
        out_shape=(
            jax.ShapeDtypeStruct((TOP_K, n_tok), jnp.int32),
            jax.ShapeDtypeStruct((n_tok, LANES), jnp.float32),
            jax.ShapeDtypeStruct((TOP_K, n_tok), jnp.int32),
            jax.ShapeDtypeStruct((N_EXPERTS, LANES), jnp.int32),
        ),
        scratch_shapes=[pltpu.VMEM((N_EXPERTS, 1), jnp.float32)],
        compiler_params=_params(("arbitrary",), 32 << 20),
        name="router",
    )(x1, wh, wl, r_bias.reshape(N_EXPERTS, 1))


def _dest_kernel(pstart_ref, eidx_ref, rank_ref, dest_ref):
    e = eidx_ref[...]
    base = jnp.zeros(e.shape, jnp.int32)
    for k in range(N_EXPERTS):
        base = jnp.where(e == k, pstart_ref[k], base)
    dest_ref[...] = base + rank_ref[...]


def _dest_rows(pstart, eidx, rank, tc):
    n_tok = eidx.shape[1]
    col = lambda i, ps: (0, i)
    return pl.pallas_call(
        _dest_kernel,
        grid_spec=pltpu.PrefetchScalarGridSpec(
            num_scalar_prefetch=1,
            grid=(n_tok // tc,),
            in_specs=[pl.BlockSpec((TOP_K, tc), col), pl.BlockSpec((TOP_K, tc), col)],
            out_specs=pl.BlockSpec((TOP_K, tc), col),
        ),
        out_shape=jax.ShapeDtypeStruct((TOP_K, n_tok), jnp.int32),
        compiler_params=_params(("arbitrary",), 32 << 20),
        name="dest_rows",
    )(pstart, eidx, rank)


def _dispatch(dest, x1p, n_rows):
    n_tok, width = x1p.shape
    c = SC_DISPATCH_CHUNK
    n_workers = SC_CORES_V7X * SC_SUBCORES_V7X
    n_chunks = n_tok // c
    assert n_tok % c == 0 and n_chunks % n_workers == 0
    per_worker = n_chunks // n_workers
    idx = dest.reshape(TOP_K, n_chunks, c).transpose(1, 0, 2)
    mesh = plsc.VectorSubcoreMesh(core_axis_name="c", subcore_axis_name="s")

    @functools.partial(
        pl.kernel, mesh=mesh,
        out_type=jax.ShapeDtypeStruct((n_rows, width), x1p.dtype),
        scratch_types=[pltpu.VMEM((TOP_K, c), jnp.int32), pltpu.VMEM((c, width), x1p.dtype), pltpu.SemaphoreType.DMA],
        name="dispatch",
    )
    def scatter_rows(idx_hbm, x_hbm, xs_hbm, idx_v, rows_v, sem):
        worker = lax.axis_index("s") * SC_CORES_V7X + lax.axis_index("c")

        @pl.loop(0, per_worker)
        def _(g):
            chunk = worker * per_worker + g
            pltpu.sync_copy(idx_hbm.at[chunk], idx_v)
            pltpu.sync_copy(x_hbm.at[pl.ds(chunk * c, c)], rows_v)
            copies = [pltpu.make_async_copy(rows_v, xs_hbm.at[idx_v.at[j]], sem) for j in range(TOP_K)]
            for cp in copies:
                cp.start()
            for cp in copies:
                cp.wait()

    return scatter_rows(idx, x1p)


def _experts_kernel(be_ref, nv_ref, rows_ref, xs_ref, w1_ref, w3_ref, w2_ref, ys_ref, xb_ref, w1b_ref, w3b_ref,
                    w2b_ref):
    b = pl.program_id(0)

    @pl.when(b < nv_ref[0])
    def _():
        @pl.when((b == 0) | (be_ref[b] != be_ref[jnp.maximum(b - 1, 0)]))
        def _():
            w1b_ref[...] = w1_ref[0].astype(jnp.bfloat16)
            w3b_ref[...] = w3_ref[0].astype(jnp.bfloat16)
            w2b_ref[...] = w2_ref[0].astype(jnp.bfloat16)

        bm, half = xs_ref.shape
        routed = lax.broadcasted_iota(jnp.int32, (bm, half), 0) < rows_ref[b]
        lo, hi = _unpack_bf16_pairs(jnp.where(routed, xs_ref[...], jnp.uint32(0)))
        xb_ref[:, :half] = lo
        xb_ref[:, half:] = hi
        xb = xb_ref[...]
        h1 = jnp.dot(xb, w1b_ref[...], preferred_element_type=jnp.float32)
        h3 = jnp.dot(xb, w3b_ref[...], preferred_element_type=jnp.float32)
        h = (_silu(h1) * h3).astype(jnp.bfloat16)
        ys_ref[...] = _pack_bf16_pairs(jnp.dot(h, w2b_ref[...], preferred_element_type=jnp.float32))

    @pl.when(b >= nv_ref[0])
    def _():
        ys_ref[...] = jnp.zeros_like(ys_ref)


def _experts(blk_exp, n_valid, blk_rows, xs, w1, w3, w2, layer, bm):
    n_rows, half = xs.shape
    d = 2 * half
    de = w1.shape[3]
    n_blk = n_rows // bm
    blk = lambda b, be, nv, br: (jnp.minimum(b, nv[0] - 1), 0)
    wsel = lambda b, be, nv, br: (layer, be[jnp.minimum(b, nv[0] - 1)], 0, 0)
    vmem = (2 * bm * half * 4 + bm * d * 2 + 2 * 3 * d * de * 4 + 3 * d * de * 2 + 2 * bm * half * 4
            + 4 * bm * de * 4 + 2 * bm * d * 4)
    return pl.pallas_call(
        _experts_kernel,
        grid_spec=pltpu.PrefetchScalarGridSpec(
            num_scalar_prefetch=3,
            grid=(n_blk,),
            in_specs=[
                pl.BlockSpec((bm, half), blk),
                pl.BlockSpec((None, 1, d, de), wsel),
                pl.BlockSpec((None, 1, d, de), wsel),
                pl.BlockSpec((None, 1, de, d), wsel),
            ],
            out_specs=pl.BlockSpec((bm, half), lambda b, be, nv, br: (b, 0)),
            scratch_shapes=[pltpu.VMEM((bm, d), jnp.bfloat16), pltpu.VMEM((d, de), jnp.bfloat16),
                            pltpu.VMEM((d, de), jnp.bfloat16), pltpu.VMEM((de, d), jnp.bfloat16)],
        ),
        out_shape=jax.ShapeDtypeStruct((n_rows, half), jnp.uint32),
        compiler_params=_params(("arbitrary",), vmem + (8 << 20)),
        name="experts",
    )(blk_exp, n_valid, blk_rows, xs, w1, w3, w2)


def _gather_rows(idx, ys):
    n_groups, _, c = idx.shape
    width = ys.shape[1]
    n_workers = SC_CORES_V7X * SC_SUBCORES_V7X
    assert n_groups % n_workers == 0
    per_worker = n_groups // n_workers
    mesh = plsc.VectorSubcoreMesh(core_axis_name="c", subcore_axis_name="s")

    @functools.partial(
        pl.kernel, mesh=mesh,
        out_type=jax.ShapeDtypeStruct((n_groups * TOP_K * c, width), ys.dtype),
        scratch_types=[pltpu.VMEM((TOP_K, c), jnp.int32), pltpu.VMEM((c, width), ys.dtype), pltpu.SemaphoreType.DMA],
        name="combine_gather",
    )
    def gather(idx_hbm, ys_hbm, out_hbm, idx_v, rows_v, sem):
        worker = lax.axis_index("s") * SC_CORES_V7X + lax.axis_index("c")

        @pl.loop(0, per_worker)
        def _(g):
            group = worker * per_worker + g
            pltpu.sync_copy(idx_hbm.at[group], idx_v)
            for j in range(TOP_K):
                pltpu.async_copy(ys_hbm.at[idx_v.at[j]], rows_v, sem).wait()
                pltpu.sync_copy(rows_v, out_hbm.at[pl.ds((group * TOP_K + j) * c, c)])

    return gather(idx, ys)


def _combine_kernel(*refs, tm, alpha, aliased):
    refs = refs[1:] if aliased else refs
    x1_ref, p_ref, gt_ref, w1_ref, w3_ref, w2_ref, wpg_ref, bpg_ref, wpe_ref, g_ref, b_ref, yb_ref, o_ref = refs
    x1 = x1_ref[...]
    xb = x1.astype(jnp.bfloat16)
    h1 = jnp.dot(xb, w1_ref[...], preferred_element_type=jnp.float32)
    h3 = jnp.dot(xb, w3_ref[...], preferred_element_type=jnp.float32)
    h = (_silu(h1) * h3).astype(jnp.bfloat16)
    total = alpha * x1 + jnp.dot(h, w2_ref[...], preferred_element_type=jnp.float32)
    gate = _sigmoid(jnp.dot(xb, wpg_ref[...], preferred_element_type=jnp.float32) + bpg_ref[...])
    pe = jnp.dot(p_ref[...].astype(jnp.bfloat16), wpe_ref[...], preferred_element_type=jnp.float32)
    total = total + gate * pe

    gt = gt_ref[...]
    half = yb_ref.shape[3]
    moe_lo = jnp.zeros((tm, half), jnp.float32)
    moe_hi = jnp.zeros((tm, half), jnp.float32)
    for j in range(TOP_K):
        w = yb_ref[0, j]
        g = gt[:, j:j + 1]
        moe_lo = moe_lo + g * lax.bitcast_convert_type(w << 16, jnp.float32)
        moe_hi = moe_hi + g * lax.bitcast_convert_type(w & jnp.uint32(0xFFFF0000), jnp.float32)
    total = total + jnp.concatenate([moe_lo, moe_hi], axis=1)
    o_ref[...] = _layer_norm_rows(total, g_ref[...], b_ref[...])


def _combine(x1, p2, gate_t, w_sh1, w_sh3, w_sh2, w_pg, b_pg, w_pe, ln_g, ln_b, yb, alpha, tm, first_tile, out_buf):
    n_tok, d = x1.shape
    ds = w_sh1.shape[1]
    dp = p2.shape[1]
    n_tiles = yb.shape[0]
    row = lambda i: (first_tile + i, 0)
    vmem = (2 * TOP_K * tm * (d // 2) * 4 + (3 * d * ds + d * d + dp * d) * 2 + 2 * 2 * tm * d * 4 + 2 * tm * dp * 4
            + 2 * tm * LANES * 4 + 6 * tm * d * 4)
    aliased = out_buf is not None
    in_specs = [
        pl.BlockSpec((tm, d), row),
        pl.BlockSpec((tm, dp), row),
        pl.BlockSpec((tm, LANES), row),
        _const_spec((d, ds)),
        _const_spec((d, ds)),
        _const_spec((ds, d)),
        _const_spec((d, d)),
        _const_spec((1, d)),
        _const_spec((dp, d)),
        _const_spec((1, d)),
        _const_spec((1, d)),
        pl.BlockSpec((1, TOP_K, tm, d // 2), lambda i: (i, 0, 0, 0)),
    ]
    args = [x1, p2, gate_t, w_sh1, w_sh3, w_sh2, w_pg, b_pg.reshape(1, d), w_pe, ln_g.reshape(1, d),
            ln_b.reshape(1, d), yb]
    if aliased:
        in_specs = [pl.BlockSpec(memory_space=pl.ANY)] + in_specs
        args = [out_buf] + args
    return pl.pallas_call(
        functools.partial(_combine_kernel, tm=tm, alpha=alpha, aliased=aliased),
        grid=(n_tiles,),
        in_specs=in_specs,
        out_specs=pl.BlockSpec((tm, d), row),
        out_shape=jax.ShapeDtypeStruct((n_tok, d), jnp.float32),
        input_output_aliases={0: 0} if aliased else {},
        compiler_params=_params(("arbitrary",), vmem + (8 << 20)),
        name="combine",
    )(*args)


def _tile(n, target):
    t = min(n, target)
    assert n % t == 0, (n, t)
    return t


def kernel(x, p, w_in, b_f, conv_w, conv_b, conv_ln_g, conv_ln_b, gmlp_ln_g, gmlp_ln_b, w_sp, b_sp, out_g, w_o,
           ln1_g, ln1_b, w_r, r_bias, w_e1, w_e3, w_e2, w_sh1, w_sh3, w_sh2, w_pe, w_pg, b_pg, ln2_g, ln2_b):
    bsz, seq, d = x.shape
    depth = w_in.shape[0]
    n_tok = bsz * seq
    alpha = (2.0 * depth) ** 0.25
    n_main = 2 * CONV_CH + 2 * GMLP_CH + 3 * ATT_CH
    assert w_in.shape[2] == n_main + ATT_HEADS and seq % CHUNK == 0

    bm = MOE_BLOCK
    n_blk = -(-n_tok * TOP_K // bm) + N_EXPERTS
    n_rows = n_blk * bm
    bf16 = jnp.bfloat16

    x2 = x.reshape(n_tok, d)
    for i in range(depth):
        w_main = w_in[i, :, :n_main].astype(bf16)
        w_f = jnp.pad(w_in[i, :, n_main:], ((0, 0), (0, LANES - ATT_HEADS))).astype(bf16)
        b_f_row = jnp.pad(b_f[i], (0, LANES - ATT_HEADS)).reshape(1, LANES)

        zc, zg, q, k, v, f = _in_projection(x2, w_main, w_f, _tile(n_tok, 256))
        ya = _conv_module(zc, conv_w[i], conv_b[i], conv_ln_g[i], conv_ln_b[i], out_g[i, :CONV_CH],
                          bsz, seq, _tile(seq, 256))
        yb = _gmlp_module(zg, gmlp_ln_g[i], gmlp_ln_b[i], w_sp[i], b_sp[i],
                          out_g[i, CONV_CH:CONV_CH + GMLP_CH], _tile(n_tok, 512))
        c = _forget_cumsum(f, b_f_row, bsz, seq, _tile(seq, 512))
        tq = _tile(seq, 2048)
        yc = _fox_attention(q, k, v, c, bsz, seq, tq, _tile(tq, 512), _tile(tq, 512))
        x1, x1p = _out_projection(ya, yb, yc, x2, w_o[i].astype(bf16), out_g[i, CONV_CH + GMLP_CH:],
                                  ln1_g[i], ln1_b[i], alpha, _tile(n_tok, 256))

        eidx, gate_t, rank, cnt = _router(x1, w_r[i], r_bias[i], _tile(n_tok, 512))
        counts = cnt[:, 0]
        padded = (counts + bm - 1) // bm * bm
        pend = jnp.cumsum(padded).astype(jnp.int32)
        pstart = pend - padded
        blk_start = jnp.arange(n_blk, dtype=jnp.int32) * bm
        blk_exp = jnp.minimum(jnp.sum(pend[None, :] <= blk_start[:, None], axis=1), N_EXPERTS - 1).astype(jnp.int32)
        n_valid = (pend[N_EXPERTS - 1:] // bm).astype(jnp.int32)
        blk_rows = jnp.clip((pstart + counts)[blk_exp] - blk_start, 0, bm).astype(jnp.int32)

        dest = _dest_rows(pstart, eidx, rank, _tile(n_tok, 4096))
        xs = _dispatch(dest, x1p, n_rows)
        ys = _experts(blk_exp, n_valid, blk_rows, xs, w_e1, w_e3, w_e2, i, bm)
        tm = _tile(n_tok, 256)
        n_tiles = n_tok // tm
        gather_idx = dest.reshape(TOP_K, n_tiles, tm).transpose(1, 0, 2).reshape(-1, TOP_K, SC_DISPATCH_CHUNK)
        groups = gather_idx.shape[0] // COMBINE_SPLITS
        tiles = n_tiles // COMBINE_SPLITS
        x2 = None
        for part in range(COMBINE_SPLITS):
            yb = _gather_rows(gather_idx[part * groups:(part + 1) * groups], ys).reshape(tiles, TOP_K, tm, d // 2)
            x2 = _combine(x1, p[i].reshape(n_tok, -1), gate_t, w_sh1[i].astype(bf16), w_sh3[i].astype(bf16),
                          w_sh2[i].astype(bf16), w_pg[i].astype(bf16), b_pg[i], w_pe[i].astype(bf16),
                          ln2_g[i], ln2_b[i], yb, alpha, tm, part * tiles, x2)
    return x2.reshape(bsz, seq, d)
```

```python
import functools

import jax
import jax.numpy as jnp
from jax import lax
from jax.experimental import pallas as pl
from jax.experimental.pallas import tpu as pltpu
from jax.experimental.pallas import tpu_sc as plsc

CONV_CH = 512
CONV_WIDTH = 31
GMLP_HEADS = 4
GMLP_HEAD_CH = 128
GMLP_CH = GMLP_HEADS * GMLP_HEAD_CH
CHUNK = 128
ATT_HEADS = 8
HEAD_DIM = 128
ATT_CH = ATT_HEADS * HEAD_DIM
N_EXPERTS = 64
TOP_K = 8
N_GROUPS = 8
GROUP_SIZE = N_EXPERTS // N_GROUPS
TOPK_GROUPS = 4
ROUTE_SCALE = 2.5
LN_EPS = 1e-5
LOG2E = 1.4426950408889634

LANES = 128
SUBLANES = 8
VMEM_BYTES_V7X = 64 * 1024 * 1024
SC_CORES_V7X = 2
SC_SUBCORES_V7X = 16

CONV_HALO = 32
CONV_ROWS = 32
MOE_BLOCK = 512
SC_DISPATCH_CHUNK = 64
SC_GATHER_PICKS = 4
COMBINE_SPLITS = 4


def _vmem_limit(nbytes):
    return int(min(nbytes, VMEM_BYTES_V7X - 8 * 1024 * 1024))


def _params(semantics, vmem_bytes):
    return pltpu.CompilerParams(dimension_semantics=semantics, vmem_limit_bytes=_vmem_limit(vmem_bytes))


def _const_spec(shape):
    nd = len(shape)
    return pl.BlockSpec(shape, lambda *_: (0,) * nd, pipeline_mode=pl.Buffered(1))


def _layer_norm_rows(x, g, b):
    mu = jnp.mean(x, axis=-1, keepdims=True)
    xc = x - mu
    var = jnp.mean(xc * xc, axis=-1, keepdims=True)
    return xc * lax.rsqrt(var + LN_EPS) * g + b


def _rms_rows(x):
    return x * lax.rsqrt(jnp.mean(x * x, axis=-1, keepdims=True) + LN_EPS)


def _sigmoid(x):
    return 1.0 / (1.0 + jnp.exp(-x))


def _silu(x):
    return x * _sigmoid(x)


def _inproj_kernel(x_ref, w_ref, wf_ref, zc_ref, zg_ref, q_ref, k_ref, v_ref, f_ref, *, q_scale):
    xb = x_ref[...].astype(jnp.bfloat16)
    wide = zc_ref.shape[1]

    def slab(n):
        return jnp.dot(xb, w_ref[:, n * wide:(n + 1) * wide], preferred_element_type=jnp.float32)

    f_ref[...] = jnp.dot(xb, wf_ref[...], preferred_element_type=jnp.float32)
    zc_ref[...] = slab(0)
    zg_ref[...] = slab(1)
    q_ref[...] = (slab(2) * q_scale).astype(jnp.bfloat16)
    k_ref[...] = slab(3).astype(jnp.bfloat16)
    v_ref[...] = slab(4).astype(jnp.bfloat16)


def _in_projection(x2, w_main, w_f, tm):
    n_tok, d = x2.shape
    wide = 2 * CONV_CH
    assert w_main.shape[1] == 5 * wide and ATT_CH == wide and 2 * GMLP_CH == wide
    row = lambda i: (i, 0)
    out_shape = (
        jax.ShapeDtypeStruct((n_tok, wide), jnp.float32),
        jax.ShapeDtypeStruct((n_tok, wide), jnp.float32),
        jax.ShapeDtypeStruct((n_tok, ATT_CH), jnp.bfloat16),
        jax.ShapeDtypeStruct((n_tok, ATT_CH), jnp.bfloat16),
        jax.ShapeDtypeStruct((n_tok, ATT_CH), jnp.bfloat16),
        jax.ShapeDtypeStruct((n_tok, LANES), jnp.float32),
    )
    vmem = (2 * tm * d * 4 + tm * d * 2 + d * 5 * wide * 2 + d * LANES * 2
            + 2 * 2 * tm * wide * 4 + 3 * 2 * tm * wide * 2 + 2 * tm * LANES * 4 + 2 * tm * wide * 4)
    return pl.pallas_call(
        functools.partial(_inproj_kernel, q_scale=HEAD_DIM ** -0.5 * LOG2E),
        grid=(n_tok // tm,),
        in_specs=[
            pl.BlockSpec((tm, d), row),
            _const_spec((d, 5 * wide)),
            _const_spec((d, LANES)),
        ],
        out_specs=[
            pl.BlockSpec((tm, wide), row),
            pl.BlockSpec((tm, wide), row),
            pl.BlockSpec((tm, ATT_CH), row),
            pl.BlockSpec((tm, ATT_CH), row),
            pl.BlockSpec((tm, ATT_CH), row),
            pl.BlockSpec((tm, LANES), row),
        ],
        out_shape=out_shape,
        compiler_params=_params(("arbitrary",), vmem + (8 << 20)),
        name="in_projection",
    )(x2, w_main, w_f)


def _conv_kernel(z_ref, w_ref, cb_ref, lng_ref, lnb_ref, og_ref, o_ref, hbuf_ref, sh_ref, *, ts):
    s = pl.program_id(1)
    span = CONV_HALO + ts - SUBLANES

    @pl.when(s == 0)
    def _():
        hbuf_ref[0:CONV_HALO, :] = jnp.zeros((CONV_HALO, CONV_CH), jnp.float32)

    @pl.when(s > 0)
    def _():
        hbuf_ref[0:CONV_HALO, :] = hbuf_ref[ts:ts + CONV_HALO, :]

    hbuf_ref[CONV_HALO:CONV_HALO + ts, :] = z_ref[:, :CONV_CH] * _sigmoid(z_ref[:, CONV_CH:])

    cb = cb_ref[...]
    lng = lng_ref[...]
    lnb = lnb_ref[...]
    og = og_ref[...]
    for r in range(1, SUBLANES):
        sh_ref[r - 1] = hbuf_ref[r:r + span, :]
    first = CONV_HALO - (CONV_WIDTH - 1)
    for c in range(ts // CONV_ROWS):
        acc = jnp.broadcast_to(cb, (CONV_ROWS, CONV_CH))
        for j in range(CONV_WIDTH):
            off = first + j + c * CONV_ROWS
            r = off % SUBLANES
            a = off - r
            rows = hbuf_ref[a:a + CONV_ROWS, :] if r == 0 else sh_ref[r - 1, a:a + CONV_ROWS, :]
            acc = acc + w_ref[j:j + 1, :] * rows
        y = _silu(_layer_norm_rows(acc, lng, lnb))
        o_ref[c * CONV_ROWS:(c + 1) * CONV_ROWS, :] = (_rms_rows(y) * og).astype(jnp.bfloat16)


def _conv_module(zc, conv_w, conv_b, ln_g, ln_b, og, bsz, seq, ts):
    n_tok = zc.shape[0]
    n_s = seq // ts
    vec = lambda v: v.reshape(1, CONV_CH)
    return pl.pallas_call(
        functools.partial(_conv_kernel, ts=ts),
        grid=(bsz, n_s),
        in_specs=[
            pl.BlockSpec((ts, 2 * CONV_CH), lambda b, s: (b * n_s + s, 0)),
            _const_spec((CONV_WIDTH, CONV_CH)),
            _const_spec((1, CONV_CH)),
            _const_spec((1, CONV_CH)),
            _const_spec((1, CONV_CH)),
            _const_spec((1, CONV_CH)),
        ],
        out_specs=pl.BlockSpec((ts, CONV_CH), lambda b, s: (b * n_s + s, 0)),
        out_shape=jax.ShapeDtypeStruct((n_tok, CONV_CH), jnp.bfloat16),
        scratch_shapes=[pltpu.VMEM((CONV_HALO + ts, CONV_CH), jnp.float32),
                        pltpu.VMEM((SUBLANES - 1, CONV_HALO + ts - SUBLANES, CONV_CH), jnp.float32)],
        compiler_params=_params(("arbitrary", "arbitrary"), 32 << 20),
        name="conv_module",
    )(zc, conv_w, vec(conv_b), vec(ln_g), vec(ln_b), vec(og))


def _gelu_tanh(x):
    c = (2.0 / jnp.pi) ** 0.5
    return 0.5 * x * (1.0 + jnp.tanh(c * (x + 0.044715 * (x * x * x))))


def _gmlp_kernel(z_ref, lng_ref, lnb_ref, wsp_ref, bsp_ref, og_ref, o_ref, y_ref, *, tg):
    z = _gelu_tanh(z_ref[...])
    u = z[:, :GMLP_CH]
    v = _layer_norm_rows(z[:, GMLP_CH:], lng_ref[...], lnb_ref[...]).astype(jnp.bfloat16)
    t_idx = lax.broadcasted_iota(jnp.int32, (CHUNK, CHUNK), 0)
    s_idx = lax.broadcasted_iota(jnp.int32, (CHUNK, CHUNK), 1)
    causal = s_idx <= t_idx
    for h in range(GMLP_HEADS):
        ws = jnp.where(causal, wsp_ref[h], 0.0).astype(jnp.bfloat16)
        bias = bsp_ref[h]
        cols = slice(h * GMLP_HEAD_CH, (h + 1) * GMLP_HEAD_CH)
        for c in range(tg // CHUNK):
            rows = slice(c * CHUNK, (c + 1) * CHUNK)
            mixed = jnp.dot(ws, v[rows, cols], preferred_element_type=jnp.float32) + bias
            y_ref[rows, cols] = u[rows, cols] * mixed
    o_ref[...] = (_rms_rows(y_ref[...]) * og_ref[...]).astype(jnp.bfloat16)


def _gmlp_module(zg, ln_g, ln_b, w_sp, b_sp, og, tg):
    n_tok = zg.shape[0]
    vec = lambda v: v.reshape(1, GMLP_CH)
    return pl.pallas_call(
        functools.partial(_gmlp_kernel, tg=tg),
        grid=(n_tok // tg,),
        in_specs=[
            pl.BlockSpec((tg, 2 * GMLP_CH), lambda i: (i, 0)),
            _const_spec((1, GMLP_CH)),
            _const_spec((1, GMLP_CH)),
            _const_spec((GMLP_HEADS, CHUNK, CHUNK)),
            _const_spec((GMLP_HEADS, CHUNK, 1)),
            _const_spec((1, GMLP_CH)),
        ],
        out_specs=pl.BlockSpec((tg, GMLP_CH), lambda i: (i, 0)),
        out_shape=jax.ShapeDtypeStruct((n_tok, GMLP_CH), jnp.bfloat16),
        scratch_shapes=[pltpu.VMEM((tg, GMLP_CH), jnp.float32)],
        compiler_params=_params(("arbitrary",), 32 << 20),
        name="gmlp_module",
    )(zg, vec(ln_g), vec(ln_b), w_sp, b_sp.reshape(GMLP_HEADS, CHUNK, 1), vec(og))


def _split3(x):
    hi = x.astype(jnp.bfloat16)
    r1 = x - hi.astype(jnp.float32)
    mid = r1.astype(jnp.bfloat16)
    lo = (r1 - mid.astype(jnp.float32)).astype(jnp.bfloat16)
    return hi, mid, lo


def _fcum_kernel(f_ref, bf_ref, c_ref, carry_ref, *, ts):
    s = pl.program_id(1)

    @pl.when(s == 0)
    def _():
        carry_ref[...] = jnp.zeros_like(carry_ref)

    x = f_ref[...] + bf_ref[...]
    ls = -(jnp.maximum(-x, 0.0) + jnp.log(1.0 + jnp.exp(-jnp.abs(x))))
    t_idx = lax.broadcasted_iota(jnp.int32, (ts, ts), 0)
    s_idx = lax.broadcasted_iota(jnp.int32, (ts, ts), 1)
    tri = jnp.where(s_idx <= t_idx, 1.0, 0.0).astype(jnp.bfloat16)
    hi, mid, lo = _split3(ls)
    c = (jnp.dot(tri, lo, preferred_element_type=jnp.float32)
         + jnp.dot(tri, mid, preferred_element_type=jnp.float32)
         + jnp.dot(tri, hi, preferred_element_type=jnp.float32)) + carry_ref[...]
    c_ref[...] = c * LOG2E
    carry_ref[...] = c[ts - 1:ts, :]


def _forget_cumsum(f, b_f_row, bsz, seq, ts):
    n_tok = f.shape[0]
    n_s = seq // ts
    return pl.pallas_call(
        functools.partial(_fcum_kernel, ts=ts),
        grid=(bsz, n_s),
        in_specs=[
            pl.BlockSpec((ts, LANES), lambda b, s: (b * n_s + s, 0)),
            _const_spec((1, LANES)),
        ],
        out_specs=pl.BlockSpec((ts, LANES), lambda b, s: (b * n_s + s, 0)),
        out_shape=jax.ShapeDtypeStruct((n_tok, LANES), jnp.float32),
        scratch_shapes=[pltpu.VMEM((1, LANES), jnp.float32)],
        compiler_params=_params(("arbitrary", "arbitrary"), 32 << 20),
        name="forget_cumsum",
    )(f, b_f_row)


def _bias_columns(c_tile, h, as_query):
    rows = c_tile.shape[0]
    lane = lax.broadcasted_iota(jnp.int32, (rows, LANES), 1)
    col = jnp.sum(jnp.where(lane == h, c_tile, 0.0), axis=1, keepdims=True)
    val = jnp.broadcast_to(col if as_query else -col, (rows, LANES))
    hi, mid, lo = (t.astype(jnp.float32) for t in _split3(val))
    base = 0 if as_query else 3
    out = jnp.where(lane < 6, 1.0, 0.0)
    for offset, term in enumerate((hi, mid, lo)):
        out = jnp.where(lane == base + offset, term, out)
    return out.astype(jnp.bfloat16)


def _fox_kernel(q_ref, k_ref, v_ref, cq_ref, ck_ref, o_ref, kaug_ref, vt_ref, qaug_ref, m_ref, l_ref, acc_ref, s_ref,
                *, tq, tk, cw, seq):
    h = pl.program_id(1)
    i = pl.program_id(2)
    n_chain = tq // cw

    @pl.when(i == 0)
    def _():
        for r in range(seq // tk):
            rows = slice(r * tk, (r + 1) * tk)
            kaug_ref[rows, :HEAD_DIM] = k_ref[rows, :]
            kaug_ref[rows, HEAD_DIM:] = _bias_columns(ck_ref[rows, :], h, as_query=False)
            vt_ref[:, rows] = v_ref[rows, :].astype(jnp.float32).T.astype(jnp.bfloat16)

    qaug_ref[:, :HEAD_DIM] = q_ref[...]
    qaug_ref[:, HEAD_DIM:] = _bias_columns(cq_ref[...], h, as_query=True)
    m_ref[...] = jnp.full(m_ref.shape, -jnp.inf, jnp.float32)
    l_ref[...] = jnp.zeros_like(l_ref)
    acc_ref[...] = jnp.zeros_like(acc_ref)

    def scores(slot, c, k0):
        s_ref[slot] = lax.dot_general(kaug_ref[pl.ds(k0, tk), :], qaug_ref[c * cw:(c + 1) * cw, :],
                                      (((1,), (1,)), ((), ())), preferred_element_type=jnp.float32)

    def absorb(slot, c, k0, key_minus_query):
        s = s_ref[slot]
        if key_minus_query is not None:
            key = lax.broadcasted_iota(jnp.int32, (tk, cw), 0) + key_minus_query
            qry = lax.broadcasted_iota(jnp.int32, (tk, cw), 1)
            s = jnp.where(key <= qry, s, -jnp.inf)
        m_prev = m_ref[c]
        m_new = jnp.maximum(m_prev, jnp.max(s, axis=0, keepdims=True))
        alpha = jnp.exp2(m_prev - m_new)
        p = jnp.exp2(s - m_new)
        l_ref[c] = alpha * l_ref[c] + jnp.sum(p, axis=0, keepdims=True)
        pv = jnp.dot(vt_ref[:, pl.ds(k0, tk)], p.astype(jnp.bfloat16), preferred_element_type=jnp.float32)
        acc_ref[c] = alpha * acc_ref[c] + pv
        m_ref[c] = m_new

    n_kb = tq // tk

    def query_tile_keys(base, masks):
        work = [(kb, c) for kb in range(n_kb) for c in range(n_chain) if masks[kb][c] is not False]
        for slot, (kb, c) in enumerate(work):
            scores(slot, c, pl.multiple_of(base + kb * tk, tk))
        for slot, (kb, c) in enumerate(work):
            absorb(slot, c, pl.multiple_of(base + kb * tk, tk), masks[kb][c])

    def body(j, carry):
        query_tile_keys(j * tq, [[None] * n_chain] * n_kb)
        return carry

    lax.fori_loop(0, i, body, 0)
    masks = []
    for kb in range(n_kb):
        row = []
        for c in range(n_chain):
            first_key, first_query = kb * tk, c * cw
            if first_key + tk - 1 <= first_query:
                row.append(None)
            elif first_key <= first_query + cw - 1:
                row.append(first_key - first_query)
            else:
                row.append(False)
        masks.append(row)
    query_tile_keys(i * tq, masks)
    for c in range(n_chain):
        o_ref[c * cw:(c + 1) * cw, :] = (acc_ref[c] / l_ref[c]).T.astype(jnp.bfloat16)


def _fox_attention(q, k, v, c, bsz, seq, tq, tk, cw):
    n_tok = q.shape[0]
    nq = seq // tq
    n_chain = tq // cw
    return pl.pallas_call(
        functools.partial(_fox_kernel, tq=tq, tk=tk, cw=cw, seq=seq),
        grid=(bsz, ATT_HEADS, nq),
        in_specs=[
            pl.BlockSpec((tq, HEAD_DIM), lambda b, h, i: (b * nq + i, h)),
            pl.BlockSpec((seq, HEAD_DIM), lambda b, h, i: (b, h)),
            pl.BlockSpec((seq, HEAD_DIM), lambda b, h, i: (b, h)),
            pl.BlockSpec((tq, LANES), lambda b, h, i: (b * nq + i, 0)),
            pl.BlockSpec((seq, LANES), lambda b, h, i: (b, 0)),
        ],
        out_specs=pl.BlockSpec((tq, HEAD_DIM), lambda b, h, i: (b * nq + i, h)),
        out_shape=jax.ShapeDtypeStruct((n_tok, ATT_CH), jnp.bfloat16),
        scratch_shapes=[
            pltpu.VMEM((seq, 2 * HEAD_DIM), jnp.bfloat16),
            pltpu.VMEM((HEAD_DIM, seq), jnp.bfloat16),
            pltpu.VMEM((tq, 2 * HEAD_DIM), jnp.bfloat16),
            pltpu.VMEM((n_chain, 1, cw), jnp.float32),
            pltpu.VMEM((n_chain, 1, cw), jnp.float32),
            pltpu.VMEM((n_chain, HEAD_DIM, cw), jnp.float32),
            pltpu.VMEM((n_chain * (tq // tk), tk, cw), jnp.float32),
        ],
        compiler_params=_params(("arbitrary", "arbitrary", "arbitrary"), 56 << 20),
        name="fox_attention",
    )(q, k, v, c, c)


def _pack_bf16_pairs(x):
    n = x.shape[1] // 2
    r = x.astype(jnp.bfloat16).astype(jnp.float32)
    lo = lax.bitcast_convert_type(r[:, :n], jnp.uint32)
    hi = lax.bitcast_convert_type(r[:, n:], jnp.uint32)
    return (lo >> 16) | (hi & jnp.uint32(0xFFFF0000))


def _unpack_bf16_pairs(w):
    lo = lax.bitcast_convert_type(w << 16, jnp.float32).astype(jnp.bfloat16)
    hi = lax.bitcast_convert_type(w & jnp.uint32(0xFFFF0000), jnp.float32).astype(jnp.bfloat16)
    return lo, hi


def _outproj_kernel(ya_ref, yb_ref, yc_ref, x_ref, wo_ref, ogc_ref, g_ref, b_ref, x1_ref, x1p_ref, y_ref, *, alpha):
    y_ref[:, 0:CONV_CH] = ya_ref[...]
    y_ref[:, CONV_CH:CONV_CH + GMLP_CH] = yb_ref[...]
    yc = yc_ref[...].astype(jnp.float32)
    y_ref[:, CONV_CH + GMLP_CH:] = (_rms_rows(yc) * ogc_ref[...]).astype(jnp.bfloat16)
    h = jnp.dot(y_ref[...], wo_ref[...], preferred_element_type=jnp.float32)
    x1 = _layer_norm_rows(alpha * x_ref[...] + h, g_ref[...], b_ref[...])
    x1_ref[...] = x1
    x1p_ref[...] = _pack_bf16_pairs(x1)


def _out_projection(ya, yb, yc, x2, w_o, og_c, ln_g, ln_b, alpha, tm):
    n_tok, d = x2.shape
    mix = w_o.shape[0]
    row = lambda i: (i, 0)
    vmem = (mix * d * 2 + 2 * tm * d * 4 * 2 + 2 * tm * (d // 2) * 4 + 2 * tm * mix * 2 + tm * mix * 2 + 3 * tm * d * 4)
    return pl.pallas_call(
        functools.partial(_outproj_kernel, alpha=alpha),
        grid=(n_tok // tm,),
        in_specs=[
            pl.BlockSpec((tm, CONV_CH), row),
            pl.BlockSpec((tm, GMLP_CH), row),
            pl.BlockSpec((tm, ATT_CH), row),
            pl.BlockSpec((tm, d), row),
            _const_spec((mix, d)),
            _const_spec((1, ATT_CH)),
            _const_spec((1, d)),
            _const_spec((1, d)),
        ],
        out_specs=[pl.BlockSpec((tm, d), row), pl.BlockSpec((tm, d // 2), row)],
        out_shape=(
            jax.ShapeDtypeStruct((n_tok, d), jnp.float32),
            jax.ShapeDtypeStruct((n_tok, d // 2), jnp.uint32),
        ),
        scratch_shapes=[pltpu.VMEM((tm, mix), jnp.bfloat16)],
        compiler_params=_params(("arbitrary",), vmem + (8 << 20)),
        name="out_projection",
    )(ya, yb, yc, x2, w_o, og_c.reshape(1, ATT_CH), ln_g.reshape(1, d), ln_b.reshape(1, d))


def _first_argmax_rows(x, idx):
    m = jnp.max(x, axis=0, keepdims=True)
    first = jnp.min(jnp.where(x == m, idx, x.shape[0]), axis=0, keepdims=True)
    return m, first


def _router_kernel(x_ref, wh_ref, wl_ref, rb_ref, eidx_ref, gate_t_ref, rank_ref, cnt_ref, carry_ref, *, tr):
    i = pl.program_id(0)

    @pl.when(i == 0)
    def _():
        carry_ref[...] = jnp.zeros_like(carry_ref)

    x = x_ref[...]
    xh = x.astype(jnp.bfloat16)
    xl = (x - xh.astype(jnp.float32)).astype(jnp.bfloat16)
    nt = (((1,), (1,)), ((), ()))
    wh = wh_ref[...]
    logits = (lax.dot_general(wl_ref[...], xh, nt, preferred_element_type=jnp.float32)
              + lax.dot_general(wh, xl, nt, preferred_element_type=jnp.float32)
              + lax.dot_general(wh, xh, nt, preferred_element_type=jnp.float32))
    scores = _sigmoid(logits)
    sel = scores + rb_ref[...]

    neg = -jnp.inf
    e_idx = lax.broadcasted_iota(jnp.int32, (N_EXPERTS, tr), 0)
    g_idx = lax.broadcasted_iota(jnp.int32, (N_GROUPS, tr), 0)
    in_idx = lax.broadcasted_iota(jnp.int32, (GROUP_SIZE, tr), 0)

    gs_rows = []
    for g in range(N_GROUPS):
        blk = sel[g * GROUP_SIZE:(g + 1) * GROUP_SIZE, :]
        m1, a1 = _first_argmax_rows(blk, in_idx)
        m2 = jnp.max(jnp.where(in_idx == a1, neg, blk), axis=0, keepdims=True)
        gs_rows.append(m1 + m2)
    gs = jnp.concatenate(gs_rows, axis=0)

    gsel = jnp.zeros((N_GROUPS, tr), jnp.float32)
    for _ in range(TOPK_GROUPS):
        _, a = _first_argmax_rows(gs, g_idx)
        pick = g_idx == a
        gsel = jnp.where(pick, 1.0, gsel)
        gs = jnp.where(pick, neg, gs)
    esel = jnp.concatenate(
        [jnp.broadcast_to(gsel[g:g + 1, :], (GROUP_SIZE, tr)) for g in range(N_GROUPS)], axis=0)
    cand = jnp.where(esel > 0.5, sel, neg)

    picks, gates = [], []
    chosen = jnp.zeros((N_EXPERTS, tr), jnp.float32)
    for _ in range(TOP_K):
        _, a = _first_argmax_rows(cand, e_idx)
        pick = e_idx == a
        picks.append(a)
        gates.append(jnp.sum(jnp.where(pick, scores, 0.0), axis=0, keepdims=True))
        chosen = jnp.where(pick, 1.0, chosen)
        cand = jnp.where(pick, neg, cand)
    gate = jnp.concatenate(gates, axis=0)
    gate = gate / jnp.sum(gate, axis=0, keepdims=True) * ROUTE_SCALE

    r_idx = lax.broadcasted_iota(jnp.int32, (tr, tr), 0)
    c_idx = lax.broadcasted_iota(jnp.int32, (tr, tr), 1)
    upper = jnp.where(r_idx <= c_idx, 1.0, 0.0).astype(jnp.bfloat16)
    incl = jnp.dot(chosen.astype(jnp.bfloat16), upper, preferred_element_type=jnp.float32)
    rank_all = carry_ref[...] + incl - chosen
    carry_ref[...] = carry_ref[...] + incl[:, tr - 1:tr]
    ranks = [jnp.sum(jnp.where(e_idx == a, rank_all, 0.0), axis=0, keepdims=True) for a in picks]

    eidx_ref[...] = jnp.concatenate(picks, axis=0)
    rank_ref[...] = jnp.concatenate(ranks, axis=0).astype(jnp.int32)
    pad = jnp.zeros((LANES - TOP_K, tr), jnp.float32)
    gate_t_ref[...] = jnp.concatenate([gate, pad], axis=0).T
    cnt_ref[...] = jnp.broadcast_to(carry_ref[...], (N_EXPERTS, LANES)).astype(jnp.int32)


def _router(x1, w_r, r_bias, tr):
    n_tok, d = x1.shape
    wt = w_r.T
    wh = wt.astype(jnp.bfloat16)
    wl = (wt - wh.astype(jnp.float32)).astype(jnp.bfloat16)
    col = lambda i: (0, i)
    return pl.pallas_call(
        functools.partial(_router_kernel, tr=tr),
        grid=(n_tok // tr,),
        in_specs=[
            pl.BlockSpec((tr, d), lambda i: (i, 0)),
            _const_spec((N_EXPERTS, d)),
            _const_spec((N_EXPERTS, d)),
            _const_spec((N_EXPERTS, 1)),
        ],
        out_specs=[
            pl.BlockSpec((TOP_K, tr), col),
            pl.BlockSpec((tr, LANES), lambda i: (i, 0)),
            pl.BlockSpec((TOP_K, tr), col),
            pl.BlockSpec((N_EXPERTS, LANES), lambda i: (0, 0)),
        ],
        out_shape=(
            jax.ShapeDtypeStruct((TOP_K, n_tok), jnp.int32),
            jax.ShapeDtypeStruct((n_tok, LANES), jnp.float32),
            jax.ShapeDtypeStruct((TOP_K, n_tok), jnp.int32),
            jax.ShapeDtypeStruct((N_EXPERTS, LANES), jnp.int32),
        ),
        scratch_shapes=[pltpu.VMEM((N_EXPERTS, 1), jnp.float32)],
        compiler_params=_params(("arbitrary",), 32 << 20),
        name="router",
    )(x1, wh, wl, r_bias.reshape(N_EXPERTS, 1))


def _dest_kernel(pstart_ref, eidx_ref, rank_ref, dest_ref):
    e = eidx_ref[...]
    base = jnp.zeros(e.shape, jnp.int32)
    for k in range(N_EXPERTS):
        base = jnp.where(e == k, pstart_ref[k], base)
    dest_ref[...] = base + rank_ref[...]


def _dest_rows(pstart, eidx, rank, tc):
    n_tok = eidx.shape[1]
    col = lambda i, ps: (0, i)
    return pl.pallas_call(
        _dest_kernel,
        grid_spec=pltpu.PrefetchScalarGridSpec(
            num_scalar_prefetch=1,
            grid=(n_tok // tc,),
            in_specs=[pl.BlockSpec((TOP_K, tc), col), pl.BlockSpec((TOP_K, tc), col)],
            out_specs=pl.BlockSpec((TOP_K, tc), col),
        ),
        out_shape=jax.ShapeDtypeStruct((TOP_K, n_tok), jnp.int32),
        compiler_params=_params(("arbitrary",), 32 << 20),
        name="dest_rows",
    )(pstart, eidx, rank)


def _dispatch(dest, x1p, n_rows):
    n_tok, width = x1p.shape
    c = SC_DISPATCH_CHUNK
    n_workers = SC_CORES_V7X * SC_SUBCORES_V7X
    n_chunks = n_tok // c
    assert n_tok % c == 0 and n_chunks % n_workers == 0
    per_worker = n_chunks // n_workers
    idx = dest.reshape(TOP_K, n_chunks, c).transpose(1, 0, 2)
    mesh = plsc.VectorSubcoreMesh(core_axis_name="c", subcore_axis_name="s")

    @functools.partial(
        pl.kernel, mesh=mesh,
        out_type=jax.ShapeDtypeStruct((n_rows, width), x1p.dtype),
        scratch_types=[pltpu.VMEM((TOP_K, c), jnp.int32), pltpu.VMEM((c, width), x1p.dtype), pltpu.SemaphoreType.DMA],
        name="dispatch",
    )
    def scatter_rows(idx_hbm, x_hbm, xs_hbm, idx_v, rows_v, sem):
        worker = lax.axis_index("s") * SC_CORES_V7X + lax.axis_index("c")

        @pl.loop(0, per_worker)
        def _(g):
            chunk = worker * per_worker + g
            pltpu.sync_copy(idx_hbm.at[chunk], idx_v)
            pltpu.sync_copy(x_hbm.at[pl.ds(chunk * c, c)], rows_v)
            copies = [pltpu.make_async_copy(rows_v, xs_hbm.at[idx_v.at[j]], sem) for j in range(TOP_K)]
            for cp in copies:
                cp.start()
            for cp in copies:
                cp.wait()

    return scatter_rows(idx, x1p)


def _experts_kernel(be_ref, nv_ref, rows_ref, xs_ref, w1_ref, w3_ref, w2_ref, ys_ref, xb_ref, w1b_ref, w3b_ref,
                    w2b_ref):
    b = pl.program_id(0)

    @pl.when(b < nv_ref[0])
    def _():
        @pl.when((b == 0) | (be_ref[b] != be_ref[jnp.maximum(b - 1, 0)]))
        def _():
            w1b_ref[...] = w1_ref[0].astype(jnp.bfloat16)
            w3b_ref[...] = w3_ref[0].astype(jnp.bfloat16)
            w2b_ref[...] = w2_ref[0].astype(jnp.bfloat16)

        bm, half = xs_ref.shape
        routed = lax.broadcasted_iota(jnp.int32, (bm, half), 0) < rows_ref[b]
        lo, hi = _unpack_bf16_pairs(jnp.where(routed, xs_ref[...], jnp.uint32(0)))
        xb_ref[:, :half] = lo
        xb_ref[:, half:] = hi
        xb = xb_ref[...]
        h1 = jnp.dot(xb, w1b_ref[...], preferred_element_type=jnp.float32)
        h3 = jnp.dot(xb, w3b_ref[...], preferred_element_type=jnp.float32)
        h = (_silu(h1) * h3).astype(jnp.bfloat16)
        ys_ref[...] = _pack_bf16_pairs(jnp.dot(h, w2b_ref[...], preferred_element_type=jnp.float32))

    @pl.when(b >= nv_ref[0])
    def _():
        ys_ref[...] = jnp.zeros_like(ys_ref)


def _experts(blk_exp, n_valid, blk_rows, xs, w1, w3, w2, layer, bm):
    n_rows, half = xs.shape
    d = 2 * half
    de = w1.shape[3]
    n_blk = n_rows // bm
    blk = lambda b, be, nv, br: (jnp.minimum(b, nv[0] - 1), 0)
    wsel = lambda b, be, nv, br: (layer, be[jnp.minimum(b, nv[0] - 1)], 0, 0)
    vmem = (2 * bm * half * 4 + bm * d * 2 + 2 * 3 * d * de * 4 + 3 * d * de * 2 + 2 * bm * half * 4
            + 4 * bm * de * 4 + 2 * bm * d * 4)
    return pl.pallas_call(
        _experts_kernel,
        grid_spec=pltpu.PrefetchScalarGridSpec(
            num_scalar_prefetch=3,
            grid=(n_blk,),
            in_specs=[
                pl.BlockSpec((bm, half), blk),
                pl.BlockSpec((None, 1, d, de), wsel),
                pl.BlockSpec((None, 1, d, de), wsel),
                pl.BlockSpec((None, 1, de, d), wsel),
            ],
            out_specs=pl.BlockSpec((bm, half), lambda b, be, nv, br: (b, 0)),
            scratch_shapes=[pltpu.VMEM((bm, d), jnp.bfloat16), pltpu.VMEM((d, de), jnp.bfloat16),
                            pltpu.VMEM((d, de), jnp.bfloat16), pltpu.VMEM((de, d), jnp.bfloat16)],
        ),
        out_shape=jax.ShapeDtypeStruct((n_rows, half), jnp.uint32),
        compiler_params=_params(("arbitrary",), vmem + (8 << 20)),
        name="experts",
    )(blk_exp, n_valid, blk_rows, xs, w1, w3, w2)


def _gather_rows(idx, ys):
    n_groups, n, c = idx.shape
    width = ys.shape[1]
    n_workers = SC_CORES_V7X * SC_SUBCORES_V7X
    assert n_groups % n_workers == 0
    per_worker = n_groups // n_workers
    mesh = plsc.VectorSubcoreMesh(core_axis_name="c", subcore_axis_name="s")

    @functools.partial(
        pl.kernel, mesh=mesh,
        out_type=jax.ShapeDtypeStruct((n_groups * n * c, width), ys.dtype),
        scratch_types=[pltpu.VMEM((n, c), jnp.int32), pltpu.VMEM((c, width), ys.dtype), pltpu.SemaphoreType.DMA],
        name="combine_gather",
    )
    def gather(idx_hbm, ys_hbm, out_hbm, idx_v, rows_v, sem):
        worker = lax.axis_index("s") * SC_CORES_V7X + lax.axis_index("c")

        @pl.loop(0, per_worker)
        def _(g):
            group = worker * per_worker + g
            pltpu.sync_copy(idx_hbm.at[group], idx_v)
            for j in range(n):
                pltpu.async_copy(ys_hbm.at[idx_v.at[j]], rows_v, sem).wait()
                pltpu.sync_copy(rows_v, out_hbm.at[pl.ds((group * n + j) * c, c)])

    return gather(idx, ys)


def _combine_kernel(*refs, tm, alpha, aliased):
    refs = refs[1:] if aliased else refs
    (dest_ref, dest_next_ref, x1_ref, p_ref, gt_ref, w1_ref, w3_ref, w2_ref, wpg_ref, bpg_ref, wpe_ref, g_ref, b_ref,
     yb_ref, ys_ref, o_ref, ybuf_ref, sems) = refs
    i = pl.program_id(0)
    slot = lax.rem(i, 2)
    other = 1 - slot
    tc_picks = range(SC_GATHER_PICKS, TOP_K)

    def row_copy(d_ref, s, j, r):
        return pltpu.make_async_copy(ys_ref.at[pl.ds(d_ref[j, r], 1), :],
                                     ybuf_ref.at[s, j - SC_GATHER_PICKS, pl.ds(r, 1), :], sems.at[s])

    def drain(s):
        for j in tc_picks:
            pltpu.make_async_copy(ys_ref.at[pl.ds(0, tm), :], ybuf_ref.at[s, j - SC_GATHER_PICKS], sems.at[s]).wait()

    @pl.when(i == 0)
    def _():
        def first_tile(r, carry):
            for j in tc_picks:
                row_copy(dest_ref, 0, j, r).start()
            return carry

        lax.fori_loop(0, tm, first_tile, 0)

    for r in range(tm):
        for j in tc_picks:
            row_copy(dest_next_ref, other, j, r).start()

    x1 = x1_ref[...]
    xb = x1.astype(jnp.bfloat16)
    h1 = jnp.dot(xb, w1_ref[...], preferred_element_type=jnp.float32)
    h3 = jnp.dot(xb, w3_ref[...], preferred_element_type=jnp.float32)
    h = (_silu(h1) * h3).astype(jnp.bfloat16)
    total = alpha * x1 + jnp.dot(h, w2_ref[...], preferred_element_type=jnp.float32)
    gate = _sigmoid(jnp.dot(xb, wpg_ref[...], preferred_element_type=jnp.float32) + bpg_ref[...])
    pe = jnp.dot(p_ref[...].astype(jnp.bfloat16), wpe_ref[...], preferred_element_type=jnp.float32)
    total = total + gate * pe

    drain(slot)
    gt = gt_ref[...]
    half = yb_ref.shape[3]
    moe_lo = jnp.zeros((tm, half), jnp.float32)
    moe_hi = jnp.zeros((tm, half), jnp.float32)
    for j in range(TOP_K):
        w = yb_ref[0, j] if j < SC_GATHER_PICKS else ybuf_ref[slot, j - SC_GATHER_PICKS]
        g = gt[:, j:j + 1]
        moe_lo = moe_lo + g * lax.bitcast_convert_type(w << 16, jnp.float32)
        moe_hi = moe_hi + g * lax.bitcast_convert_type(w & jnp.uint32(0xFFFF0000), jnp.float32)
    total = total + jnp.concatenate([moe_lo, moe_hi], axis=1)
    o_ref[...] = _layer_norm_rows(total, g_ref[...], b_ref[...])

    @pl.when(i == pl.num_programs(0) - 1)
    def _():
        drain(other)


def _combine(dest, x1, p2, gate_t, w_sh1, w_sh3, w_sh2, w_pg, b_pg, w_pe, ln_g, ln_b, yb, ys, alpha, tm, first_tile,
             out_buf):
    n_tok, d = x1.shape
    ds = w_sh1.shape[1]
    dp = p2.shape[1]
    n_tiles = yb.shape[0]
    row = lambda i: (first_tile + i, 0)
    vmem = (2 * TOP_K * tm * (d // 2) * 4 + (3 * d * ds + d * d + dp * d) * 2 + 2 * 2 * tm * d * 4 + 2 * tm * dp * 4
            + 2 * tm * LANES * 4 + 6 * tm * d * 4)
    aliased = out_buf is not None
    in_specs = [
        pl.BlockSpec((TOP_K, tm), lambda i: (0, first_tile + i), memory_space=pltpu.SMEM),
        pl.BlockSpec((TOP_K, tm), lambda i: (0, first_tile + jnp.minimum(i + 1, n_tiles - 1)),
                     memory_space=pltpu.SMEM),
        pl.BlockSpec((tm, d), row),
        pl.BlockSpec((tm, dp), row),
        pl.BlockSpec((tm, LANES), row),
        _const_spec((d, ds)),
        _const_spec((d, ds)),
        _const_spec((ds, d)),
        _const_spec((d, d)),
        _const_spec((1, d)),
        _const_spec((dp, d)),
        _const_spec((1, d)),
        _const_spec((1, d)),
        pl.BlockSpec((1, SC_GATHER_PICKS, tm, d // 2), lambda i: (i, 0, 0, 0)),
        pl.BlockSpec(memory_space=pl.ANY),
    ]
    args = [dest, dest, x1, p2, gate_t, w_sh1, w_sh3, w_sh2, w_pg, b_pg.reshape(1, d), w_pe, ln_g.reshape(1, d),
            ln_b.reshape(1, d), yb, ys]
    if aliased:
        in_specs = [pl.BlockSpec(memory_space=pl.ANY)] + in_specs
        args = [out_buf] + args
    return pl.pallas_call(
        functools.partial(_combine_kernel, tm=tm, alpha=alpha, aliased=aliased),
        grid=(n_tiles,),
        in_specs=in_specs,
        out_specs=pl.BlockSpec((tm, d), row),
        out_shape=jax.ShapeDtypeStruct((n_tok, d), jnp.float32),
        scratch_shapes=[pltpu.VMEM((2, TOP_K - SC_GATHER_PICKS, tm, d // 2), jnp.uint32),
                        pltpu.SemaphoreType.DMA((2,))],
        input_output_aliases={0: 0} if aliased else {},
        compiler_params=_params(("arbitrary",), vmem + (8 << 20)),
        name="combine",
    )(*args)


def _tile(n, target):
    t = min(n, target)
    assert n % t == 0, (n, t)
    return t


def kernel(x, p, w_in, b_f, conv_w, conv_b, conv_ln_g, conv_ln_b, gmlp_ln_g, gmlp_ln_b, w_sp, b_sp, out_g, w_o,
           ln1_g, ln1_b, w_r, r_bias, w_e1, w_e3, w_e2, w_sh1, w_sh3, w_sh2, w_pe, w_pg, b_pg, ln2_g, ln2_b):
    bsz, seq, d = x.shape
    depth = w_in.shape[0]
    n_tok = bsz * seq
    alpha = (2.0 * depth) ** 0.25
    n_main = 2 * CONV_CH + 2 * GMLP_CH + 3 * ATT_CH
    assert w_in.shape[2] == n_main + ATT_HEADS and seq % CHUNK == 0

    bm = MOE_BLOCK
    n_blk = -(-n_tok * TOP_K // bm) + N_EXPERTS
    n_rows = n_blk * bm
    bf16 = jnp.bfloat16

    x2 = x.reshape(n_tok, d)
    for i in range(depth):
        w_main = w_in[i, :, :n_main].astype(bf16)
        w_f = jnp.pad(w_in[i, :, n_main:], ((0, 0), (0, LANES - ATT_HEADS))).astype(bf16)
        b_f_row = jnp.pad(b_f[i], (0, LANES - ATT_HEADS)).reshape(1, LANES)

        zc, zg, q, k, v, f = _in_projection(x2, w_main, w_f, _tile(n_tok, 256))
        ya = _conv_module(zc, conv_w[i], conv_b[i], conv_ln_g[i], conv_ln_b[i], out_g[i, :CONV_CH],
                          bsz, seq, _tile(seq, 256))
        yb = _gmlp_module(zg, gmlp_ln_g[i], gmlp_ln_b[i], w_sp[i], b_sp[i],
                          out_g[i, CONV_CH:CONV_CH + GMLP_CH], _tile(n_tok, 512))
        c = _forget_cumsum(f, b_f_row, bsz, seq, _tile(seq, 512))
        tq = _tile(seq, 2048)
        yc = _fox_attention(q, k, v, c, bsz, seq, tq, _tile(tq, 512), _tile(tq, 512))
        x1, x1p = _out_projection(ya, yb, yc, x2, w_o[i].astype(bf16), out_g[i, CONV_CH + GMLP_CH:],
                                  ln1_g[i], ln1_b[i], alpha, _tile(n_tok, 256))

        eidx, gate_t, rank, cnt = _router(x1, w_r[i], r_bias[i], _tile(n_tok, 512))
        counts = cnt[:, 0]
        padded = (counts + bm - 1) // bm * bm
        pend = jnp.cumsum(padded).astype(jnp.int32)
        pstart = pend - padded
        blk_start = jnp.arange(n_blk, dtype=jnp.int32) * bm
        blk_exp = jnp.minimum(jnp.sum(pend[None, :] <= blk_start[:, None], axis=1), N_EXPERTS - 1).astype(jnp.int32)
        n_valid = (pend[N_EXPERTS - 1:] // bm).astype(jnp.int32)
        blk_rows = jnp.clip((pstart + counts)[blk_exp] - blk_start, 0, bm).astype(jnp.int32)

        dest = _dest_rows(pstart, eidx, rank, _tile(n_tok, 4096))
        xs = _dispatch(dest, x1p, n_rows)
        ys = _experts(blk_exp, n_valid, blk_rows, xs, w_e1, w_e3, w_e2, i, bm)
        tm = _tile(n_tok, 256)
        n_tiles = n_tok // tm
        gather_idx = (dest[:SC_GATHER_PICKS].reshape(SC_GATHER_PICKS, n_tiles, tm).transpose(1, 0, 2)
                      .reshape(-1, SC_GATHER_PICKS, SC_DISPATCH_CHUNK))
        groups = gather_idx.shape[0] // COMBINE_SPLITS
        tiles = n_tiles // COMBINE_SPLITS
        x2 = None
        for part in range(COMBINE_SPLITS):
            yb = _gather_rows(gather_idx[part * groups:(part + 1) * groups], ys)
            x2 = _combine(dest, x1, p[i].reshape(n_tok, -1), gate_t, w_sh1[i].astype(bf16), w_sh3[i].astype(bf16),
                          w_sh2[i].astype(bf16), w_pg[i].astype(bf16), b_pg[i], w_pe[i].astype(bf16),
                          ln2_g[i], ln2_b[i], yb.reshape(tiles, SC_GATHER_PICKS, tm, d // 2), ys, alpha, tm,
                          part * tiles, x2)
    return x2.reshape(bsz, seq, d)
```

```python
import functools

import jax
import jax.numpy as jnp
from jax import lax
from jax.experimental import pallas as pl
from jax.experimental.pallas import tpu as pltpu
from jax.experimental.pallas import tpu_sc as plsc

CONV_CH = 512
CONV_WIDTH = 31
GMLP_HEADS = 4
GMLP_HEAD_CH = 128
GMLP_CH = GMLP_HEADS * GMLP_HEAD_CH
CHUNK = 128
ATT_HEADS = 8
HEAD_DIM = 128
ATT_CH = ATT_HEADS * HEAD_DIM
N_EXPERTS = 64
TOP_K = 8
N_GROUPS = 8
GROUP_SIZE = N_EXPERTS // N_GROUPS
TOPK_GROUPS = 4
ROUTE_SCALE = 2.5
LN_EPS = 1e-5
LOG2E = 1.4426950408889634

LANES = 128
SUBLANES = 8
VMEM_BYTES_V7X = 64 * 1024 * 1024
SC_CORES_V7X = 2
SC_SUBCORES_V7X = 16

CONV_HALO = 32
CONV_ROWS = 32
MOE_BLOCK = 512
SC_DISPATCH_CHUNK = 64
SC_GATHER_PICKS = 6
COMBINE_SPLITS = 4


def _vmem_limit(nbytes):
    return int(min(nbytes, VMEM_BYTES_V7X - 8 * 1024 * 1024))


def _params(semantics, vmem_bytes):
    return pltpu.CompilerParams(dimension_semantics=semantics, vmem_limit_bytes=_vmem_limit(vmem_bytes))


def _const_spec(shape):
    nd = len(shape)
    return pl.BlockSpec(shape, lambda *_: (0,) * nd, pipeline_mode=pl.Buffered(1))


def _layer_norm_rows(x, g, b):
    mu = jnp.mean(x, axis=-1, keepdims=True)
    xc = x - mu
    var = jnp.mean(xc * xc, axis=-1, keepdims=True)
    return xc * lax.rsqrt(var + LN_EPS) * g + b


def _rms_rows(x):
    return x * lax.rsqrt(jnp.mean(x * x, axis=-1, keepdims=True) + LN_EPS)


def _sigmoid(x):
    return 1.0 / (1.0 + jnp.exp(-x))


def _silu(x):
    return x * _sigmoid(x)


def _inproj_kernel(x_ref, w_ref, wf_ref, zc_ref, zg_ref, q_ref, k_ref, v_ref, f_ref, *, q_scale):
    xb = x_ref[...].astype(jnp.bfloat16)
    wide = zc_ref.shape[1]

    def slab(n):
        return jnp.dot(xb, w_ref[:, n * wide:(n + 1) * wide], preferred_element_type=jnp.float32)

    f_ref[...] = jnp.dot(xb, wf_ref[...], preferred_element_type=jnp.float32)
    zc_ref[...] = slab(0)
    zg_ref[...] = slab(1)
    q_ref[...] = (slab(2) * q_scale).astype(jnp.bfloat16)
    k_ref[...] = slab(3).astype(jnp.bfloat16)
    v_ref[...] = slab(4).astype(jnp.bfloat16)


def _in_projection(x2, w_main, w_f, tm):
    n_tok, d = x2.shape
    wide = 2 * CONV_CH
    assert w_main.shape[1] == 5 * wide and ATT_CH == wide and 2 * GMLP_CH == wide
    row = lambda i: (i, 0)
    out_shape = (
        jax.ShapeDtypeStruct((n_tok, wide), jnp.float32),
        jax.ShapeDtypeStruct((n_tok, wide), jnp.float32),
        jax.ShapeDtypeStruct((n_tok, ATT_CH), jnp.bfloat16),
        jax.ShapeDtypeStruct((n_tok, ATT_CH), jnp.bfloat16),
        jax.ShapeDtypeStruct((n_tok, ATT_CH), jnp.bfloat16),
        jax.ShapeDtypeStruct((n_tok, LANES), jnp.float32),
    )
    vmem = (2 * tm * d * 4 + tm * d * 2 + d * 5 * wide * 2 + d * LANES * 2
            + 2 * 2 * tm * wide * 4 + 3 * 2 * tm * wide * 2 + 2 * tm * LANES * 4 + 2 * tm * wide * 4)
    return pl.pallas_call(
        functools.partial(_inproj_kernel, q_scale=HEAD_DIM ** -0.5 * LOG2E),
        grid=(n_tok // tm,),
        in_specs=[
            pl.BlockSpec((tm, d), row),
            _const_spec((d, 5 * wide)),
            _const_spec((d, LANES)),
        ],
        out_specs=[
            pl.BlockSpec((tm, wide), row),
            pl.BlockSpec((tm, wide), row),
            pl.BlockSpec((tm, ATT_CH), row),
            pl.BlockSpec((tm, ATT_CH), row),
            pl.BlockSpec((tm, ATT_CH), row),
            pl.BlockSpec((tm, LANES), row),
        ],
        out_shape=out_shape,
        compiler_params=_params(("arbitrary",), vmem + (8 << 20)),
        name="in_projection",
    )(x2, w_main, w_f)


def _conv_kernel(z_ref, w_ref, cb_ref, lng_ref, lnb_ref, og_ref, o_ref, hbuf_ref, sh_ref, *, ts):
    s = pl.program_id(1)
    span = CONV_HALO + ts - SUBLANES

    @pl.when(s == 0)
    def _():
        hbuf_ref[0:CONV_HALO, :] = jnp.zeros((CONV_HALO, CONV_CH), jnp.float32)

    @pl.when(s > 0)
    def _():
        hbuf_ref[0:CONV_HALO, :] = hbuf_ref[ts:ts + CONV_HALO, :]

    hbuf_ref[CONV_HALO:CONV_HALO + ts, :] = z_ref[:, :CONV_CH] * _sigmoid(z_ref[:, CONV_CH:])

    cb = cb_ref[...]
    lng = lng_ref[...]
    lnb = lnb_ref[...]
    og = og_ref[...]
    for r in range(1, SUBLANES):
        sh_ref[r - 1] = hbuf_ref[r:r + span, :]
    first = CONV_HALO - (CONV_WIDTH - 1)
    for c in range(ts // CONV_ROWS):
        acc = jnp.broadcast_to(cb, (CONV_ROWS, CONV_CH))
        for j in range(CONV_WIDTH):
            off = first + j + c * CONV_ROWS
            r = off % SUBLANES
            a = off - r
            rows = hbuf_ref[a:a + CONV_ROWS, :] if r == 0 else sh_ref[r - 1, a:a + CONV_ROWS, :]
            acc = acc + w_ref[j:j + 1, :] * rows
        y = _silu(_layer_norm_rows(acc, lng, lnb))
        o_ref[c * CONV_ROWS:(c + 1) * CONV_ROWS, :] = (_rms_rows(y) * og).astype(jnp.bfloat16)


def _conv_module(zc, conv_w, conv_b, ln_g, ln_b, og, bsz, seq, ts):
    n_tok = zc.shape[0]
    n_s = seq // ts
    vec = lambda v: v.reshape(1, CONV_CH)
    return pl.pallas_call(
        functools.partial(_conv_kernel, ts=ts),
        grid=(bsz, n_s),
        in_specs=[
            pl.BlockSpec((ts, 2 * CONV_CH), lambda b, s: (b * n_s + s, 0)),
            _const_spec((CONV_WIDTH, CONV_CH)),
            _const_spec((1, CONV_CH)),
            _const_spec((1, CONV_CH)),
            _const_spec((1, CONV_CH)),
            _const_spec((1, CONV_CH)),
        ],
        out_specs=pl.BlockSpec((ts, CONV_CH), lambda b, s: (b * n_s + s, 0)),
        out_shape=jax.ShapeDtypeStruct((n_tok, CONV_CH), jnp.bfloat16),
        scratch_shapes=[pltpu.VMEM((CONV_HALO + ts, CONV_CH), jnp.float32),
                        pltpu.VMEM((SUBLANES - 1, CONV_HALO + ts - SUBLANES, CONV_CH), jnp.float32)],
        compiler_params=_params(("arbitrary", "arbitrary"), 32 << 20),
        name="conv_module",
    )(zc, conv_w, vec(conv_b), vec(ln_g), vec(ln_b), vec(og))


def _gelu_tanh(x):
    c = (2.0 / jnp.pi) ** 0.5
    return 0.5 * x * (1.0 + jnp.tanh(c * (x + 0.044715 * (x * x * x))))


def _gmlp_kernel(z_ref, lng_ref, lnb_ref, wsp_ref, bsp_ref, og_ref, o_ref, y_ref, *, tg):
    z = _gelu_tanh(z_ref[...])
    u = z[:, :GMLP_CH]
    v = _layer_norm_rows(z[:, GMLP_CH:], lng_ref[...], lnb_ref[...]).astype(jnp.bfloat16)
    t_idx = lax.broadcasted_iota(jnp.int32, (CHUNK, CHUNK), 0)
    s_idx = lax.broadcasted_iota(jnp.int32, (CHUNK, CHUNK), 1)
    causal = s_idx <= t_idx
    for h in range(GMLP_HEADS):
        ws = jnp.where(causal, wsp_ref[h], 0.0).astype(jnp.bfloat16)
        bias = bsp_ref[h]
        cols = slice(h * GMLP_HEAD_CH, (h + 1) * GMLP_HEAD_CH)
        for c in range(tg // CHUNK):
            rows = slice(c * CHUNK, (c + 1) * CHUNK)
            mixed = jnp.dot(ws, v[rows, cols], preferred_element_type=jnp.float32) + bias
            y_ref[rows, cols] = u[rows, cols] * mixed
    o_ref[...] = (_rms_rows(y_ref[...]) * og_ref[...]).astype(jnp.bfloat16)


def _gmlp_module(zg, ln_g, ln_b, w_sp, b_sp, og, tg):
    n_tok = zg.shape[0]
    vec = lambda v: v.reshape(1, GMLP_CH)
    return pl.pallas_call(
        functools.partial(_gmlp_kernel, tg=tg),
        grid=(n_tok // tg,),
        in_specs=[
            pl.BlockSpec((tg, 2 * GMLP_CH), lambda i: (i, 0)),
            _const_spec((1, GMLP_CH)),
            _const_spec((1, GMLP_CH)),
            _const_spec((GMLP_HEADS, CHUNK, CHUNK)),
            _const_spec((GMLP_HEADS, CHUNK, 1)),
            _const_spec((1, GMLP_CH)),
        ],
        out_specs=pl.BlockSpec((tg, GMLP_CH), lambda i: (i, 0)),
        out_shape=jax.ShapeDtypeStruct((n_tok, GMLP_CH), jnp.bfloat16),
        scratch_shapes=[pltpu.VMEM((tg, GMLP_CH), jnp.float32)],
        compiler_params=_params(("arbitrary",), 32 << 20),
        name="gmlp_module",
    )(zg, vec(ln_g), vec(ln_b), w_sp, b_sp.reshape(GMLP_HEADS, CHUNK, 1), vec(og))


def _split3(x):
    hi = x.astype(jnp.bfloat16)
    r1 = x - hi.astype(jnp.float32)
    mid = r1.astype(jnp.bfloat16)
    lo = (r1 - mid.astype(jnp.float32)).astype(jnp.bfloat16)
    return hi, mid, lo


def _fcum_kernel(f_ref, bf_ref, c_ref, carry_ref, *, ts):
    s = pl.program_id(1)

    @pl.when(s == 0)
    def _():
        carry_ref[...] = jnp.zeros_like(carry_ref)

    x = f_ref[...] + bf_ref[...]
    ls = -(jnp.maximum(-x, 0.0) + jnp.log(1.0 + jnp.exp(-jnp.abs(x))))
    t_idx = lax.broadcasted_iota(jnp.int32, (ts, ts), 0)
    s_idx = lax.broadcasted_iota(jnp.int32, (ts, ts), 1)
    tri = jnp.where(s_idx <= t_idx, 1.0, 0.0).astype(jnp.bfloat16)
    hi, mid, lo = _split3(ls)
    c = (jnp.dot(tri, lo, preferred_element_type=jnp.float32)
         + jnp.dot(tri, mid, preferred_element_type=jnp.float32)
         + jnp.dot(tri, hi, preferred_element_type=jnp.float32)) + carry_ref[...]
    c_ref[...] = c * LOG2E
    carry_ref[...] = c[ts - 1:ts, :]


def _forget_cumsum(f, b_f_row, bsz, seq, ts):
    n_tok = f.shape[0]
    n_s = seq // ts
    return pl.pallas_call(
        functools.partial(_fcum_kernel, ts=ts),
        grid=(bsz, n_s),
        in_specs=[
            pl.BlockSpec((ts, LANES), lambda b, s: (b * n_s + s, 0)),
            _const_spec((1, LANES)),
        ],
        out_specs=pl.BlockSpec((ts, LANES), lambda b, s: (b * n_s + s, 0)),
        out_shape=jax.ShapeDtypeStruct((n_tok, LANES), jnp.float32),
        scratch_shapes=[pltpu.VMEM((1, LANES), jnp.float32)],
        compiler_params=_params(("arbitrary", "arbitrary"), 32 << 20),
        name="forget_cumsum",
    )(f, b_f_row)


def _bias_columns(c_tile, h, as_query):
    rows = c_tile.shape[0]
    lane = lax.broadcasted_iota(jnp.int32, (rows, LANES), 1)
    col = jnp.sum(jnp.where(lane == h, c_tile, 0.0), axis=1, keepdims=True)
    val = jnp.broadcast_to(col if as_query else -col, (rows, LANES))
    hi, mid, lo = (t.astype(jnp.float32) for t in _split3(val))
    base = 0 if as_query else 3
    out = jnp.where(lane < 6, 1.0, 0.0)
    for offset, term in enumerate((hi, mid, lo)):
        out = jnp.where(lane == base + offset, term, out)
    return out.astype(jnp.bfloat16)


def _fox_kernel(q_ref, k_ref, v_ref, cq_ref, ck_ref, o_ref, kaug_ref, vt_ref, qaug_ref, m_ref, l_ref, acc_ref, s_ref,
                *, tq, tk, cw, seq):
    h = pl.program_id(1)
    i = pl.program_id(2)
    n_chain = tq // cw

    @pl.when(i == 0)
    def _():
        for r in range(seq // tk):
            rows = slice(r * tk, (r + 1) * tk)
            kaug_ref[rows, :HEAD_DIM] = k_ref[rows, :]
            kaug_ref[rows, HEAD_DIM:] = _bias_columns(ck_ref[rows, :], h, as_query=False)
            vt_ref[:, rows] = v_ref[rows, :].astype(jnp.float32).T.astype(jnp.bfloat16)

    qaug_ref[:, :HEAD_DIM] = q_ref[...]
    qaug_ref[:, HEAD_DIM:] = _bias_columns(cq_ref[...], h, as_query=True)
    m_ref[...] = jnp.full(m_ref.shape, -jnp.inf, jnp.float32)
    l_ref[...] = jnp.zeros_like(l_ref)
    acc_ref[...] = jnp.zeros_like(acc_ref)

    def scores(slot, c, k0):
        s_ref[slot] = lax.dot_general(kaug_ref[pl.ds(k0, tk), :], qaug_ref[c * cw:(c + 1) * cw, :],
                                      (((1,), (1,)), ((), ())), preferred_element_type=jnp.float32)

    def absorb(slot, c, k0, key_minus_query):
        s = s_ref[slot]
        if key_minus_query is not None:
            key = lax.broadcasted_iota(jnp.int32, (tk, cw), 0) + key_minus_query
            qry = lax.broadcasted_iota(jnp.int32, (tk, cw), 1)
            s = jnp.where(key <= qry, s, -jnp.inf)
        m_prev = m_ref[c]
        m_new = jnp.maximum(m_prev, jnp.max(s, axis=0, keepdims=True))
        alpha = jnp.exp2(m_prev - m_new)
        p = jnp.exp2(s - m_new)
        l_ref[c] = alpha * l_ref[c] + jnp.sum(p, axis=0, keepdims=True)
        pv = jnp.dot(vt_ref[:, pl.ds(k0, tk)], p.astype(jnp.bfloat16), preferred_element_type=jnp.float32)
        acc_ref[c] = alpha * acc_ref[c] + pv
        m_ref[c] = m_new

    n_kb = tq // tk

    def query_tile_keys(base, masks):
        work = [(kb, c) for kb in range(n_kb) for c in range(n_chain) if masks[kb][c] is not False]
        for slot, (kb, c) in enumerate(work):
            scores(slot, c, pl.multiple_of(base + kb * tk, tk))
        for slot, (kb, c) in enumerate(work):
            absorb(slot, c, pl.multiple_of(base + kb * tk, tk), masks[kb][c])

    def body(j, carry):
        query_tile_keys(j * tq, [[None] * n_chain] * n_kb)
        return carry

    lax.fori_loop(0, i, body, 0)
    masks = []
    for kb in range(n_kb):
        row = []
        for c in range(n_chain):
            first_key, first_query = kb * tk, c * cw
            if first_key + tk - 1 <= first_query:
                row.append(None)
            elif first_key <= first_query + cw - 1:
                row.append(first_key - first_query)
            else:
                row.append(False)
        masks.append(row)
    query_tile_keys(i * tq, masks)
    for c in range(n_chain):
        o_ref[c * cw:(c + 1) * cw, :] = (acc_ref[c] / l_ref[c]).T.astype(jnp.bfloat16)


def _fox_attention(q, k, v, c, bsz, seq, tq, tk, cw):
    n_tok = q.shape[0]
    nq = seq // tq
    n_chain = tq // cw
    return pl.pallas_call(
        functools.partial(_fox_kernel, tq=tq, tk=tk, cw=cw, seq=seq),
        grid=(bsz, ATT_HEADS, nq),
        in_specs=[
            pl.BlockSpec((tq, HEAD_DIM), lambda b, h, i: (b * nq + i, h)),
            pl.BlockSpec((seq, HEAD_DIM), lambda b, h, i: (b, h)),
            pl.BlockSpec((seq, HEAD_DIM), lambda b, h, i: (b, h)),
            pl.BlockSpec((tq, LANES), lambda b, h, i: (b * nq + i, 0)),
            pl.BlockSpec((seq, LANES), lambda b, h, i: (b, 0)),
        ],
        out_specs=pl.BlockSpec((tq, HEAD_DIM), lambda b, h, i: (b * nq + i, h)),
        out_shape=jax.ShapeDtypeStruct((n_tok, ATT_CH), jnp.bfloat16),
        scratch_shapes=[
            pltpu.VMEM((seq, 2 * HEAD_DIM), jnp.bfloat16),
            pltpu.VMEM((HEAD_DIM, seq), jnp.bfloat16),
            pltpu.VMEM((tq, 2 * HEAD_DIM), jnp.bfloat16),
            pltpu.VMEM((n_chain, 1, cw), jnp.float32),
            pltpu.VMEM((n_chain, 1, cw), jnp.float32),
            pltpu.VMEM((n_chain, HEAD_DIM, cw), jnp.float32),
            pltpu.VMEM((n_chain * (tq // tk), tk, cw), jnp.float32),
        ],
        compiler_params=_params(("arbitrary", "arbitrary", "arbitrary"), 56 << 20),
        name="fox_attention",
    )(q, k, v, c, c)


def _pack_bf16_pairs(x):
    n = x.shape[1] // 2
    r = x.astype(jnp.bfloat16).astype(jnp.float32)
    lo = lax.bitcast_convert_type(r[:, :n], jnp.uint32)
    hi = lax.bitcast_convert_type(r[:, n:], jnp.uint32)
    return (lo >> 16) | (hi & jnp.uint32(0xFFFF0000))


def _unpack_bf16_pairs(w):
    lo = lax.bitcast_convert_type(w << 16, jnp.float32).astype(jnp.bfloat16)
    hi = lax.bitcast_convert_type(w & jnp.uint32(0xFFFF0000), jnp.float32).astype(jnp.bfloat16)
    return lo, hi


def _outproj_kernel(ya_ref, yb_ref, yc_ref, x_ref, wo_ref, ogc_ref, g_ref, b_ref, x1_ref, x1p_ref, y_ref, *, alpha):
    y_ref[:, 0:CONV_CH] = ya_ref[...]
    y_ref[:, CONV_CH:CONV_CH + GMLP_CH] = yb_ref[...]
    yc = yc_ref[...].astype(jnp.float32)
    y_ref[:, CONV_CH + GMLP_CH:] = (_rms_rows(yc) * ogc_ref[...]).astype(jnp.bfloat16)
    h = jnp.dot(y_ref[...], wo_ref[...], preferred_element_type=jnp.float32)
    x1 = _layer_norm_rows(alpha * x_ref[...] + h, g_ref[...], b_ref[...])
    x1_ref[...] = x1
    x1p_ref[...] = _pack_bf16_pairs(x1)


def _out_projection(ya, yb, yc, x2, w_o, og_c, ln_g, ln_b, alpha, tm):
    n_tok, d = x2.shape
    mix = w_o.shape[0]
    row = lambda i: (i, 0)
    vmem = (mix * d * 2 + 2 * tm * d * 4 * 2 + 2 * tm * (d // 2) * 4 + 2 * tm * mix * 2 + tm * mix * 2 + 3 * tm * d * 4)
    return pl.pallas_call(
        functools.partial(_outproj_kernel, alpha=alpha),
        grid=(n_tok // tm,),
        in_specs=[
            pl.BlockSpec((tm, CONV_CH), row),
            pl.BlockSpec((tm, GMLP_CH), row),
            pl.BlockSpec((tm, ATT_CH), row),
            pl.BlockSpec((tm, d), row),
            _const_spec((mix, d)),
            _const_spec((1, ATT_CH)),
            _const_spec((1, d)),
            _const_spec((1, d)),
        ],
        out_specs=[pl.BlockSpec((tm, d), row), pl.BlockSpec((tm, d // 2), row)],
        out_shape=(
            jax.ShapeDtypeStruct((n_tok, d), jnp.float32),
            jax.ShapeDtypeStruct((n_tok, d // 2), jnp.uint32),
        ),
        scratch_shapes=[pltpu.VMEM((tm, mix), jnp.bfloat16)],
        compiler_params=_params(("arbitrary",), vmem + (8 << 20)),
        name="out_projection",
    )(ya, yb, yc, x2, w_o, og_c.reshape(1, ATT_CH), ln_g.reshape(1, d), ln_b.reshape(1, d))


def _first_argmax_rows(x, idx):
    m = jnp.max(x, axis=0, keepdims=True)
    first = jnp.min(jnp.where(x == m, idx, x.shape[0]), axis=0, keepdims=True)
    return m, first


def _router_kernel(x_ref, wh_ref, wl_ref, rb_ref, eidx_ref, gate_t_ref, rank_ref, cnt_ref, carry_ref, *, tr):
    i = pl.program_id(0)

    @pl.when(i == 0)
    def _():
        carry_ref[...] = jnp.zeros_like(carry_ref)

    x = x_ref[...]
    xh = x.astype(jnp.bfloat16)
    xl = (x - xh.astype(jnp.float32)).astype(jnp.bfloat16)
    nt = (((1,), (1,)), ((), ()))
    wh = wh_ref[...]
    logits = (lax.dot_general(wl_ref[...], xh, nt, preferred_element_type=jnp.float32)
              + lax.dot_general(wh, xl, nt, preferred_element_type=jnp.float32)
              + lax.dot_general(wh, xh, nt, preferred_element_type=jnp.float32))
    scores = _sigmoid(logits)
    sel = scores + rb_ref[...]

    neg = -jnp.inf
    e_idx = lax.broadcasted_iota(jnp.int32, (N_EXPERTS, tr), 0)
    g_idx = lax.broadcasted_iota(jnp.int32, (N_GROUPS, tr), 0)
    in_idx = lax.broadcasted_iota(jnp.int32, (GROUP_SIZE, tr), 0)

    gs_rows = []
    for g in range(N_GROUPS):
        blk = sel[g * GROUP_SIZE:(g + 1) * GROUP_SIZE, :]
        m1, a1 = _first_argmax_rows(blk, in_idx)
        m2 = jnp.max(jnp.where(in_idx == a1, neg, blk), axis=0, keepdims=True)
        gs_rows.append(m1 + m2)
    gs = jnp.concatenate(gs_rows, axis=0)

    gsel = jnp.zeros((N_GROUPS, tr), jnp.float32)
    for _ in range(TOPK_GROUPS):
        _, a = _first_argmax_rows(gs, g_idx)
        pick = g_idx == a
        gsel = jnp.where(pick, 1.0, gsel)
        gs = jnp.where(pick, neg, gs)
    esel = jnp.concatenate(
        [jnp.broadcast_to(gsel[g:g + 1, :], (GROUP_SIZE, tr)) for g in range(N_GROUPS)], axis=0)
    cand = jnp.where(esel > 0.5, sel, neg)

    picks, gates = [], []
    chosen = jnp.zeros((N_EXPERTS, tr), jnp.float32)
    for _ in range(TOP_K):
        _, a = _first_argmax_rows(cand, e_idx)
        pick = e_idx == a
        picks.append(a)
        gates.append(jnp.sum(jnp.where(pick, scores, 0.0), axis=0, keepdims=True))
        chosen = jnp.where(pick, 1.0, chosen)
        cand = jnp.where(pick, neg, cand)
    gate = jnp.concatenate(gates, axis=0)
    gate = gate / jnp.sum(gate, axis=0, keepdims=True) * ROUTE_SCALE

    r_idx = lax.broadcasted_iota(jnp.int32, (tr, tr), 0)
    c_idx = lax.broadcasted_iota(jnp.int32, (tr, tr), 1)
    upper = jnp.where(r_idx <= c_idx, 1.0, 0.0).astype(jnp.bfloat16)
    incl = jnp.dot(chosen.astype(jnp.bfloat16), upper, preferred_element_type=jnp.float32)
    rank_all = carry_ref[...] + incl - chosen
    carry_ref[...] = carry_ref[...] + incl[:, tr - 1:tr]
    ranks = [jnp.sum(jnp.where(e_idx == a, rank_all, 0.0), axis=0, keepdims=True) for a in picks]

    eidx_ref[...] = jnp.concatenate(picks, axis=0)
    rank_ref[...] = jnp.concatenate(ranks, axis=0).astype(jnp.int32)
    pad = jnp.zeros((LANES - TOP_K, tr), jnp.float32)
    gate_t_ref[...] = jnp.concatenate([gate, pad], axis=0).T
    cnt_ref[...] = jnp.broadcast_to(carry_ref[...], (N_EXPERTS, LANES)).astype(jnp.int32)


def _router(x1, w_r, r_bias, tr):
    n_tok, d = x1.shape
    wt = w_r.T
    wh = wt.astype(jnp.bfloat16)
    wl = (wt - wh.astype(jnp.float32)).astype(jnp.bfloat16)
    col = lambda i: (0, i)
    return pl.pallas_call(
        functools.partial(_router_kernel, tr=tr),
        grid=(n_tok // tr,),
        in_specs=[
            pl.BlockSpec((tr, d), lambda i: (i, 0)),
            _const_spec((N_EXPERTS, d)),
            _const_spec((N_EXPERTS, d)),
            _const_spec((N_EXPERTS, 1)),
        ],
        out_specs=[
            pl.BlockSpec((TOP_K, tr), col),
            pl.BlockSpec((tr, LANES), lambda i: (i, 0)),
            pl.BlockSpec((TOP_K, tr), col),
            pl.BlockSpec((N_EXPERTS, LANES), lambda i: (0, 0)),
        ],
        out_shape=(
            jax.ShapeDtypeStruct((TOP_K, n_tok), jnp.int32),
            jax.ShapeDtypeStruct((n_tok, LANES), jnp.float32),
            jax.ShapeDtypeStruct((TOP_K, n_tok), jnp.int32),
            jax.ShapeDtypeStruct((N_EXPERTS, LANES), jnp.int32),
        ),
        scratch_shapes=[pltpu.VMEM((N_EXPERTS, 1), jnp.float32)],
        compiler_params=_params(("arbitrary",), 32 << 20),
        name="router",
    )(x1, wh, wl, r_bias.reshape(N_EXPERTS, 1))


def _dest_kernel(pstart_ref, eidx_ref, rank_ref, dest_ref):
    e = eidx_ref[...]
    base = jnp.zeros(e.shape, jnp.int32)
    for k in range(N_EXPERTS):
        base = jnp.where(e == k, pstart_ref[k], base)
    dest_ref[...] = base + rank_ref[...]


def _dest_rows(pstart, eidx, rank, tc):
    n_tok = eidx.shape[1]
    col = lambda i, ps: (0, i)
    return pl.pallas_call(
        _dest_kernel,
        grid_spec=pltpu.PrefetchScalarGridSpec(
            num_scalar_prefetch=1,
            grid=(n_tok // tc,),
            in_specs=[pl.BlockSpec((TOP_K, tc), col), pl.BlockSpec((TOP_K, tc), col)],
            out_specs=pl.BlockSpec((TOP_K, tc), col),
        ),
        out_shape=jax.ShapeDtypeStruct((TOP_K, n_tok), jnp.int32),
        compiler_params=_params(("arbitrary",), 32 << 20),
        name="dest_rows",
    )(pstart, eidx, rank)


def _dispatch(dest, x1p, n_rows):
    n_tok, width = x1p.shape
    c = SC_DISPATCH_CHUNK
    n_workers = SC_CORES_V7X * SC_SUBCORES_V7X
    n_chunks = n_tok // c
    assert n_tok % c == 0 and n_chunks % n_workers == 0
    per_worker = n_chunks // n_workers
    idx = dest.reshape(TOP_K, n_chunks, c).transpose(1, 0, 2)
    mesh = plsc.VectorSubcoreMesh(core_axis_name="c", subcore_axis_name="s")

    @functools.partial(
        pl.kernel, mesh=mesh,
        out_type=jax.ShapeDtypeStruct((n_rows, width), x1p.dtype),
        scratch_types=[pltpu.VMEM((TOP_K, c), jnp.int32), pltpu.VMEM((c, width), x1p.dtype), pltpu.SemaphoreType.DMA],
        name="dispatch",
    )
    def scatter_rows(idx_hbm, x_hbm, xs_hbm, idx_v, rows_v, sem):
        worker = lax.axis_index("s") * SC_CORES_V7X + lax.axis_index("c")

        @pl.loop(0, per_worker)
        def _(g):
            chunk = worker * per_worker + g
            pltpu.sync_copy(idx_hbm.at[chunk], idx_v)
            pltpu.sync_copy(x_hbm.at[pl.ds(chunk * c, c)], rows_v)
            copies = [pltpu.make_async_copy(rows_v, xs_hbm.at[idx_v.at[j]], sem) for j in range(TOP_K)]
            for cp in copies:
                cp.start()
            for cp in copies:
                cp.wait()

    return scatter_rows(idx, x1p)


def _experts_kernel(be_ref, nv_ref, rows_ref, xs_ref, w1_ref, w3_ref, w2_ref, ys_ref, xb_ref, w1b_ref, w3b_ref,
                    w2b_ref):
    b = pl.program_id(0)

    @pl.when(b < nv_ref[0])
    def _():
        @pl.when((b == 0) | (be_ref[b] != be_ref[jnp.maximum(b - 1, 0)]))
        def _():
            w1b_ref[...] = w1_ref[0].astype(jnp.bfloat16)
            w3b_ref[...] = w3_ref[0].astype(jnp.bfloat16)
            w2b_ref[...] = w2_ref[0].astype(jnp.bfloat16)

        bm, half = xs_ref.shape
        routed = lax.broadcasted_iota(jnp.int32, (bm, half), 0) < rows_ref[b]
        lo, hi = _unpack_bf16_pairs(jnp.where(routed, xs_ref[...], jnp.uint32(0)))
        xb_ref[:, :half] = lo
        xb_ref[:, half:] = hi
        xb = xb_ref[...]
        h1 = jnp.dot(xb, w1b_ref[...], preferred_element_type=jnp.float32)
        h3 = jnp.dot(xb, w3b_ref[...], preferred_element_type=jnp.float32)
        h = (_silu(h1) * h3).astype(jnp.bfloat16)
        ys_ref[...] = _pack_bf16_pairs(jnp.dot(h, w2b_ref[...], preferred_element_type=jnp.float32))

    @pl.when(b >= nv_ref[0])
    def _():
        ys_ref[...] = jnp.zeros_like(ys_ref)


def _experts(blk_exp, n_valid, blk_rows, xs, w1, w3, w2, layer, bm):
    n_rows, half = xs.shape
    d = 2 * half
    de = w1.shape[3]
    n_blk = n_rows // bm
    blk = lambda b, be, nv, br: (jnp.minimum(b, nv[0] - 1), 0)
    wsel = lambda b, be, nv, br: (layer, be[jnp.minimum(b, nv[0] - 1)], 0, 0)
    vmem = (2 * bm * half * 4 + bm * d * 2 + 2 * 3 * d * de * 4 + 3 * d * de * 2 + 2 * bm * half * 4
            + 4 * bm * de * 4 + 2 * bm * d * 4)
    return pl.pallas_call(
        _experts_kernel,
        grid_spec=pltpu.PrefetchScalarGridSpec(
            num_scalar_prefetch=3,
            grid=(n_blk,),
            in_specs=[
                pl.BlockSpec((bm, half), blk),
                pl.BlockSpec((None, 1, d, de), wsel),
                pl.BlockSpec((None, 1, d, de), wsel),
                pl.BlockSpec((None, 1, de, d), wsel),
            ],
            out_specs=pl.BlockSpec((bm, half), lambda b, be, nv, br: (b, 0)),
            scratch_shapes=[pltpu.VMEM((bm, d), jnp.bfloat16), pltpu.VMEM((d, de), jnp.bfloat16),
                            pltpu.VMEM((d, de), jnp.bfloat16), pltpu.VMEM((de, d), jnp.bfloat16)],
        ),
        out_shape=jax.ShapeDtypeStruct((n_rows, half), jnp.uint32),
        compiler_params=_params(("arbitrary",), vmem + (8 << 20)),
        name="experts",
    )(blk_exp, n_valid, blk_rows, xs, w1, w3, w2)


def _gather_rows(idx, ys):
    n_groups, n, c = idx.shape
    width = ys.shape[1]
    n_workers = SC_CORES_V7X * SC_SUBCORES_V7X
    assert n_groups % n_workers == 0
    per_worker = n_groups // n_workers
    mesh = plsc.VectorSubcoreMesh(core_axis_name="c", subcore_axis_name="s")

    @functools.partial(
        pl.kernel, mesh=mesh,
        out_type=jax.ShapeDtypeStruct((n_groups * n * c, width), ys.dtype),
        scratch_types=[pltpu.VMEM((n, c), jnp.int32), pltpu.VMEM((c, width), ys.dtype), pltpu.SemaphoreType.DMA],
        name="combine_gather",
    )
    def gather(idx_hbm, ys_hbm, out_hbm, idx_v, rows_v, sem):
        worker = lax.axis_index("s") * SC_CORES_V7X + lax.axis_index("c")

        @pl.loop(0, per_worker)
        def _(g):
            group = worker * per_worker + g
            pltpu.sync_copy(idx_hbm.at[group], idx_v)
            for j in range(n):
                pltpu.async_copy(ys_hbm.at[idx_v.at[j]], rows_v, sem).wait()
                pltpu.sync_copy(rows_v, out_hbm.at[pl.ds((group * n + j) * c, c)])

    return gather(idx, ys)


def _combine_kernel(*refs, tm, alpha, aliased):
    refs = refs[1:] if aliased else refs
    (dest_ref, dest_next_ref, x1_ref, p_ref, gt_ref, w1_ref, w3_ref, w2_ref, wpg_ref, bpg_ref, wpe_ref, g_ref, b_ref,
     yb_ref, ys_ref, o_ref, ybuf_ref, sems) = refs
    i = pl.program_id(0)
    slot = lax.rem(i, 2)
    other = 1 - slot
    tc_picks = range(SC_GATHER_PICKS, TOP_K)

    def row_copy(d_ref, s, j, r):
        return pltpu.make_async_copy(ys_ref.at[pl.ds(d_ref[j, r], 1), :],
                                     ybuf_ref.at[s, j - SC_GATHER_PICKS, pl.ds(r, 1), :], sems.at[s])

    def drain(s):
        for j in tc_picks:
            pltpu.make_async_copy(ys_ref.at[pl.ds(0, tm), :], ybuf_ref.at[s, j - SC_GATHER_PICKS], sems.at[s]).wait()

    @pl.when(i == 0)
    def _():
        def first_tile(r, carry):
            for j in tc_picks:
                row_copy(dest_ref, 0, j, r).start()
            return carry

        lax.fori_loop(0, tm, first_tile, 0)

    for r in range(tm):
        for j in tc_picks:
            row_copy(dest_next_ref, other, j, r).start()

    x1 = x1_ref[...]
    xb = x1.astype(jnp.bfloat16)
    h1 = jnp.dot(xb, w1_ref[...], preferred_element_type=jnp.float32)
    h3 = jnp.dot(xb, w3_ref[...], preferred_element_type=jnp.float32)
    h = (_silu(h1) * h3).astype(jnp.bfloat16)
    total = alpha * x1 + jnp.dot(h, w2_ref[...], preferred_element_type=jnp.float32)
    gate = _sigmoid(jnp.dot(xb, wpg_ref[...], preferred_element_type=jnp.float32) + bpg_ref[...])
    pe = jnp.dot(p_ref[...].astype(jnp.bfloat16), wpe_ref[...], preferred_element_type=jnp.float32)
    total = total + gate * pe

    drain(slot)
    gt = gt_ref[...]
    half = yb_ref.shape[3]
    moe_lo = jnp.zeros((tm, half), jnp.float32)
    moe_hi = jnp.zeros((tm, half), jnp.float32)
    for j in range(TOP_K):
        w = yb_ref[0, j] if j < SC_GATHER_PICKS else ybuf_ref[slot, j - SC_GATHER_PICKS]
        g = gt[:, j:j + 1]
        moe_lo = moe_lo + g * lax.bitcast_convert_type(w << 16, jnp.float32)
        moe_hi = moe_hi + g * lax.bitcast_convert_type(w & jnp.uint32(0xFFFF0000), jnp.float32)
    total = total + jnp.concatenate([moe_lo, moe_hi], axis=1)
    o_ref[...] = _layer_norm_rows(total, g_ref[...], b_ref[...])

    @pl.when(i == pl.num_programs(0) - 1)
    def _():
        drain(other)


def _combine(dest, x1, p2, gate_t, w_sh1, w_sh3, w_sh2, w_pg, b_pg, w_pe, ln_g, ln_b, yb, ys, alpha, tm, first_tile,
             out_buf):
    n_tok, d = x1.shape
    ds = w_sh1.shape[1]
    dp = p2.shape[1]
    n_tiles = yb.shape[0]
    row = lambda i: (first_tile + i, 0)
    vmem = (2 * TOP_K * tm * (d // 2) * 4 + (3 * d * ds + d * d + dp * d) * 2 + 2 * 2 * tm * d * 4 + 2 * tm * dp * 4
            + 2 * tm * LANES * 4 + 6 * tm * d * 4)
    aliased = out_buf is not None
    in_specs = [
        pl.BlockSpec((TOP_K, tm), lambda i: (0, first_tile + i), memory_space=pltpu.SMEM),
        pl.BlockSpec((TOP_K, tm), lambda i: (0, first_tile + jnp.minimum(i + 1, n_tiles - 1)),
                     memory_space=pltpu.SMEM),
        pl.BlockSpec((tm, d), row),
        pl.BlockSpec((tm, dp), row),
        pl.BlockSpec((tm, LANES), row),
        _const_spec((d, ds)),
        _const_spec((d, ds)),
        _const_spec((ds, d)),
        _const_spec((d, d)),
        _const_spec((1, d)),
        _const_spec((dp, d)),
        _const_spec((1, d)),
        _const_spec((1, d)),
        pl.BlockSpec((1, SC_GATHER_PICKS, tm, d // 2), lambda i: (i, 0, 0, 0)),
        pl.BlockSpec(memory_space=pl.ANY),
    ]
    args = [dest, dest, x1, p2, gate_t, w_sh1, w_sh3, w_sh2, w_pg, b_pg.reshape(1, d), w_pe, ln_g.reshape(1, d),
            ln_b.reshape(1, d), yb, ys]
    if aliased:
        in_specs = [pl.BlockSpec(memory_space=pl.ANY)] + in_specs
        args = [out_buf] + args
    return pl.pallas_call(
        functools.partial(_combine_kernel, tm=tm, alpha=alpha, aliased=aliased),
        grid=(n_tiles,),
        in_specs=in_specs,
        out_specs=pl.BlockSpec((tm, d), row),
        out_shape=jax.ShapeDtypeStruct((n_tok, d), jnp.float32),
        scratch_shapes=[pltpu.VMEM((2, TOP_K - SC_GATHER_PICKS, tm, d // 2), jnp.uint32),
                        pltpu.SemaphoreType.DMA((2,))],
        input_output_aliases={0: 0} if aliased else {},
        compiler_params=_params(("arbitrary",), vmem + (8 << 20)),
        name="combine",
    )(*args)


def _tile(n, target):
    t = min(n, target)
    assert n % t == 0, (n, t)
    return t


def kernel(x, p, w_in, b_f, conv_w, conv_b, conv_ln_g, conv_ln_b, gmlp_ln_g, gmlp_ln_b, w_sp, b_sp, out_g, w_o,
           ln1_g, ln1_b, w_r, r_bias, w_e1, w_e3, w_e2, w_sh1, w_sh3, w_sh2, w_pe, w_pg, b_pg, ln2_g, ln2_b):
    bsz, seq, d = x.shape
    depth = w_in.shape[0]
    n_tok = bsz * seq
    alpha = (2.0 * depth) ** 0.25
    n_main = 2 * CONV_CH + 2 * GMLP_CH + 3 * ATT_CH
    assert w_in.shape[2] == n_main + ATT_HEADS and seq % CHUNK == 0

    bm = MOE_BLOCK
    n_blk = -(-n_tok * TOP_K // bm) + N_EXPERTS
    n_rows = n_blk * bm
    bf16 = jnp.bfloat16

    x2 = x.reshape(n_tok, d)
    for i in range(depth):
        w_main = w_in[i, :, :n_main].astype(bf16)
        w_f = jnp.pad(w_in[i, :, n_main:], ((0, 0), (0, LANES - ATT_HEADS))).astype(bf16)
        b_f_row = jnp.pad(b_f[i], (0, LANES - ATT_HEADS)).reshape(1, LANES)

        zc, zg, q, k, v, f = _in_projection(x2, w_main, w_f, _tile(n_tok, 256))
        ya = _conv_module(zc, conv_w[i], conv_b[i], conv_ln_g[i], conv_ln_b[i], out_g[i, :CONV_CH],
                          bsz, seq, _tile(seq, 256))
        yb = _gmlp_module(zg, gmlp_ln_g[i], gmlp_ln_b[i], w_sp[i], b_sp[i],
                          out_g[i, CONV_CH:CONV_CH + GMLP_CH], _tile(n_tok, 512))
        c = _forget_cumsum(f, b_f_row, bsz, seq, _tile(seq, 512))
        tq = _tile(seq, 2048)
        yc = _fox_attention(q, k, v, c, bsz, seq, tq, _tile(tq, 512), _tile(tq, 512))
        x1, x1p = _out_projection(ya, yb, yc, x2, w_o[i].astype(bf16), out_g[i, CONV_CH + GMLP_CH:],
                                  ln1_g[i], ln1_b[i], alpha, _tile(n_tok, 256))

        eidx, gate_t, rank, cnt = _router(x1, w_r[i], r_bias[i], _tile(n_tok, 512))
        counts = cnt[:, 0]
        padded = (counts + bm - 1) // bm * bm
        pend = jnp.cumsum(padded).astype(jnp.int32)
        pstart = pend - padded
        blk_start = jnp.arange(n_blk, dtype=jnp.int32) * bm
        blk_exp = jnp.minimum(jnp.sum(pend[None, :] <= blk_start[:, None], axis=1), N_EXPERTS - 1).astype(jnp.int32)
        n_valid = (pend[N_EXPERTS - 1:] // bm).astype(jnp.int32)
        blk_rows = jnp.clip((pstart + counts)[blk_exp] - blk_start, 0, bm).astype(jnp.int32)

        dest = _dest_rows(pstart, eidx, rank, _tile(n_tok, 4096))
        xs = _dispatch(dest, x1p, n_rows)
        ys = _experts(blk_exp, n_valid, blk_rows, xs, w_e1, w_e3, w_e2, i, bm)
        tm = _tile(n_tok, 256)
        n_tiles = n_tok // tm
        gather_idx = (dest[:SC_GATHER_PICKS].reshape(SC_GATHER_PICKS, n_tiles, tm).transpose(1, 0, 2)
                      .reshape(-1, SC_GATHER_PICKS, SC_DISPATCH_CHUNK))
        groups = gather_idx.shape[0] // COMBINE_SPLITS
        tiles = n_tiles // COMBINE_SPLITS
        x2 = None
        for part in range(COMBINE_SPLITS):
            yb = _gather_rows(gather_idx[part * groups:(part + 1) * groups], ys)
            x2 = _combine(dest, x1, p[i].reshape(n_tok, -1), gate_t, w_sh1[i].astype(bf16), w_sh3[i].astype(bf16),
                          w_sh2[i].astype(bf16), w_pg[i].astype(bf16), b_pg[i], w_pe[i].astype(bf16),
                          ln2_g[i], ln2_b[i], yb.reshape(tiles, SC_GATHER_PICKS, tm, d // 2), ys, alpha, tm,
                          part * tiles, x2)
    return x2.reshape(bsz, seq, d)
```

```python
import functools

import jax
import jax.numpy as jnp
from jax import lax
from jax.experimental import pallas as pl
from jax.experimental.pallas import tpu as pltpu
from jax.experimental.pallas import tpu_sc as plsc

CONV_CH = 512
CONV_WIDTH = 31
GMLP_HEADS = 4
GMLP_HEAD_CH = 128
GMLP_CH = GMLP_HEADS * GMLP_HEAD_CH
CHUNK = 128
ATT_HEADS = 8
HEAD_DIM = 128
ATT_CH = ATT_HEADS * HEAD_DIM
N_EXPERTS = 64
TOP_K = 8
N_GROUPS = 8
GROUP_SIZE = N_EXPERTS // N_GROUPS
TOPK_GROUPS = 4
ROUTE_SCALE = 2.5
LN_EPS = 1e-5
LOG2E = 1.4426950408889634

LANES = 128
SUBLANES = 8
VMEM_BYTES_V7X = 64 * 1024 * 1024
SC_CORES_V7X = 2
SC_SUBCORES_V7X = 16

CONV_HALO = 32
CONV_ROWS = 32
MOE_BLOCK = 512
SC_DISPATCH_CHUNK = 64
SC_GATHER_PICKS = 4
COMBINE_SPLITS = 4


def _vmem_limit(nbytes):
    return int(min(nbytes, VMEM_BYTES_V7X - 8 * 1024 * 1024))


def _params(semantics, vmem_bytes):
    return pltpu.CompilerParams(dimension_semantics=semantics, vmem_limit_bytes=_vmem_limit(vmem_bytes))


def _const_spec(shape):
    nd = len(shape)
    return pl.BlockSpec(shape, lambda *_: (0,) * nd, pipeline_mode=pl.Buffered(1))


def _layer_norm_rows(x, g, b):
    mu = jnp.mean(x, axis=-1, keepdims=True)
    xc = x - mu
    var = jnp.mean(xc * xc, axis=-1, keepdims=True)
    return xc * lax.rsqrt(var + LN_EPS) * g + b


def _rms_rows(x):
    return x * lax.rsqrt(jnp.mean(x * x, axis=-1, keepdims=True) + LN_EPS)


def _sigmoid(x):
    return 1.0 / (1.0 + jnp.exp(-x))


def _silu(x):
    return x * _sigmoid(x)


def _inproj_kernel(x_ref, w_ref, wf_ref, zc_ref, zg_ref, q_ref, k_ref, v_ref, f_ref, *, q_scale):
    xb = x_ref[...].astype(jnp.bfloat16)
    wide = zc_ref.shape[1]

    def slab(n):
        return jnp.dot(xb, w_ref[:, n * wide:(n + 1) * wide], preferred_element_type=jnp.float32)

    f_ref[...] = jnp.dot(xb, wf_ref[...], preferred_element_type=jnp.float32)
    zc_ref[...] = slab(0)
    zg_ref[...] = slab(1)
    q_ref[...] = (slab(2) * q_scale).astype(jnp.bfloat16)
    k_ref[...] = slab(3).astype(jnp.bfloat16)
    v_ref[...] = slab(4).astype(jnp.bfloat16)


def _in_projection(x2, w_main, w_f, tm):
    n_tok, d = x2.shape
    wide = 2 * CONV_CH
    assert w_main.shape[1] == 5 * wide and ATT_CH == wide and 2 * GMLP_CH == wide
    row = lambda i: (i, 0)
    out_shape = (
        jax.ShapeDtypeStruct((n_tok, wide), jnp.float32),
        jax.ShapeDtypeStruct((n_tok, wide), jnp.float32),
        jax.ShapeDtypeStruct((n_tok, ATT_CH), jnp.bfloat16),
        jax.ShapeDtypeStruct((n_tok, ATT_CH), jnp.bfloat16),
        jax.ShapeDtypeStruct((n_tok, ATT_CH), jnp.bfloat16),
        jax.ShapeDtypeStruct((n_tok, LANES), jnp.float32),
    )
    vmem = (2 * tm * d * 4 + tm * d * 2 + d * 5 * wide * 2 + d * LANES * 2
            + 2 * 2 * tm * wide * 4 + 3 * 2 * tm * wide * 2 + 2 * tm * LANES * 4 + 2 * tm * wide * 4)
    return pl.pallas_call(
        functools.partial(_inproj_kernel, q_scale=HEAD_DIM ** -0.5 * LOG2E),
        grid=(n_tok // tm,),
        in_specs=[
            pl.BlockSpec((tm, d), row),
            _const_spec((d, 5 * wide)),
            _const_spec((d, LANES)),
        ],
        out_specs=[
            pl.BlockSpec((tm, wide), row),
            pl.BlockSpec((tm, wide), row),
            pl.BlockSpec((tm, ATT_CH), row),
            pl.BlockSpec((tm, ATT_CH), row),
            pl.BlockSpec((tm, ATT_CH), row),
            pl.BlockSpec((tm, LANES), row),
        ],
        out_shape=out_shape,
        compiler_params=_params(("arbitrary",), vmem + (8 << 20)),
        name="in_projection",
    )(x2, w_main, w_f)


def _conv_kernel(z_ref, w_ref, cb_ref, lng_ref, lnb_ref, og_ref, o_ref, hbuf_ref, sh_ref, *, ts):
    s = pl.program_id(1)
    span = CONV_HALO + ts - SUBLANES

    @pl.when(s == 0)
    def _():
        hbuf_ref[0:CONV_HALO, :] = jnp.zeros((CONV_HALO, CONV_CH), jnp.float32)

    @pl.when(s > 0)
    def _():
        hbuf_ref[0:CONV_HALO, :] = hbuf_ref[ts:ts + CONV_HALO, :]

    hbuf_ref[CONV_HALO:CONV_HALO + ts, :] = z_ref[:, :CONV_CH] * _sigmoid(z_ref[:, CONV_CH:])

    cb = cb_ref[...]
    lng = lng_ref[...]
    lnb = lnb_ref[...]
    og = og_ref[...]
    for r in range(1, SUBLANES):
        sh_ref[r - 1] = hbuf_ref[r:r + span, :]
    first = CONV_HALO - (CONV_WIDTH - 1)
    for c in range(ts // CONV_ROWS):
        acc = jnp.broadcast_to(cb, (CONV_ROWS, CONV_CH))
        for j in range(CONV_WIDTH):
            off = first + j + c * CONV_ROWS
            r = off % SUBLANES
            a = off - r
            rows = hbuf_ref[a:a + CONV_ROWS, :] if r == 0 else sh_ref[r - 1, a:a + CONV_ROWS, :]
            acc = acc + w_ref[j:j + 1, :] * rows
        y = _silu(_layer_norm_rows(acc, lng, lnb))
        o_ref[c * CONV_ROWS:(c + 1) * CONV_ROWS, :] = (_rms_rows(y) * og).astype(jnp.bfloat16)


def _conv_module(zc, conv_w, conv_b, ln_g, ln_b, og, bsz, seq, ts):
    n_tok = zc.shape[0]
    n_s = seq // ts
    vec = lambda v: v.reshape(1, CONV_CH)
    return pl.pallas_call(
        functools.partial(_conv_kernel, ts=ts),
        grid=(bsz, n_s),
        in_specs=[
            pl.BlockSpec((ts, 2 * CONV_CH), lambda b, s: (b * n_s + s, 0)),
            _const_spec((CONV_WIDTH, CONV_CH)),
            _const_spec((1, CONV_CH)),
            _const_spec((1, CONV_CH)),
            _const_spec((1, CONV_CH)),
            _const_spec((1, CONV_CH)),
        ],
        out_specs=pl.BlockSpec((ts, CONV_CH), lambda b, s: (b * n_s + s, 0)),
        out_shape=jax.ShapeDtypeStruct((n_tok, CONV_CH), jnp.bfloat16),
        scratch_shapes=[pltpu.VMEM((CONV_HALO + ts, CONV_CH), jnp.float32),
                        pltpu.VMEM((SUBLANES - 1, CONV_HALO + ts - SUBLANES, CONV_CH), jnp.float32)],
        compiler_params=_params(("arbitrary", "arbitrary"), 32 << 20),
        name="conv_module",
    )(zc, conv_w, vec(conv_b), vec(ln_g), vec(ln_b), vec(og))


def _gelu_tanh(x):
    c = (2.0 / jnp.pi) ** 0.5
    return 0.5 * x * (1.0 + jnp.tanh(c * (x + 0.044715 * (x * x * x))))


def _gmlp_kernel(z_ref, lng_ref, lnb_ref, wsp_ref, bsp_ref, og_ref, o_ref, y_ref, *, tg):
    z = _gelu_tanh(z_ref[...])
    u = z[:, :GMLP_CH]
    v = _layer_norm_rows(z[:, GMLP_CH:], lng_ref[...], lnb_ref[...]).astype(jnp.bfloat16)
    t_idx = lax.broadcasted_iota(jnp.int32, (CHUNK, CHUNK), 0)
    s_idx = lax.broadcasted_iota(jnp.int32, (CHUNK, CHUNK), 1)
    causal = s_idx <= t_idx
    for h in range(GMLP_HEADS):
        ws = jnp.where(causal, wsp_ref[h], 0.0).astype(jnp.bfloat16)
        bias = bsp_ref[h]
        cols = slice(h * GMLP_HEAD_CH, (h + 1) * GMLP_HEAD_CH)
        for c in range(tg // CHUNK):
            rows = slice(c * CHUNK, (c + 1) * CHUNK)
            mixed = jnp.dot(ws, v[rows, cols], preferred_element_type=jnp.float32) + bias
            y_ref[rows, cols] = u[rows, cols] * mixed
    o_ref[...] = (_rms_rows(y_ref[...]) * og_ref[...]).astype(jnp.bfloat16)


def _gmlp_module(zg, ln_g, ln_b, w_sp, b_sp, og, tg):
    n_tok = zg.shape[0]
    vec = lambda v: v.reshape(1, GMLP_CH)
    return pl.pallas_call(
        functools.partial(_gmlp_kernel, tg=tg),
        grid=(n_tok // tg,),
        in_specs=[
            pl.BlockSpec((tg, 2 * GMLP_CH), lambda i: (i, 0)),
            _const_spec((1, GMLP_CH)),
            _const_spec((1, GMLP_CH)),
            _const_spec((GMLP_HEADS, CHUNK, CHUNK)),
            _const_spec((GMLP_HEADS, CHUNK, 1)),
            _const_spec((1, GMLP_CH)),
        ],
        out_specs=pl.BlockSpec((tg, GMLP_CH), lambda i: (i, 0)),
        out_shape=jax.ShapeDtypeStruct((n_tok, GMLP_CH), jnp.bfloat16),
        scratch_shapes=[pltpu.VMEM((tg, GMLP_CH), jnp.float32)],
        compiler_params=_params(("arbitrary",), 32 << 20),
        name="gmlp_module",
    )(zg, vec(ln_g), vec(ln_b), w_sp, b_sp.reshape(GMLP_HEADS, CHUNK, 1), vec(og))


def _split3(x):
    hi = x.astype(jnp.bfloat16)
    r1 = x - hi.astype(jnp.float32)
    mid = r1.astype(jnp.bfloat16)
    lo = (r1 - mid.astype(jnp.float32)).astype(jnp.bfloat16)
    return hi, mid, lo


def _fcum_kernel(f_ref, bf_ref, c_ref, carry_ref, *, ts):
    s = pl.program_id(1)

    @pl.when(s == 0)
    def _():
        carry_ref[...] = jnp.zeros_like(carry_ref)

    x = f_ref[...] + bf_ref[...]
    ls = -(jnp.maximum(-x, 0.0) + jnp.log(1.0 + jnp.exp(-jnp.abs(x))))
    t_idx = lax.broadcasted_iota(jnp.int32, (ts, ts), 0)
    s_idx = lax.broadcasted_iota(jnp.int32, (ts, ts), 1)
    tri = jnp.where(s_idx <= t_idx, 1.0, 0.0).astype(jnp.bfloat16)
    hi, mid, lo = _split3(ls)
    c = (jnp.dot(tri, lo, preferred_element_type=jnp.float32)
         + jnp.dot(tri, mid, preferred_element_type=jnp.float32)
         + jnp.dot(tri, hi, preferred_element_type=jnp.float32)) + carry_ref[...]
    c_ref[...] = c * LOG2E
    carry_ref[...] = c[ts - 1:ts, :]


def _forget_cumsum(f, b_f_row, bsz, seq, ts):
    n_tok = f.shape[0]
    n_s = seq // ts
    return pl.pallas_call(
        functools.partial(_fcum_kernel, ts=ts),
        grid=(bsz, n_s),
        in_specs=[
            pl.BlockSpec((ts, LANES), lambda b, s: (b * n_s + s, 0)),
            _const_spec((1, LANES)),
        ],
        out_specs=pl.BlockSpec((ts, LANES), lambda b, s: (b * n_s + s, 0)),
        out_shape=jax.ShapeDtypeStruct((n_tok, LANES), jnp.float32),
        scratch_shapes=[pltpu.VMEM((1, LANES), jnp.float32)],
        compiler_params=_params(("arbitrary", "arbitrary"), 32 << 20),
        name="forget_cumsum",
    )(f, b_f_row)


def _bias_columns(c_tile, h, as_query):
    rows = c_tile.shape[0]
    lane = lax.broadcasted_iota(jnp.int32, (rows, LANES), 1)
    col = jnp.sum(jnp.where(lane == h, c_tile, 0.0), axis=1, keepdims=True)
    val = jnp.broadcast_to(col if as_query else -col, (rows, LANES))
    hi, mid, lo = (t.astype(jnp.float32) for t in _split3(val))
    base = 0 if as_query else 3
    out = jnp.where(lane < 6, 1.0, 0.0)
    for offset, term in enumerate((hi, mid, lo)):
        out = jnp.where(lane == base + offset, term, out)
    return out.astype(jnp.bfloat16)


def _fox_kernel(q_ref, k_ref, v_ref, cq_ref, ck_ref, o_ref, kaug_ref, vt_ref, qaug_ref, m_ref, l_ref, acc_ref, s_ref,
                *, tq, tk, cw, seq):
    h = pl.program_id(1)
    i = pl.program_id(2)
    n_chain = tq // cw

    @pl.when(i == 0)
    def _():
        for r in range(seq // tk):
            rows = slice(r * tk, (r + 1) * tk)
            kaug_ref[rows, :HEAD_DIM] = k_ref[rows, :]
            kaug_ref[rows, HEAD_DIM:] = _bias_columns(ck_ref[rows, :], h, as_query=False)
            vt_ref[:, rows] = v_ref[rows, :].astype(jnp.float32).T.astype(jnp.bfloat16)

    qaug_ref[:, :HEAD_DIM] = q_ref[...]
    qaug_ref[:, HEAD_DIM:] = _bias_columns(cq_ref[...], h, as_query=True)
    m_ref[...] = jnp.full(m_ref.shape, -jnp.inf, jnp.float32)
    l_ref[...] = jnp.zeros_like(l_ref)
    acc_ref[...] = jnp.zeros_like(acc_ref)

    def scores(slot, c, k0):
        s_ref[slot] = lax.dot_general(kaug_ref[pl.ds(k0, tk), :], qaug_ref[c * cw:(c + 1) * cw, :],
                                      (((1,), (1,)), ((), ())), preferred_element_type=jnp.float32)

    def absorb(slot, c, k0, key_minus_query):
        s = s_ref[slot]
        if key_minus_query is not None:
            key = lax.broadcasted_iota(jnp.int32, (tk, cw), 0) + key_minus_query
            qry = lax.broadcasted_iota(jnp.int32, (tk, cw), 1)
            s = jnp.where(key <= qry, s, -jnp.inf)
        m_prev = m_ref[c]
        m_new = jnp.maximum(m_prev, jnp.max(s, axis=0, keepdims=True))
        alpha = jnp.exp2(m_prev - m_new)
        p = jnp.exp2(s - m_new)
        l_ref[c] = alpha * l_ref[c] + jnp.sum(p, axis=0, keepdims=True)
        pv = jnp.dot(vt_ref[:, pl.ds(k0, tk)], p.astype(jnp.bfloat16), preferred_element_type=jnp.float32)
        acc_ref[c] = alpha * acc_ref[c] + pv
        m_ref[c] = m_new

    n_kb = tq // tk

    def query_tile_keys(base, masks):
        work = [(kb, c) for kb in range(n_kb) for c in range(n_chain) if masks[kb][c] is not False]
        for slot, (kb, c) in enumerate(work):
            scores(slot, c, pl.multiple_of(base + kb * tk, tk))
        for slot, (kb, c) in enumerate(work):
            absorb(slot, c, pl.multiple_of(base + kb * tk, tk), masks[kb][c])

    def body(j, carry):
        query_tile_keys(j * tq, [[None] * n_chain] * n_kb)
        return carry

    lax.fori_loop(0, i, body, 0)
    masks = []
    for kb in range(n_kb):
        row = []
        for c in range(n_chain):
            first_key, first_query = kb * tk, c * cw
            if first_key + tk - 1 <= first_query:
                row.append(None)
            elif first_key <= first_query + cw - 1:
                row.append(first_key - first_query)
            else:
                row.append(False)
        masks.append(row)
    query_tile_keys(i * tq, masks)
    for c in range(n_chain):
        o_ref[c * cw:(c + 1) * cw, :] = (acc_ref[c] / l_ref[c]).T.astype(jnp.bfloat16)


def _fox_attention(q, k, v, c, bsz, seq, tq, tk, cw):
    n_tok = q.shape[0]
    nq = seq // tq
    n_chain = tq // cw
    return pl.pallas_call(
        functools.partial(_fox_kernel, tq=tq, tk=tk, cw=cw, seq=seq),
        grid=(bsz, ATT_HEADS, nq),
        in_specs=[
            pl.BlockSpec((tq, HEAD_DIM), lambda b, h, i: (b * nq + i, h)),
            pl.BlockSpec((seq, HEAD_DIM), lambda b, h, i: (b, h)),
            pl.BlockSpec((seq, HEAD_DIM), lambda b, h, i: (b, h)),
            pl.BlockSpec((tq, LANES), lambda b, h, i: (b * nq + i, 0)),
            pl.BlockSpec((seq, LANES), lambda b, h, i: (b, 0)),
        ],
        out_specs=pl.BlockSpec((tq, HEAD_DIM), lambda b, h, i: (b * nq + i, h)),
        out_shape=jax.ShapeDtypeStruct((n_tok, ATT_CH), jnp.bfloat16),
        scratch_shapes=[
            pltpu.VMEM((seq, 2 * HEAD_DIM), jnp.bfloat16),
            pltpu.VMEM((HEAD_DIM, seq), jnp.bfloat16),
            pltpu.VMEM((tq, 2 * HEAD_DIM), jnp.bfloat16),
            pltpu.VMEM((n_chain, 1, cw), jnp.float32),
            pltpu.VMEM((n_chain, 1, cw), jnp.float32),
            pltpu.VMEM((n_chain, HEAD_DIM, cw), jnp.float32),
            pltpu.VMEM((n_chain * (tq // tk), tk, cw), jnp.float32),
        ],
        compiler_params=_params(("arbitrary", "arbitrary", "arbitrary"), 56 << 20),
        name="fox_attention",
    )(q, k, v, c, c)


def _pack_bf16_pairs(x):
    n = x.shape[1] // 2
    r = x.astype(jnp.bfloat16).astype(jnp.float32)
    lo = lax.bitcast_convert_type(r[:, :n], jnp.uint32)
    hi = lax.bitcast_convert_type(r[:, n:], jnp.uint32)
    return (lo >> 16) | (hi & jnp.uint32(0xFFFF0000))


def _unpack_bf16_pairs(w):
    lo = lax.bitcast_convert_type(w << 16, jnp.float32).astype(jnp.bfloat16)
    hi = lax.bitcast_convert_type(w & jnp.uint32(0xFFFF0000), jnp.float32).astype(jnp.bfloat16)
    return lo, hi


def _outproj_kernel(ya_ref, yb_ref, yc_ref, x_ref, wo_ref, ogc_ref, g_ref, b_ref, x1_ref, x1p_ref, y_ref, *, alpha):
    y_ref[:, 0:CONV_CH] = ya_ref[...]
    y_ref[:, CONV_CH:CONV_CH + GMLP_CH] = yb_ref[...]
    yc = yc_ref[...].astype(jnp.float32)
    y_ref[:, CONV_CH + GMLP_CH:] = (_rms_rows(yc) * ogc_ref[...]).astype(jnp.bfloat16)
    h = jnp.dot(y_ref[...], wo_ref[...], preferred_element_type=jnp.float32)
    x1 = _layer_norm_rows(alpha * x_ref[...] + h, g_ref[...], b_ref[...])
    x1_ref[...] = x1
    x1p_ref[...] = _pack_bf16_pairs(x1)


def _out_projection(ya, yb, yc, x2, w_o, og_c, ln_g, ln_b, alpha, tm):
    n_tok, d = x2.shape
    mix = w_o.shape[0]
    row = lambda i: (i, 0)
    vmem = (mix * d * 2 + 2 * tm * d * 4 * 2 + 2 * tm * (d // 2) * 4 + 2 * tm * mix * 2 + tm * mix * 2 + 3 * tm * d * 4)
    return pl.pallas_call(
        functools.partial(_outproj_kernel, alpha=alpha),
        grid=(n_tok // tm,),
        in_specs=[
            pl.BlockSpec((tm, CONV_CH), row),
            pl.BlockSpec((tm, GMLP_CH), row),
            pl.BlockSpec((tm, ATT_CH), row),
            pl.BlockSpec((tm, d), row),
            _const_spec((mix, d)),
            _const_spec((1, ATT_CH)),
            _const_spec((1, d)),
            _const_spec((1, d)),
        ],
        out_specs=[pl.BlockSpec((tm, d), row), pl.BlockSpec((tm, d // 2), row)],
        out_shape=(
            jax.ShapeDtypeStruct((n_tok, d), jnp.float32),
            jax.ShapeDtypeStruct((n_tok, d // 2), jnp.uint32),
        ),
        scratch_shapes=[pltpu.VMEM((tm, mix), jnp.bfloat16)],
        compiler_params=_params(("arbitrary",), vmem + (8 << 20)),
        name="out_projection",
    )(ya, yb, yc, x2, w_o, og_c.reshape(1, ATT_CH), ln_g.reshape(1, d), ln_b.reshape(1, d))


def _first_argmax_rows(x, idx):
    m = jnp.max(x, axis=0, keepdims=True)
    first = jnp.min(jnp.where(x == m, idx, x.shape[0]), axis=0, keepdims=True)
    return m, first


def _router_kernel(x_ref, wh_ref, wl_ref, rb_ref, eidx_ref, gate_t_ref, rank_ref, cnt_ref, carry_ref, *, tr):
    i = pl.program_id(0)

    @pl.when(i == 0)
    def _():
        carry_ref[...] = jnp.zeros_like(carry_ref)

    x = x_ref[...]
    xh = x.astype(jnp.bfloat16)
    xl = (x - xh.astype(jnp.float32)).astype(jnp.bfloat16)
    nt = (((1,), (1,)), ((), ()))
    wh = wh_ref[...]
    logits = (lax.dot_general(wl_ref[...], xh, nt, preferred_element_type=jnp.float32)
              + lax.dot_general(wh, xl, nt, preferred_element_type=jnp.float32)
              + lax.dot_general(wh, xh, nt, preferred_element_type=jnp.float32))
    scores = _sigmoid(logits)
    sel = scores + rb_ref[...]

    neg = -jnp.inf
    e_idx = lax.broadcasted_iota(jnp.int32, (N_EXPERTS, tr), 0)
    g_idx = lax.broadcasted_iota(jnp.int32, (N_GROUPS, tr), 0)
    in_idx = lax.broadcasted_iota(jnp.int32, (GROUP_SIZE, tr), 0)

    gs_rows = []
    for g in range(N_GROUPS):
        blk = sel[g * GROUP_SIZE:(g + 1) * GROUP_SIZE, :]
        m1, a1 = _first_argmax_rows(blk, in_idx)
        m2 = jnp.max(jnp.where(in_idx == a1, neg, blk), axis=0, keepdims=True)
        gs_rows.append(m1 + m2)
    gs = jnp.concatenate(gs_rows, axis=0)

    gsel = jnp.zeros((N_GROUPS, tr), jnp.float32)
    for _ in range(TOPK_GROUPS):
        _, a = _first_argmax_rows(gs, g_idx)
        pick = g_idx == a
        gsel = jnp.where(pick, 1.0, gsel)
        gs = jnp.where(pick, neg, gs)
    esel = jnp.concatenate(
        [jnp.broadcast_to(gsel[g:g + 1, :], (GROUP_SIZE, tr)) for g in range(N_GROUPS)], axis=0)
    cand = jnp.where(esel > 0.5, sel, neg)

    picks, gates = [], []
    chosen = jnp.zeros((N_EXPERTS, tr), jnp.float32)
    for _ in range(TOP_K):
        _, a = _first_argmax_rows(cand, e_idx)
        pick = e_idx == a
        picks.append(a)
        gates.append(jnp.sum(jnp.where(pick, scores, 0.0), axis=0, keepdims=True))
        chosen = jnp.where(pick, 1.0, chosen)
        cand = jnp.where(pick, neg, cand)
    gate = jnp.concatenate(gates, axis=0)
    gate = gate / jnp.sum(gate, axis=0, keepdims=True) * ROUTE_SCALE

    r_idx = lax.broadcasted_iota(jnp.int32, (tr, tr), 0)
    c_idx = lax.broadcasted_iota(jnp.int32, (tr, tr), 1)
    upper = jnp.where(r_idx <= c_idx, 1.0, 0.0).astype(jnp.bfloat16)
    incl = jnp.dot(chosen.astype(jnp.bfloat16), upper, preferred_element_type=jnp.float32)
    rank_all = carry_ref[...] + incl - chosen
    carry_ref[...] = carry_ref[...] + incl[:, tr - 1:tr]
    ranks = [jnp.sum(jnp.where(e_idx == a, rank_all, 0.0), axis=0, keepdims=True) for a in picks]

    eidx_ref[...] = jnp.concatenate(picks, axis=0)
    rank_ref[...] = jnp.concatenate(ranks, axis=0).astype(jnp.int32)
    pad = jnp.zeros((LANES - TOP_K, tr), jnp.float32)
    gate_t_ref[...] = jnp.concatenate([gate, pad], axis=0).T
    cnt_ref[...] = jnp.broadcast_to(carry_ref[...], (N_EXPERTS, LANES)).astype(jnp.int32)


def _router(x1, w_r, r_bias, tr):
    n_tok, d = x1.shape
    wt = w_r.T
    wh = wt.astype(jnp.bfloat16)
    wl = (wt - wh.astype(jnp.float32)).astype(jnp.bfloat16)
    col = lambda i: (0, i)
    return pl.pallas_call(
        functools.partial(_router_kernel, tr=tr),
        grid=(n_tok // tr,),
        in_specs=[
            pl.BlockSpec((tr, d), lambda i: (i, 0)),
            _const_spec((N_EXPERTS, d)),
            _const_spec((N_EXPERTS, d)),
            _const_spec((N_EXPERTS, 1)),
        ],
        out_specs=[
            pl.BlockSpec((TOP_K, tr), col),
            pl.BlockSpec((tr, LANES), lambda i: (i, 0)),
            pl.BlockSpec((TOP_K, tr), col),
            pl.BlockSpec((N_EXPERTS, LANES), lambda i: (0, 0)),
        ],
        out_shape=(
            jax.ShapeDtypeStruct((TOP_K, n_tok), jnp.int32),
            jax.ShapeDtypeStruct((n_tok, LANES), jnp.float32),
            jax.ShapeDtypeStruct((TOP_K, n_tok), jnp.int32),
            jax.ShapeDtypeStruct((N_EXPERTS, LANES), jnp.int32),
        ),
        scratch_shapes=[pltpu.VMEM((N_EXPERTS, 1), jnp.float32)],
        compiler_params=_params(("arbitrary",), 32 << 20),
        name="router",
    )(x1, wh, wl, r_bias.reshape(N_EXPERTS, 1))


def _dest_kernel(pstart_ref, eidx_ref, rank_ref, dest_ref):
    e = eidx_ref[...]
    base = jnp.zeros(e.shape, jnp.int32)
    for k in range(N_EXPERTS):
        base = jnp.where(e == k, pstart_ref[k], base)
    dest_ref[...] = base + rank_ref[...]


def _dest_rows(pstart, eidx, rank, tc):
    n_tok = eidx.shape[1]
    col = lambda i, ps: (0, i)
    return pl.pallas_call(
        _dest_kernel,
        grid_spec=pltpu.PrefetchScalarGridSpec(
            num_scalar_prefetch=1,
            grid=(n_tok // tc,),
            in_specs=[pl.BlockSpec((TOP_K, tc), col), pl.BlockSpec((TOP_K, tc), col)],
            out_specs=pl.BlockSpec((TOP_K, tc), col),
        ),
        out_shape=jax.ShapeDtypeStruct((TOP_K, n_tok), jnp.int32),
        compiler_params=_params(("arbitrary",), 32 << 20),
        name="dest_rows",
    )(pstart, eidx, rank)


def _dispatch(dest, x1p, n_rows):
    n_tok, width = x1p.shape
    c = SC_DISPATCH_CHUNK
    n_workers = SC_CORES_V7X * SC_SUBCORES_V7X
    n_chunks = n_tok // c
    assert n_tok % c == 0 and n_chunks % n_workers == 0
    per_worker = n_chunks // n_workers
    idx = dest.reshape(TOP_K, n_chunks, c).transpose(1, 0, 2)
    mesh = plsc.VectorSubcoreMesh(core_axis_name="c", subcore_axis_name="s")

    @functools.partial(
        pl.kernel, mesh=mesh,
        out_type=jax.ShapeDtypeStruct((n_rows, width), x1p.dtype),
        scratch_types=[pltpu.VMEM((TOP_K, c), jnp.int32), pltpu.VMEM((c, width), x1p.dtype), pltpu.SemaphoreType.DMA],
        name="dispatch",
    )
    def scatter_rows(idx_hbm, x_hbm, xs_hbm, idx_v, rows_v, sem):
        worker = lax.axis_index("s") * SC_CORES_V7X + lax.axis_index("c")

        @pl.loop(0, per_worker)
        def _(g):
            chunk = worker * per_worker + g
            pltpu.sync_copy(idx_hbm.at[chunk], idx_v)
            pltpu.sync_copy(x_hbm.at[pl.ds(chunk * c, c)], rows_v)
            copies = [pltpu.make_async_copy(rows_v, xs_hbm.at[idx_v.at[j]], sem) for j in range(TOP_K)]
            for cp in copies:
                cp.start()
            for cp in copies:
                cp.wait()

    return scatter_rows(idx, x1p)


def _experts_kernel(be_ref, nv_ref, rows_ref, xs_ref, w1_ref, w3_ref, w2_ref, ys_ref, xb_ref, w1b_ref, w3b_ref,
                    w2b_ref):
    b = pl.program_id(0)

    @pl.when(b < nv_ref[0])
    def _():
        @pl.when((b == 0) | (be_ref[b] != be_ref[jnp.maximum(b - 1, 0)]))
        def _():
            w1b_ref[...] = w1_ref[0].astype(jnp.bfloat16)
            w3b_ref[...] = w3_ref[0].astype(jnp.bfloat16)
            w2b_ref[...] = w2_ref[0].astype(jnp.bfloat16)

        bm, half = xs_ref.shape
        routed = lax.broadcasted_iota(jnp.int32, (bm, half), 0) < rows_ref[b]
        lo, hi = _unpack_bf16_pairs(jnp.where(routed, xs_ref[...], jnp.uint32(0)))
        xb_ref[:, :half] = lo
        xb_ref[:, half:] = hi
        xb = xb_ref[...]
        h1 = jnp.dot(xb, w1b_ref[...], preferred_element_type=jnp.float32)
        h3 = jnp.dot(xb, w3b_ref[...], preferred_element_type=jnp.float32)
        h = (_silu(h1) * h3).astype(jnp.bfloat16)
        ys_ref[...] = _pack_bf16_pairs(jnp.dot(h, w2b_ref[...], preferred_element_type=jnp.float32))

    @pl.when(b >= nv_ref[0])
    def _():
        ys_ref[...] = jnp.zeros_like(ys_ref)


def _experts(blk_exp, n_valid, blk_rows, xs, w1, w3, w2, layer, bm):
    n_rows, half = xs.shape
    d = 2 * half
    de = w1.shape[3]
    n_blk = n_rows // bm
    blk = lambda b, be, nv, br: (jnp.minimum(b, nv[0] - 1), 0)
    wsel = lambda b, be, nv, br: (layer, be[jnp.minimum(b, nv[0] - 1)], 0, 0)
    vmem = (2 * bm * half * 4 + bm * d * 2 + 2 * 3 * d * de * 4 + 3 * d * de * 2 + 2 * bm * half * 4
            + 4 * bm * de * 4 + 2 * bm * d * 4)
    return pl.pallas_call(
        _experts_kernel,
        grid_spec=pltpu.PrefetchScalarGridSpec(
            num_scalar_prefetch=3,
            grid=(n_blk,),
            in_specs=[
                pl.BlockSpec((bm, half), blk),
                pl.BlockSpec((None, 1, d, de), wsel),
                pl.BlockSpec((None, 1, d, de), wsel),
                pl.BlockSpec((None, 1, de, d), wsel),
            ],
            out_specs=pl.BlockSpec((bm, half), lambda b, be, nv, br: (b, 0)),
            scratch_shapes=[pltpu.VMEM((bm, d), jnp.bfloat16), pltpu.VMEM((d, de), jnp.bfloat16),
                            pltpu.VMEM((d, de), jnp.bfloat16), pltpu.VMEM((de, d), jnp.bfloat16)],
        ),
        out_shape=jax.ShapeDtypeStruct((n_rows, half), jnp.uint32),
        compiler_params=_params(("arbitrary",), vmem + (8 << 20)),
        name="experts",
    )(blk_exp, n_valid, blk_rows, xs, w1, w3, w2)


def _gather_rows(idx, ys):
    n_groups, n, c = idx.shape
    width = ys.shape[1]
    n_workers = SC_CORES_V7X * SC_SUBCORES_V7X
    assert n_groups % n_workers == 0
    per_worker = n_groups // n_workers
    mesh = plsc.VectorSubcoreMesh(core_axis_name="c", subcore_axis_name="s")

    @functools.partial(
        pl.kernel, mesh=mesh,
        out_type=jax.ShapeDtypeStruct((n_groups * n * c, width), ys.dtype),
        scratch_types=[pltpu.VMEM((n, c), jnp.int32), pltpu.VMEM((c, width), ys.dtype), pltpu.SemaphoreType.DMA],
        name="combine_gather",
    )
    def gather(idx_hbm, ys_hbm, out_hbm, idx_v, rows_v, sem):
        worker = lax.axis_index("s") * SC_CORES_V7X + lax.axis_index("c")

        @pl.loop(0, per_worker)
        def _(g):
            group = worker * per_worker + g
            pltpu.sync_copy(idx_hbm.at[group], idx_v)
            for j in range(n):
                pltpu.async_copy(ys_hbm.at[idx_v.at[j]], rows_v, sem).wait()
                pltpu.sync_copy(rows_v, out_hbm.at[pl.ds((group * n + j) * c, c)])

    return gather(idx, ys)


def _combine_kernel(*refs, tm, alpha, aliased):
    refs = refs[1:] if aliased else refs
    (dest_ref, dest_next_ref, x1_ref, p_ref, gt_ref, w1_ref, w3_ref, w2_ref, wpg_ref, bpg_ref, wpe_ref, g_ref, b_ref,
     yb_ref, ys_ref, o_ref, ybuf_ref, sems) = refs
    i = pl.program_id(0)
    slot = lax.rem(i, 2)
    other = 1 - slot
    tc_picks = range(SC_GATHER_PICKS, TOP_K)

    def row_copy(d_ref, s, j, r):
        return pltpu.make_async_copy(ys_ref.at[pl.ds(d_ref[j, r], 1), :],
                                     ybuf_ref.at[s, j - SC_GATHER_PICKS, pl.ds(r, 1), :], sems.at[s])

    def drain(s):
        for j in tc_picks:
            pltpu.make_async_copy(ys_ref.at[pl.ds(0, tm), :], ybuf_ref.at[s, j - SC_GATHER_PICKS], sems.at[s]).wait()

    @pl.when(i == 0)
    def _():
        def first_tile(r, carry):
            for j in tc_picks:
                row_copy(dest_ref, 0, j, r).start()
            return carry

        lax.fori_loop(0, tm, first_tile, 0)

    for r in range(tm):
        for j in tc_picks:
            row_copy(dest_next_ref, other, j, r).start()

    x1 = x1_ref[...]
    xb = x1.astype(jnp.bfloat16)
    h1 = jnp.dot(xb, w1_ref[...], preferred_element_type=jnp.float32)
    h3 = jnp.dot(xb, w3_ref[...], preferred_element_type=jnp.float32)
    h = (_silu(h1) * h3).astype(jnp.bfloat16)
    total = alpha * x1 + jnp.dot(h, w2_ref[...], preferred_element_type=jnp.float32)
    gate = _sigmoid(jnp.dot(xb, wpg_ref[...], preferred_element_type=jnp.float32) + bpg_ref[...])
    pe = jnp.dot(p_ref[...].astype(jnp.bfloat16), wpe_ref[...], preferred_element_type=jnp.float32)
    total = total + gate * pe

    drain(slot)
    gt = gt_ref[...]
    half = yb_ref.shape[3]
    moe_lo = jnp.zeros((tm, half), jnp.float32)
    moe_hi = jnp.zeros((tm, half), jnp.float32)
    for j in range(TOP_K):
        w = yb_ref[0, j] if j < SC_GATHER_PICKS else ybuf_ref[slot, j - SC_GATHER_PICKS]
        g = gt[:, j:j + 1]
        moe_lo = moe_lo + g * lax.bitcast_convert_type(w << 16, jnp.float32)
        moe_hi = moe_hi + g * lax.bitcast_convert_type(w & jnp.uint32(0xFFFF0000), jnp.float32)
    total = total + jnp.concatenate([moe_lo, moe_hi], axis=1)
    o_ref[...] = _layer_norm_rows(total, g_ref[...], b_ref[...])

    @pl.when(i == pl.num_programs(0) - 1)
    def _():
        drain(other)


def _combine(dest, x1, p2, p_tile, gate_t, w_sh1, w_sh3, w_sh2, w_pg, b_pg, w_pe, ln_g, ln_b, yb, ys, alpha, tm,
             first_tile, out_buf):
    n_tok, d = x1.shape
    ds = w_sh1.shape[1]
    dp = p2.shape[1]
    n_tiles = yb.shape[0]
    row = lambda i: (first_tile + i, 0)
    vmem = (2 * TOP_K * tm * (d // 2) * 4 + (3 * d * ds + d * d + dp * d) * 2 + 2 * 2 * tm * d * 4 + 2 * tm * dp * 4
            + 2 * tm * LANES * 4 + 6 * tm * d * 4)
    aliased = out_buf is not None
    in_specs = [
        pl.BlockSpec((TOP_K, tm), lambda i: (0, first_tile + i), memory_space=pltpu.SMEM),
        pl.BlockSpec((TOP_K, tm), lambda i: (0, first_tile + jnp.minimum(i + 1, n_tiles - 1)),
                     memory_space=pltpu.SMEM),
        pl.BlockSpec((tm, d), row),
        pl.BlockSpec((tm, dp), lambda i: (p_tile + first_tile + i, 0)),
        pl.BlockSpec((tm, LANES), row),
        _const_spec((d, ds)),
        _const_spec((d, ds)),
        _const_spec((ds, d)),
        _const_spec((d, d)),
        _const_spec((1, d)),
        _const_spec((dp, d)),
        _const_spec((1, d)),
        _const_spec((1, d)),
        pl.BlockSpec((1, SC_GATHER_PICKS, tm, d // 2), lambda i: (i, 0, 0, 0)),
        pl.BlockSpec(memory_space=pl.ANY),
    ]
    args = [dest, dest, x1, p2, gate_t, w_sh1, w_sh3, w_sh2, w_pg, b_pg.reshape(1, d), w_pe, ln_g.reshape(1, d),
            ln_b.reshape(1, d), yb, ys]
    if aliased:
        in_specs = [pl.BlockSpec(memory_space=pl.ANY)] + in_specs
        args = [out_buf] + args
    return pl.pallas_call(
        functools.partial(_combine_kernel, tm=tm, alpha=alpha, aliased=aliased),
        grid=(n_tiles,),
        in_specs=in_specs,
        out_specs=pl.BlockSpec((tm, d), row),
        out_shape=jax.ShapeDtypeStruct((n_tok, d), jnp.float32),
        scratch_shapes=[pltpu.VMEM((2, TOP_K - SC_GATHER_PICKS, tm, d // 2), jnp.uint32),
                        pltpu.SemaphoreType.DMA((2,))],
        input_output_aliases={0: 0} if aliased else {},
        compiler_params=_params(("arbitrary",), vmem + (8 << 20)),
        name="combine",
    )(*args)


def _tile(n, target):
    t = min(n, target)
    assert n % t == 0, (n, t)
    return t


def kernel(x, p, w_in, b_f, conv_w, conv_b, conv_ln_g, conv_ln_b, gmlp_ln_g, gmlp_ln_b, w_sp, b_sp, out_g, w_o,
           ln1_g, ln1_b, w_r, r_bias, w_e1, w_e3, w_e2, w_sh1, w_sh3, w_sh2, w_pe, w_pg, b_pg, ln2_g, ln2_b):
    bsz, seq, d = x.shape
    depth = w_in.shape[0]
    n_tok = bsz * seq
    alpha = (2.0 * depth) ** 0.25
    n_main = 2 * CONV_CH + 2 * GMLP_CH + 3 * ATT_CH
    assert w_in.shape[2] == n_main + ATT_HEADS and seq % CHUNK == 0

    bm = MOE_BLOCK
    n_blk = -(-n_tok * TOP_K // bm) + N_EXPERTS
    n_rows = n_blk * bm
    bf16 = jnp.bfloat16

    x2 = x.reshape(n_tok, d)
    for i in range(depth):
        w_main = w_in[i, :, :n_main].astype(bf16)
        w_f = jnp.pad(w_in[i, :, n_main:], ((0, 0), (0, LANES - ATT_HEADS))).astype(bf16)
        b_f_row = jnp.pad(b_f[i], (0, LANES - ATT_HEADS)).reshape(1, LANES)

        zc, zg, q, k, v, f = _in_projection(x2, w_main, w_f, _tile(n_tok, 256))
        ya = _conv_module(zc, conv_w[i], conv_b[i], conv_ln_g[i], conv_ln_b[i], out_g[i, :CONV_CH],
                          bsz, seq, _tile(seq, 256))
        yb = _gmlp_module(zg, gmlp_ln_g[i], gmlp_ln_b[i], w_sp[i], b_sp[i],
                          out_g[i, CONV_CH:CONV_CH + GMLP_CH], _tile(n_tok, 512))
        c = _forget_cumsum(f, b_f_row, bsz, seq, _tile(seq, 512))
        tq = _tile(seq, 2048)
        yc = _fox_attention(q, k, v, c, bsz, seq, tq, _tile(tq, 512), _tile(tq, 512))
        x1, x1p = _out_projection(ya, yb, yc, x2, w_o[i].astype(bf16), out_g[i, CONV_CH + GMLP_CH:],
                                  ln1_g[i], ln1_b[i], alpha, _tile(n_tok, 256))

        eidx, gate_t, rank, cnt = _router(x1, w_r[i], r_bias[i], _tile(n_tok, 512))
        counts = cnt[:, 0]
        padded = (counts + bm - 1) // bm * bm
        pend = jnp.cumsum(padded).astype(jnp.int32)
        pstart = pend - padded
        blk_start = jnp.arange(n_blk, dtype=jnp.int32) * bm
        blk_exp = jnp.minimum(jnp.sum(pend[None, :] <= blk_start[:, None], axis=1), N_EXPERTS - 1).astype(jnp.int32)
        n_valid = (pend[N_EXPERTS - 1:] // bm).astype(jnp.int32)
        blk_rows = jnp.clip((pstart + counts)[blk_exp] - blk_start, 0, bm).astype(jnp.int32)

        dest = _dest_rows(pstart, eidx, rank, _tile(n_tok, 4096))
        xs = _dispatch(dest, x1p, n_rows)
        ys = _experts(blk_exp, n_valid, blk_rows, xs, w_e1, w_e3, w_e2, i, bm)
        tm = _tile(n_tok, 256)
        n_tiles = n_tok // tm
        gather_idx = (dest[:SC_GATHER_PICKS].reshape(SC_GATHER_PICKS, n_tiles, tm).transpose(1, 0, 2)
                      .reshape(-1, SC_GATHER_PICKS, SC_DISPATCH_CHUNK))
        groups = gather_idx.shape[0] // COMBINE_SPLITS
        tiles = n_tiles // COMBINE_SPLITS
        x2 = None
        for part in range(COMBINE_SPLITS):
            yb = _gather_rows(gather_idx[part * groups:(part + 1) * groups], ys)
            x2 = _combine(dest, x1, p.reshape(depth * n_tok, -1), i * n_tiles, gate_t, w_sh1[i].astype(bf16),
                          w_sh3[i].astype(bf16),
                          w_sh2[i].astype(bf16), w_pg[i].astype(bf16), b_pg[i], w_pe[i].astype(bf16),
                          ln2_g[i], ln2_b[i], yb.reshape(tiles, SC_GATHER_PICKS, tm, d // 2), ys, alpha, tm,
                          part * tiles, x2)
    return x2.reshape(bsz, seq, d)
```

```python
import functools

import jax
import jax.numpy as jnp
from jax import lax
from jax.experimental import pallas as pl
from jax.experimental.pallas import tpu as pltpu
from jax.experimental.pallas import tpu_sc as plsc

CONV_CH = 512
CONV_WIDTH = 31
GMLP_HEADS = 4
GMLP_HEAD_CH = 128
GMLP_CH = GMLP_HEADS * GMLP_HEAD_CH
CHUNK = 128
ATT_HEADS = 8
HEAD_DIM = 128
ATT_CH = ATT_HEADS * HEAD_DIM
N_EXPERTS = 64
TOP_K = 8
N_GROUPS = 8
GROUP_SIZE = N_EXPERTS // N_GROUPS
TOPK_GROUPS = 4
ROUTE_SCALE = 2.5
LN_EPS = 1e-5
LOG2E = 1.4426950408889634

LANES = 128
SUBLANES = 8
VMEM_BYTES_V7X = 64 * 1024 * 1024
SC_CORES_V7X = 2
SC_SUBCORES_V7X = 16

CONV_HALO = 32
CONV_ROWS = 32
MOE_BLOCK = 512
SC_DISPATCH_CHUNK = 64
SC_GATHER_PICKS = 4
COMBINE_SPLITS = 4


def _vmem_limit(nbytes):
    return int(min(nbytes, VMEM_BYTES_V7X - 8 * 1024 * 1024))


def _params(semantics, vmem_bytes):
    return pltpu.CompilerParams(dimension_semantics=semantics, vmem_limit_bytes=_vmem_limit(vmem_bytes))


def _const_spec(shape):
    nd = len(shape)
    return pl.BlockSpec(shape, lambda *_: (0,) * nd, pipeline_mode=pl.Buffered(1))


def _layer_norm_rows(x, g, b):
    mu = jnp.mean(x, axis=-1, keepdims=True)
    xc = x - mu
    var = jnp.mean(xc * xc, axis=-1, keepdims=True)
    return xc * lax.rsqrt(var + LN_EPS) * g + b


def _rms_rows(x):
    return x * lax.rsqrt(jnp.mean(x * x, axis=-1, keepdims=True) + LN_EPS)


def _sigmoid(x):
    return 1.0 / (1.0 + jnp.exp(-x))


def _silu(x):
    return x * _sigmoid(x)


def _inproj_kernel(x_ref, w_ref, wf_ref, zc_ref, zg_ref, q_ref, k_ref, v_ref, f_ref, *, q_scale):
    xb = x_ref[...].astype(jnp.bfloat16)
    wide = zc_ref.shape[1]

    def slab(n):
        return jnp.dot(xb, w_ref[:, n * wide:(n + 1) * wide], preferred_element_type=jnp.float32)

    f_ref[...] = jnp.dot(xb, wf_ref[...], preferred_element_type=jnp.float32)
    zc_ref[...] = slab(0)
    zg_ref[...] = slab(1)
    q_ref[...] = (slab(2) * q_scale).astype(jnp.bfloat16)
    k_ref[...] = slab(3).astype(jnp.bfloat16)
    v_ref[...] = slab(4).astype(jnp.bfloat16)


def _in_projection(x2, w_main, w_f, tm):
    n_tok, d = x2.shape
    wide = 2 * CONV_CH
    assert w_main.shape[1] == 5 * wide and ATT_CH == wide and 2 * GMLP_CH == wide
    row = lambda i: (i, 0)
    out_shape = (
        jax.ShapeDtypeStruct((n_tok, wide), jnp.float32),
        jax.ShapeDtypeStruct((n_tok, wide), jnp.float32),
        jax.ShapeDtypeStruct((n_tok, ATT_CH), jnp.bfloat16),
        jax.ShapeDtypeStruct((n_tok, ATT_CH), jnp.bfloat16),
        jax.ShapeDtypeStruct((n_tok, ATT_CH), jnp.bfloat16),
        jax.ShapeDtypeStruct((n_tok, LANES), jnp.float32),
    )
    vmem = (2 * tm * d * 4 + tm * d * 2 + d * 5 * wide * 2 + d * LANES * 2
            + 2 * 2 * tm * wide * 4 + 3 * 2 * tm * wide * 2 + 2 * tm * LANES * 4 + 2 * tm * wide * 4)
    return pl.pallas_call(
        functools.partial(_inproj_kernel, q_scale=HEAD_DIM ** -0.5 * LOG2E),
        grid=(n_tok // tm,),
        in_specs=[
            pl.BlockSpec((tm, d), row),
            _const_spec((d, 5 * wide)),
            _const_spec((d, LANES)),
        ],
        out_specs=[
            pl.BlockSpec((tm, wide), row),
            pl.BlockSpec((tm, wide), row),
            pl.BlockSpec((tm, ATT_CH), row),
            pl.BlockSpec((tm, ATT_CH), row),
            pl.BlockSpec((tm, ATT_CH), row),
            pl.BlockSpec((tm, LANES), row),
        ],
        out_shape=out_shape,
        compiler_params=_params(("arbitrary",), vmem + (8 << 20)),
        name="in_projection",
    )(x2, w_main, w_f)


def _conv_kernel(z_ref, w_ref, cb_ref, lng_ref, lnb_ref, og_ref, o_ref, hbuf_ref, sh_ref, *, ts):
    s = pl.program_id(1)
    span = CONV_HALO + ts - SUBLANES

    @pl.when(s == 0)
    def _():
        hbuf_ref[0:CONV_HALO, :] = jnp.zeros((CONV_HALO, CONV_CH), jnp.float32)

    @pl.when(s > 0)
    def _():
        hbuf_ref[0:CONV_HALO, :] = hbuf_ref[ts:ts + CONV_HALO, :]

    hbuf_ref[CONV_HALO:CONV_HALO + ts, :] = z_ref[:, :CONV_CH] * _sigmoid(z_ref[:, CONV_CH:])

    cb = cb_ref[...]
    lng = lng_ref[...]
    lnb = lnb_ref[...]
    og = og_ref[...]
    for r in range(1, SUBLANES):
        sh_ref[r - 1] = hbuf_ref[r:r + span, :]
    first = CONV_HALO - (CONV_WIDTH - 1)
    for c in range(ts // CONV_ROWS):
        acc = jnp.broadcast_to(cb, (CONV_ROWS, CONV_CH))
        for j in range(CONV_WIDTH):
            off = first + j + c * CONV_ROWS
            r = off % SUBLANES
            a = off - r
            rows = hbuf_ref[a:a + CONV_ROWS, :] if r == 0 else sh_ref[r - 1, a:a + CONV_ROWS, :]
            acc = acc + w_ref[j:j + 1, :] * rows
        y = _silu(_layer_norm_rows(acc, lng, lnb))
        o_ref[c * CONV_ROWS:(c + 1) * CONV_ROWS, :] = (_rms_rows(y) * og).astype(jnp.bfloat16)


def _conv_module(zc, conv_w, conv_b, ln_g, ln_b, og, bsz, seq, ts):
    n_tok = zc.shape[0]
    n_s = seq // ts
    vec = lambda v: v.reshape(1, CONV_CH)
    return pl.pallas_call(
        functools.partial(_conv_kernel, ts=ts),
        grid=(bsz, n_s),
        in_specs=[
            pl.BlockSpec((ts, 2 * CONV_CH), lambda b, s: (b * n_s + s, 0)),
            _const_spec((CONV_WIDTH, CONV_CH)),
            _const_spec((1, CONV_CH)),
            _const_spec((1, CONV_CH)),
            _const_spec((1, CONV_CH)),
            _const_spec((1, CONV_CH)),
        ],
        out_specs=pl.BlockSpec((ts, CONV_CH), lambda b, s: (b * n_s + s, 0)),
        out_shape=jax.ShapeDtypeStruct((n_tok, CONV_CH), jnp.bfloat16),
        scratch_shapes=[pltpu.VMEM((CONV_HALO + ts, CONV_CH), jnp.float32),
                        pltpu.VMEM((SUBLANES - 1, CONV_HALO + ts - SUBLANES, CONV_CH), jnp.float32)],
        compiler_params=_params(("arbitrary", "arbitrary"), 32 << 20),
        name="conv_module",
    )(zc, conv_w, vec(conv_b), vec(ln_g), vec(ln_b), vec(og))


def _gelu_tanh(x):
    c = (2.0 / jnp.pi) ** 0.5
    return 0.5 * x * (1.0 + jnp.tanh(c * (x + 0.044715 * (x * x * x))))


def _gmlp_kernel(z_ref, lng_ref, lnb_ref, wsp_ref, bsp_ref, og_ref, o_ref, y_ref, *, tg):
    z = _gelu_tanh(z_ref[...])
    u = z[:, :GMLP_CH]
    v = _layer_norm_rows(z[:, GMLP_CH:], lng_ref[...], lnb_ref[...]).astype(jnp.bfloat16)
    t_idx = lax.broadcasted_iota(jnp.int32, (CHUNK, CHUNK), 0)
    s_idx = lax.broadcasted_iota(jnp.int32, (CHUNK, CHUNK), 1)
    causal = s_idx <= t_idx
    for h in range(GMLP_HEADS):
        ws = jnp.where(causal, wsp_ref[h], 0.0).astype(jnp.bfloat16)
        bias = bsp_ref[h]
        cols = slice(h * GMLP_HEAD_CH, (h + 1) * GMLP_HEAD_CH)
        for c in range(tg // CHUNK):
            rows = slice(c * CHUNK, (c + 1) * CHUNK)
            mixed = jnp.dot(ws, v[rows, cols], preferred_element_type=jnp.float32) + bias
            y_ref[rows, cols] = u[rows, cols] * mixed
    o_ref[...] = (_rms_rows(y_ref[...]) * og_ref[...]).astype(jnp.bfloat16)


def _gmlp_module(zg, ln_g, ln_b, w_sp, b_sp, og, tg):
    n_tok = zg.shape[0]
    vec = lambda v: v.reshape(1, GMLP_CH)
    return pl.pallas_call(
        functools.partial(_gmlp_kernel, tg=tg),
        grid=(n_tok // tg,),
        in_specs=[
            pl.BlockSpec((tg, 2 * GMLP_CH), lambda i: (i, 0)),
            _const_spec((1, GMLP_CH)),
            _const_spec((1, GMLP_CH)),
            _const_spec((GMLP_HEADS, CHUNK, CHUNK)),
            _const_spec((GMLP_HEADS, CHUNK, 1)),
            _const_spec((1, GMLP_CH)),
        ],
        out_specs=pl.BlockSpec((tg, GMLP_CH), lambda i: (i, 0)),
        out_shape=jax.ShapeDtypeStruct((n_tok, GMLP_CH), jnp.bfloat16),
        scratch_shapes=[pltpu.VMEM((tg, GMLP_CH), jnp.float32)],
        compiler_params=_params(("arbitrary",), 32 << 20),
        name="gmlp_module",
    )(zg, vec(ln_g), vec(ln_b), w_sp, b_sp.reshape(GMLP_HEADS, CHUNK, 1), vec(og))


def _split3(x):
    hi = x.astype(jnp.bfloat16)
    r1 = x - hi.astype(jnp.float32)
    mid = r1.astype(jnp.bfloat16)
    lo = (r1 - mid.astype(jnp.float32)).astype(jnp.bfloat16)
    return hi, mid, lo


def _fcum_kernel(f_ref, bf_ref, c_ref, carry_ref, *, ts):
    s = pl.program_id(1)

    @pl.when(s == 0)
    def _():
        carry_ref[...] = jnp.zeros_like(carry_ref)

    x = f_ref[...] + bf_ref[...]
    ls = -(jnp.maximum(-x, 0.0) + jnp.log(1.0 + jnp.exp(-jnp.abs(x))))
    t_idx = lax.broadcasted_iota(jnp.int32, (ts, ts), 0)
    s_idx = lax.broadcasted_iota(jnp.int32, (ts, ts), 1)
    tri = jnp.where(s_idx <= t_idx, 1.0, 0.0).astype(jnp.bfloat16)
    hi, mid, lo = _split3(ls)
    c = (jnp.dot(tri, lo, preferred_element_type=jnp.float32)
         + jnp.dot(tri, mid, preferred_element_type=jnp.float32)
         + jnp.dot(tri, hi, preferred_element_type=jnp.float32)) + carry_ref[...]
    c_ref[...] = c * LOG2E
    carry_ref[...] = c[ts - 1:ts, :]


def _forget_cumsum(f, b_f_row, bsz, seq, ts):
    n_tok = f.shape[0]
    n_s = seq // ts
    return pl.pallas_call(
        functools.partial(_fcum_kernel, ts=ts),
        grid=(bsz, n_s),
        in_specs=[
            pl.BlockSpec((ts, LANES), lambda b, s: (b * n_s + s, 0)),
            _const_spec((1, LANES)),
        ],
        out_specs=pl.BlockSpec((ts, LANES), lambda b, s: (b * n_s + s, 0)),
        out_shape=jax.ShapeDtypeStruct((n_tok, LANES), jnp.float32),
        scratch_shapes=[pltpu.VMEM((1, LANES), jnp.float32)],
        compiler_params=_params(("arbitrary", "arbitrary"), 32 << 20),
        name="forget_cumsum",
    )(f, b_f_row)


def _bias_columns(c_tile, h, as_query):
    rows = c_tile.shape[0]
    lane = lax.broadcasted_iota(jnp.int32, (rows, LANES), 1)
    col = jnp.sum(jnp.where(lane == h, c_tile, 0.0), axis=1, keepdims=True)
    val = jnp.broadcast_to(col if as_query else -col, (rows, LANES))
    hi, mid, lo = (t.astype(jnp.float32) for t in _split3(val))
    base = 0 if as_query else 3
    out = jnp.where(lane < 6, 1.0, 0.0)
    for offset, term in enumerate((hi, mid, lo)):
        out = jnp.where(lane == base + offset, term, out)
    return out.astype(jnp.bfloat16)


def _fox_kernel(q_ref, k_ref, v_ref, cq_ref, ck_ref, o_ref, kaug_ref, vt_ref, qaug_ref, m_ref, l_ref, acc_ref, s_ref,
                *, tq, tk, cw, seq):
    h = pl.program_id(1)
    i = pl.program_id(2)
    n_chain = tq // cw

    @pl.when(i == 0)
    def _():
        for r in range(seq // tk):
            rows = slice(r * tk, (r + 1) * tk)
            kaug_ref[rows, :HEAD_DIM] = k_ref[rows, :]
            kaug_ref[rows, HEAD_DIM:] = _bias_columns(ck_ref[rows, :], h, as_query=False)
            vt_ref[:, rows] = v_ref[rows, :].astype(jnp.float32).T.astype(jnp.bfloat16)

    qaug_ref[:, :HEAD_DIM] = q_ref[...]
    qaug_ref[:, HEAD_DIM:] = _bias_columns(cq_ref[...], h, as_query=True)
    m_ref[...] = jnp.full(m_ref.shape, -jnp.inf, jnp.float32)
    l_ref[...] = jnp.zeros_like(l_ref)
    acc_ref[...] = jnp.zeros_like(acc_ref)

    def scores(slot, c, k0):
        s_ref[slot] = lax.dot_general(kaug_ref[pl.ds(k0, tk), :], qaug_ref[c * cw:(c + 1) * cw, :],
                                      (((1,), (1,)), ((), ())), preferred_element_type=jnp.float32)

    def absorb(slot, c, k0, key_minus_query):
        s = s_ref[slot]
        if key_minus_query is not None:
            key = lax.broadcasted_iota(jnp.int32, (tk, cw), 0) + key_minus_query
            qry = lax.broadcasted_iota(jnp.int32, (tk, cw), 1)
            s = jnp.where(key <= qry, s, -jnp.inf)
        m_prev = m_ref[c]
        m_new = jnp.maximum(m_prev, jnp.max(s, axis=0, keepdims=True))
        alpha = jnp.exp2(m_prev - m_new)
        p = jnp.exp2(s - m_new)
        l_ref[c] = alpha * l_ref[c] + jnp.sum(p, axis=0, keepdims=True)
        pv = jnp.dot(vt_ref[:, pl.ds(k0, tk)], p.astype(jnp.bfloat16), preferred_element_type=jnp.float32)
        acc_ref[c] = alpha * acc_ref[c] + pv
        m_ref[c] = m_new

    n_kb = tq // tk

    def query_tile_keys(base, masks):
        work = [(kb, c) for kb in range(n_kb) for c in range(n_chain) if masks[kb][c] is not False]
        for slot, (kb, c) in enumerate(work):
            scores(slot, c, pl.multiple_of(base + kb * tk, tk))
        for slot, (kb, c) in enumerate(work):
            absorb(slot, c, pl.multiple_of(base + kb * tk, tk), masks[kb][c])

    def body(j, carry):
        query_tile_keys(j * tq, [[None] * n_chain] * n_kb)
        return carry

    lax.fori_loop(0, i, body, 0)
    masks = []
    for kb in range(n_kb):
        row = []
        for c in range(n_chain):
            first_key, first_query = kb * tk, c * cw
            if first_key + tk - 1 <= first_query:
                row.append(None)
            elif first_key <= first_query + cw - 1:
                row.append(first_key - first_query)
            else:
                row.append(False)
        masks.append(row)
    query_tile_keys(i * tq, masks)
    for c in range(n_chain):
        o_ref[c * cw:(c + 1) * cw, :] = (acc_ref[c] / l_ref[c]).T.astype(jnp.bfloat16)


def _fox_attention(q, k, v, c, bsz, seq, tq, tk, cw):
    n_tok = q.shape[0]
    nq = seq // tq
    n_chain = tq // cw
    return pl.pallas_call(
        functools.partial(_fox_kernel, tq=tq, tk=tk, cw=cw, seq=seq),
        grid=(bsz, ATT_HEADS, nq),
        in_specs=[
            pl.BlockSpec((tq, HEAD_DIM), lambda b, h, i: (b * nq + i, h)),
            pl.BlockSpec((seq, HEAD_DIM), lambda b, h, i: (b, h)),
            pl.BlockSpec((seq, HEAD_DIM), lambda b, h, i: (b, h)),
            pl.BlockSpec((tq, LANES), lambda b, h, i: (b * nq + i, 0)),
            pl.BlockSpec((seq, LANES), lambda b, h, i: (b, 0)),
        ],
        out_specs=pl.BlockSpec((tq, HEAD_DIM), lambda b, h, i: (b * nq + i, h)),
        out_shape=jax.ShapeDtypeStruct((n_tok, ATT_CH), jnp.bfloat16),
        scratch_shapes=[
            pltpu.VMEM((seq, 2 * HEAD_DIM), jnp.bfloat16),
            pltpu.VMEM((HEAD_DIM, seq), jnp.bfloat16),
            pltpu.VMEM((tq, 2 * HEAD_DIM), jnp.bfloat16),
            pltpu.VMEM((n_chain, 1, cw), jnp.float32),
            pltpu.VMEM((n_chain, 1, cw), jnp.float32),
            pltpu.VMEM((n_chain, HEAD_DIM, cw), jnp.float32),
            pltpu.VMEM((n_chain * (tq // tk), tk, cw), jnp.float32),
        ],
        compiler_params=_params(("arbitrary", "arbitrary", "arbitrary"), 56 << 20),
        name="fox_attention",
    )(q, k, v, c, c)


def _pack_bf16_pairs(x):
    n = x.shape[1] // 2
    r = x.astype(jnp.bfloat16).astype(jnp.float32)
    lo = lax.bitcast_convert_type(r[:, :n], jnp.uint32)
    hi = lax.bitcast_convert_type(r[:, n:], jnp.uint32)
    return (lo >> 16) | (hi & jnp.uint32(0xFFFF0000))


def _unpack_bf16_pairs(w):
    lo = lax.bitcast_convert_type(w << 16, jnp.float32).astype(jnp.bfloat16)
    hi = lax.bitcast_convert_type(w & jnp.uint32(0xFFFF0000), jnp.float32).astype(jnp.bfloat16)
    return lo, hi


def _outproj_kernel(ya_ref, yb_ref, yc_ref, x_ref, wo_ref, ogc_ref, g_ref, b_ref, x1_ref, x1p_ref, y_ref, *, alpha):
    y_ref[:, 0:CONV_CH] = ya_ref[...]
    y_ref[:, CONV_CH:CONV_CH + GMLP_CH] = yb_ref[...]
    yc = yc_ref[...].astype(jnp.float32)
    y_ref[:, CONV_CH + GMLP_CH:] = (_rms_rows(yc) * ogc_ref[...]).astype(jnp.bfloat16)
    h = jnp.dot(y_ref[...], wo_ref[...], preferred_element_type=jnp.float32)
    x1 = _layer_norm_rows(alpha * x_ref[...] + h, g_ref[...], b_ref[...])
    x1_ref[...] = x1
    x1p_ref[...] = _pack_bf16_pairs(x1)


def _out_projection(ya, yb, yc, x2, w_o, og_c, ln_g, ln_b, alpha, tm):
    n_tok, d = x2.shape
    mix = w_o.shape[0]
    row = lambda i: (i, 0)
    vmem = (mix * d * 2 + 2 * tm * d * 4 * 2 + 2 * tm * (d // 2) * 4 + 2 * tm * mix * 2 + tm * mix * 2 + 3 * tm * d * 4)
    return pl.pallas_call(
        functools.partial(_outproj_kernel, alpha=alpha),
        grid=(n_tok // tm,),
        in_specs=[
            pl.BlockSpec((tm, CONV_CH), row),
            pl.BlockSpec((tm, GMLP_CH), row),
            pl.BlockSpec((tm, ATT_CH), row),
            pl.BlockSpec((tm, d), row),
            _const_spec((mix, d)),
            _const_spec((1, ATT_CH)),
            _const_spec((1, d)),
            _const_spec((1, d)),
        ],
        out_specs=[pl.BlockSpec((tm, d), row), pl.BlockSpec((tm, d // 2), row)],
        out_shape=(
            jax.ShapeDtypeStruct((n_tok, d), jnp.float32),
            jax.ShapeDtypeStruct((n_tok, d // 2), jnp.uint32),
        ),
        scratch_shapes=[pltpu.VMEM((tm, mix), jnp.bfloat16)],
        compiler_params=_params(("arbitrary",), vmem + (8 << 20)),
        name="out_projection",
    )(ya, yb, yc, x2, w_o, og_c.reshape(1, ATT_CH), ln_g.reshape(1, d), ln_b.reshape(1, d))


def _first_argmax_rows(x, idx):
    m = jnp.max(x, axis=0, keepdims=True)
    first = jnp.min(jnp.where(x == m, idx, x.shape[0]), axis=0, keepdims=True)
    return m, first


def _router_kernel(x_ref, wh_ref, wl_ref, rb_ref, eidx_ref, gate_t_ref, rank_ref, cnt_ref, carry_ref, *, tr):
    i = pl.program_id(0)

    @pl.when(i == 0)
    def _():
        carry_ref[...] = jnp.zeros_like(carry_ref)

    x = x_ref[...]
    xh = x.astype(jnp.bfloat16)
    xl = (x - xh.astype(jnp.float32)).astype(jnp.bfloat16)
    nt = (((1,), (1,)), ((), ()))
    wh = wh_ref[...]
    logits = (lax.dot_general(wl_ref[...], xh, nt, preferred_element_type=jnp.float32)
              + lax.dot_general(wh, xl, nt, preferred_element_type=jnp.float32)
              + lax.dot_general(wh, xh, nt, preferred_element_type=jnp.float32))
    scores = _sigmoid(logits)
    sel = scores + rb_ref[...]

    neg = -jnp.inf
    e_idx = lax.broadcasted_iota(jnp.int32, (N_EXPERTS, tr), 0)
    g_idx = lax.broadcasted_iota(jnp.int32, (N_GROUPS, tr), 0)
    in_idx = lax.broadcasted_iota(jnp.int32, (GROUP_SIZE, tr), 0)

    gs_rows = []
    for g in range(N_GROUPS):
        blk = sel[g * GROUP_SIZE:(g + 1) * GROUP_SIZE, :]
        m1, a1 = _first_argmax_rows(blk, in_idx)
        m2 = jnp.max(jnp.where(in_idx == a1, neg, blk), axis=0, keepdims=True)
        gs_rows.append(m1 + m2)
    gs = jnp.concatenate(gs_rows, axis=0)

    gsel = jnp.zeros((N_GROUPS, tr), jnp.float32)
    for _ in range(TOPK_GROUPS):
        _, a = _first_argmax_rows(gs, g_idx)
        pick = g_idx == a
        gsel = jnp.where(pick, 1.0, gsel)
        gs = jnp.where(pick, neg, gs)
    esel = jnp.concatenate(
        [jnp.broadcast_to(gsel[g:g + 1, :], (GROUP_SIZE, tr)) for g in range(N_GROUPS)], axis=0)
    cand = jnp.where(esel > 0.5, sel, neg)

    picks, gates = [], []
    chosen = jnp.zeros((N_EXPERTS, tr), jnp.float32)
    for _ in range(TOP_K):
        _, a = _first_argmax_rows(cand, e_idx)
        pick = e_idx == a
        picks.append(a)
        gates.append(jnp.sum(jnp.where(pick, scores, 0.0), axis=0, keepdims=True))
        chosen = jnp.where(pick, 1.0, chosen)
        cand = jnp.where(pick, neg, cand)
    gate = jnp.concatenate(gates, axis=0)
    gate = gate / jnp.sum(gate, axis=0, keepdims=True) * ROUTE_SCALE

    r_idx = lax.broadcasted_iota(jnp.int32, (tr, tr), 0)
    c_idx = lax.broadcasted_iota(jnp.int32, (tr, tr), 1)
    upper = jnp.where(r_idx <= c_idx, 1.0, 0.0).astype(jnp.bfloat16)
    incl = jnp.dot(chosen.astype(jnp.bfloat16), upper, preferred_element_type=jnp.float32)
    rank_all = carry_ref[...] + incl - chosen
    carry_ref[...] = carry_ref[...] + incl[:, tr - 1:tr]
    ranks = [jnp.sum(jnp.where(e_idx == a, rank_all, 0.0), axis=0, keepdims=True) for a in picks]

    eidx_ref[...] = jnp.concatenate(picks, axis=0)
    rank_ref[...] = jnp.concatenate(ranks, axis=0).astype(jnp.int32)
    pad = jnp.zeros((LANES - TOP_K, tr), jnp.float32)
    gate_t_ref[...] = jnp.concatenate([gate, pad], axis=0).T
    cnt_ref[...] = jnp.broadcast_to(carry_ref[...], (N_EXPERTS, LANES)).astype(jnp.int32)


def _router(x1, w_r, r_bias, tr):
    n_tok, d = x1.shape
    wt = w_r.T
    wh = wt.astype(jnp.bfloat16)
    wl = (wt - wh.astype(jnp.float32)).astype(jnp.bfloat16)
    col = lambda i: (0, i)
    return pl.pallas_call(
        functools.partial(_router_kernel, tr=tr),
        grid=(n_tok // tr,),
        in_specs=[
            pl.BlockSpec((tr, d), lambda i: (i, 0)),
            _const_spec((N_EXPERTS, d)),
            _const_spec((N_EXPERTS, d)),
            _const_spec((N_EXPERTS, 1)),
        ],
        out_specs=[
            pl.BlockSpec((TOP_K, tr), col),
            pl.BlockSpec((tr, LANES), lambda i: (i, 0)),
            pl.BlockSpec((TOP_K, tr), col),
            pl.BlockSpec((N_EXPERTS, LANES), lambda i: (0, 0)),
        ],
        out_shape=(
            jax.ShapeDtypeStruct((TOP_K, n_tok), jnp.int32),
            jax.ShapeDtypeStruct((n_tok, LANES), jnp.float32),
            jax.ShapeDtypeStruct((TOP_K, n_tok), jnp.int32),
            jax.ShapeDtypeStruct((N_EXPERTS, LANES), jnp.int32),
        ),
        scratch_shapes=[pltpu.VMEM((N_EXPERTS, 1), jnp.float32)],
        compiler_params=_params(("arbitrary",), 32 << 20),
        name="router",
    )(x1, wh, wl, r_bias.reshape(N_EXPERTS, 1))


def _dest_kernel(pstart_ref, eidx_ref, rank_ref, dest_ref):
    e = eidx_ref[...]
    base = jnp.zeros(e.shape, jnp.int32)
    for k in range(N_EXPERTS):
        base = jnp.where(e == k, pstart_ref[k], base)
    dest_ref[...] = base + rank_ref[...]


def _dest_rows(pstart, eidx, rank, tc):
    n_tok = eidx.shape[1]
    col = lambda i, ps: (0, i)
    return pl.pallas_call(
        _dest_kernel,
        grid_spec=pltpu.PrefetchScalarGridSpec(
            num_scalar_prefetch=1,
            grid=(n_tok // tc,),
            in_specs=[pl.BlockSpec((TOP_K, tc), col), pl.BlockSpec((TOP_K, tc), col)],
            out_specs=pl.BlockSpec((TOP_K, tc), col),
        ),
        out_shape=jax.ShapeDtypeStruct((TOP_K, n_tok), jnp.int32),
        compiler_params=_params(("arbitrary",), 32 << 20),
        name="dest_rows",
    )(pstart, eidx, rank)


def _dispatch(dest, x1p, n_rows):
    n_tok, width = x1p.shape
    c = SC_DISPATCH_CHUNK
    n_workers = SC_CORES_V7X * SC_SUBCORES_V7X
    n_chunks = n_tok // c
    assert n_tok % c == 0 and n_chunks % n_workers == 0
    per_worker = n_chunks // n_workers
    idx = dest.reshape(TOP_K, n_chunks, c).transpose(1, 0, 2)
    mesh = plsc.VectorSubcoreMesh(core_axis_name="c", subcore_axis_name="s")

    @functools.partial(
        pl.kernel, mesh=mesh,
        out_type=jax.ShapeDtypeStruct((n_rows, width), x1p.dtype),
        scratch_types=[pltpu.VMEM((TOP_K, c), jnp.int32), pltpu.VMEM((c, width), x1p.dtype), pltpu.SemaphoreType.DMA],
        name="dispatch",
    )
    def scatter_rows(idx_hbm, x_hbm, xs_hbm, idx_v, rows_v, sem):
        worker = lax.axis_index("s") * SC_CORES_V7X + lax.axis_index("c")

        @pl.loop(0, per_worker)
        def _(g):
            chunk = worker * per_worker + g
            pltpu.sync_copy(idx_hbm.at[chunk], idx_v)
            pltpu.sync_copy(x_hbm.at[pl.ds(chunk * c, c)], rows_v)
            copies = [pltpu.make_async_copy(rows_v, xs_hbm.at[idx_v.at[j]], sem) for j in range(TOP_K)]
            for cp in copies:
                cp.start()
            for cp in copies:
                cp.wait()

    return scatter_rows(idx, x1p)


def _experts_kernel(be_ref, nv_ref, rows_ref, xs_ref, w1_ref, w3_ref, w2_ref, ys_ref, xb_ref, w1b_ref, w3b_ref,
                    w2b_ref):
    b = pl.program_id(0)

    @pl.when(b < nv_ref[0])
    def _():
        @pl.when((b == 0) | (be_ref[b] != be_ref[jnp.maximum(b - 1, 0)]))
        def _():
            w1b_ref[...] = w1_ref[0].astype(jnp.bfloat16)
            w3b_ref[...] = w3_ref[0].astype(jnp.bfloat16)
            w2b_ref[...] = w2_ref[0].astype(jnp.bfloat16)

        bm, half = xs_ref.shape
        routed = lax.broadcasted_iota(jnp.int32, (bm, half), 0) < rows_ref[b]
        lo, hi = _unpack_bf16_pairs(jnp.where(routed, xs_ref[...], jnp.uint32(0)))
        xb_ref[:, :half] = lo
        xb_ref[:, half:] = hi
        xb = xb_ref[...]
        h1 = jnp.dot(xb, w1b_ref[...], preferred_element_type=jnp.float32)
        h3 = jnp.dot(xb, w3b_ref[...], preferred_element_type=jnp.float32)
        h = (_silu(h1) * h3).astype(jnp.bfloat16)
        ys_ref[...] = _pack_bf16_pairs(jnp.dot(h, w2b_ref[...], preferred_element_type=jnp.float32))

    @pl.when(b >= nv_ref[0])
    def _():
        ys_ref[...] = jnp.zeros_like(ys_ref)


def _experts(blk_exp, n_valid, blk_rows, xs, w1, w3, w2, layer, bm):
    n_rows, half = xs.shape
    d = 2 * half
    de = w1.shape[3]
    n_blk = n_rows // bm
    blk = lambda b, be, nv, br: (jnp.minimum(b, nv[0] - 1), 0)
    wsel = lambda b, be, nv, br: (layer, be[jnp.minimum(b, nv[0] - 1)], 0, 0)
    vmem = (2 * bm * half * 4 + bm * d * 2 + 2 * 3 * d * de * 4 + 3 * d * de * 2 + 2 * bm * half * 4
            + 4 * bm * de * 4 + 2 * bm * d * 4)
    return pl.pallas_call(
        _experts_kernel,
        grid_spec=pltpu.PrefetchScalarGridSpec(
            num_scalar_prefetch=3,
            grid=(n_blk,),
            in_specs=[
                pl.BlockSpec((bm, half), blk),
                pl.BlockSpec((None, 1, d, de), wsel),
                pl.BlockSpec((None, 1, d, de), wsel),
                pl.BlockSpec((None, 1, de, d), wsel),
            ],
            out_specs=pl.BlockSpec((bm, half), lambda b, be, nv, br: (b, 0)),
            scratch_shapes=[pltpu.VMEM((bm, d), jnp.bfloat16), pltpu.VMEM((d, de), jnp.bfloat16),
                            pltpu.VMEM((d, de), jnp.bfloat16), pltpu.VMEM((de, d), jnp.bfloat16)],
        ),
        out_shape=jax.ShapeDtypeStruct((n_rows, half), jnp.uint32),
        compiler_params=_params(("arbitrary",), vmem + (8 << 20)),
        name="experts",
    )(blk_exp, n_valid, blk_rows, xs, w1, w3, w2)


def _gather_rows(idx, ys):
    n_groups, n, c = idx.shape
    width = ys.shape[1]
    n_workers = SC_CORES_V7X * SC_SUBCORES_V7X
    assert n_groups % n_workers == 0
    per_worker = n_groups // n_workers
    mesh = plsc.VectorSubcoreMesh(core_axis_name="c", subcore_axis_name="s")

    @functools.partial(
        pl.kernel, mesh=mesh,
        out_type=jax.ShapeDtypeStruct((n_groups * n * c, width), ys.dtype),
        scratch_types=[pltpu.VMEM((n, c), jnp.int32), pltpu.VMEM((c, width), ys.dtype), pltpu.SemaphoreType.DMA],
        name="combine_gather",
    )
    def gather(idx_hbm, ys_hbm, out_hbm, idx_v, rows_v, sem):
        worker = lax.axis_index("s") * SC_CORES_V7X + lax.axis_index("c")

        @pl.loop(0, per_worker)
        def _(g):
            group = worker * per_worker + g
            pltpu.sync_copy(idx_hbm.at[group], idx_v)
            for j in range(n):
                pltpu.async_copy(ys_hbm.at[idx_v.at[j]], rows_v, sem).wait()
                pltpu.sync_copy(rows_v, out_hbm.at[pl.ds((group * n + j) * c, c)])

    return gather(idx, ys)


def _combine_kernel(*refs, tm, alpha, aliased):
    refs = refs[1:] if aliased else refs
    (dest_ref, dest_next_ref, x1_ref, p_ref, gt_ref, w1_ref, w3_ref, w2_ref, wpg_ref, bpg_ref, wpe_ref, g_ref, b_ref,
     yb_ref, ys_ref, o_ref, ybuf_ref, sems) = refs
    i = pl.program_id(0)
    slot = lax.rem(i, 2)
    other = 1 - slot
    tc_picks = range(SC_GATHER_PICKS, TOP_K)

    def row_copy(d_ref, s, j, r):
        return pltpu.make_async_copy(ys_ref.at[pl.ds(d_ref[j, r], 1), :],
                                     ybuf_ref.at[s, j - SC_GATHER_PICKS, pl.ds(r, 1), :], sems.at[s])

    def drain(s):
        for j in tc_picks:
            pltpu.make_async_copy(ys_ref.at[pl.ds(0, tm), :], ybuf_ref.at[s, j - SC_GATHER_PICKS], sems.at[s]).wait()

    @pl.when(i == 0)
    def _():
        def first_tile(r, carry):
            for j in tc_picks:
                row_copy(dest_ref, 0, j, r).start()
            return carry

        lax.fori_loop(0, tm, first_tile, 0)

    for r in range(tm):
        for j in tc_picks:
            row_copy(dest_next_ref, other, j, r).start(priority=j % 2)

    x1 = x1_ref[...]
    xb = x1.astype(jnp.bfloat16)
    h1 = jnp.dot(xb, w1_ref[...], preferred_element_type=jnp.float32)
    h3 = jnp.dot(xb, w3_ref[...], preferred_element_type=jnp.float32)
    h = (_silu(h1) * h3).astype(jnp.bfloat16)
    total = alpha * x1 + jnp.dot(h, w2_ref[...], preferred_element_type=jnp.float32)
    gate = _sigmoid(jnp.dot(xb, wpg_ref[...], preferred_element_type=jnp.float32) + bpg_ref[...])
    pe = jnp.dot(p_ref[...].astype(jnp.bfloat16), wpe_ref[...], preferred_element_type=jnp.float32)
    total = total + gate * pe

    drain(slot)
    gt = gt_ref[...]
    half = yb_ref.shape[3]
    moe_lo = jnp.zeros((tm, half), jnp.float32)
    moe_hi = jnp.zeros((tm, half), jnp.float32)
    for j in range(TOP_K):
        w = yb_ref[0, j] if j < SC_GATHER_PICKS else ybuf_ref[slot, j - SC_GATHER_PICKS]
        g = gt[:, j:j + 1]
        moe_lo = moe_lo + g * lax.bitcast_convert_type(w << 16, jnp.float32)
        moe_hi = moe_hi + g * lax.bitcast_convert_type(w & jnp.uint32(0xFFFF0000), jnp.float32)
    total = total + jnp.concatenate([moe_lo, moe_hi], axis=1)
    o_ref[...] = _layer_norm_rows(total, g_ref[...], b_ref[...])

    @pl.when(i == pl.num_programs(0) - 1)
    def _():
        drain(other)


def _combine(dest, x1, p2, gate_t, w_sh1, w_sh3, w_sh2, w_pg, b_pg, w_pe, ln_g, ln_b, yb, ys, alpha, tm, first_tile,
             out_buf):
    n_tok, d = x1.shape
    ds = w_sh1.shape[1]
    dp = p2.shape[1]
    n_tiles = yb.shape[0]
    row = lambda i: (first_tile + i, 0)
    vmem = (2 * TOP_K * tm * (d // 2) * 4 + (3 * d * ds + d * d + dp * d) * 2 + 2 * 2 * tm * d * 4 + 2 * tm * dp * 4
            + 2 * tm * LANES * 4 + 6 * tm * d * 4)
    aliased = out_buf is not None
    in_specs = [
        pl.BlockSpec((TOP_K, tm), lambda i: (0, first_tile + i), memory_space=pltpu.SMEM),
        pl.BlockSpec((TOP_K, tm), lambda i: (0, first_tile + jnp.minimum(i + 1, n_tiles - 1)),
                     memory_space=pltpu.SMEM),
        pl.BlockSpec((tm, d), row),
        pl.BlockSpec((tm, dp), row),
        pl.BlockSpec((tm, LANES), row),
        _const_spec((d, ds)),
        _const_spec((d, ds)),
        _const_spec((ds, d)),
        _const_spec((d, d)),
        _const_spec((1, d)),
        _const_spec((dp, d)),
        _const_spec((1, d)),
        _const_spec((1, d)),
        pl.BlockSpec((1, SC_GATHER_PICKS, tm, d // 2), lambda i: (i, 0, 0, 0)),
        pl.BlockSpec(memory_space=pl.ANY),
    ]
    args = [dest, dest, x1, p2, gate_t, w_sh1, w_sh3, w_sh2, w_pg, b_pg.reshape(1, d), w_pe, ln_g.reshape(1, d),
            ln_b.reshape(1, d), yb, ys]
    if aliased:
        in_specs = [pl.BlockSpec(memory_space=pl.ANY)] + in_specs
        args = [out_buf] + args
    return pl.pallas_call(
        functools.partial(_combine_kernel, tm=tm, alpha=alpha, aliased=aliased),
        grid=(n_tiles,),
        in_specs=in_specs,
        out_specs=pl.BlockSpec((tm, d), row),
        out_shape=jax.ShapeDtypeStruct((n_tok, d), jnp.float32),
        scratch_shapes=[pltpu.VMEM((2, TOP_K - SC_GATHER_PICKS, tm, d // 2), jnp.uint32),
                        pltpu.SemaphoreType.DMA((2,))],
        input_output_aliases={0: 0} if aliased else {},
        compiler_params=_params(("arbitrary",), vmem + (8 << 20)),
        name="combine",
    )(*args)


def _tile(n, target):
    t = min(n, target)
    assert n % t == 0, (n, t)
    return t


def kernel(x, p, w_in, b_f, conv_w, conv_b, conv_ln_g, conv_ln_b, gmlp_ln_g, gmlp_ln_b, w_sp, b_sp, out_g, w_o,
           ln1_g, ln1_b, w_r, r_bias, w_e1, w_e3, w_e2, w_sh1, w_sh3, w_sh2, w_pe, w_pg, b_pg, ln2_g, ln2_b):
    bsz, seq, d = x.shape
    depth = w_in.shape[0]
    n_tok = bsz * seq
    alpha = (2.0 * depth) ** 0.25
    n_main = 2 * CONV_CH + 2 * GMLP_CH + 3 * ATT_CH
    assert w_in.shape[2] == n_main + ATT_HEADS and seq % CHUNK == 0

    bm = MOE_BLOCK
    n_blk = -(-n_tok * TOP_K // bm) + N_EXPERTS
    n_rows = n_blk * bm
    bf16 = jnp.bfloat16

    x2 = x.reshape(n_tok, d)
    for i in range(depth):
        w_main = w_in[i, :, :n_main].astype(bf16)
        w_f = jnp.pad(w_in[i, :, n_main:], ((0, 0), (0, LANES - ATT_HEADS))).astype(bf16)
        b_f_row = jnp.pad(b_f[i], (0, LANES - ATT_HEADS)).reshape(1, LANES)

        zc, zg, q, k, v, f = _in_projection(x2, w_main, w_f, _tile(n_tok, 256))
        ya = _conv_module(zc, conv_w[i], conv_b[i], conv_ln_g[i], conv_ln_b[i], out_g[i, :CONV_CH],
                          bsz, seq, _tile(seq, 256))
        yb = _gmlp_module(zg, gmlp_ln_g[i], gmlp_ln_b[i], w_sp[i], b_sp[i],
                          out_g[i, CONV_CH:CONV_CH + GMLP_CH], _tile(n_tok, 512))
        c = _forget_cumsum(f, b_f_row, bsz, seq, _tile(seq, 512))
        tq = _tile(seq, 2048)
        yc = _fox_attention(q, k, v, c, bsz, seq, tq, _tile(tq, 512), _tile(tq, 512))
        x1, x1p = _out_projection(ya, yb, yc, x2, w_o[i].astype(bf16), out_g[i, CONV_CH + GMLP_CH:],
                                  ln1_g[i], ln1_b[i], alpha, _tile(n_tok, 256))

        eidx, gate_t, rank, cnt = _router(x1, w_r[i], r_bias[i], _tile(n_tok, 512))
        counts = cnt[:, 0]
        padded = (counts + bm - 1) // bm * bm
        pend = jnp.cumsum(padded).astype(jnp.int32)
        pstart = pend - padded
        blk_start = jnp.arange(n_blk, dtype=jnp.int32) * bm
        blk_exp = jnp.minimum(jnp.sum(pend[None, :] <= blk_start[:, None], axis=1), N_EXPERTS - 1).astype(jnp.int32)
        n_valid = (pend[N_EXPERTS - 1:] // bm).astype(jnp.int32)
        blk_rows = jnp.clip((pstart + counts)[blk_exp] - blk_start, 0, bm).astype(jnp.int32)

        dest = _dest_rows(pstart, eidx, rank, _tile(n_tok, 4096))
        xs = _dispatch(dest, x1p, n_rows)
        ys = _experts(blk_exp, n_valid, blk_rows, xs, w_e1, w_e3, w_e2, i, bm)
        tm = _tile(n_tok, 256)
        n_tiles = n_tok // tm
        gather_idx = (dest[:SC_GATHER_PICKS].reshape(SC_GATHER_PICKS, n_tiles, tm).transpose(1, 0, 2)
                      .reshape(-1, SC_GATHER_PICKS, SC_DISPATCH_CHUNK))
        groups = gather_idx.shape[0] // COMBINE_SPLITS
        tiles = n_tiles // COMBINE_SPLITS
        x2 = None
        for part in range(COMBINE_SPLITS):
            yb = _gather_rows(gather_idx[part * groups:(part + 1) * groups], ys)
            x2 = _combine(dest, x1, p[i].reshape(n_tok, -1), gate_t, w_sh1[i].astype(bf16), w_sh3[i].astype(bf16),
                          w_sh2[i].astype(bf16), w_pg[i].astype(bf16), b_pg[i], w_pe[i].astype(bf16),
                          ln2_g[i], ln2_b[i], yb.reshape(tiles, SC_GATHER_PICKS, tm, d // 2), ys, alpha, tm,
                          part * tiles, x2)
    return x2.reshape(bsz, seq, d)
```
